```python
import jax, jax.numpy as jnp
from jax import lax
import numpy as np

D_MODEL = 1024
BATCH = 2
SEQ = 8192
DEPTH = 2

HEAD_DIM = 64
GROUP_WIDTH = D_MODEL // 2
CONV_CH = GROUP_WIDTH
CONV_WIDTH = 31
NSA_HEADS = GROUP_WIDTH // HEAD_DIM
NSA_KV_GROUPS = 2
NSA_CMP_BLOCK = 32
NSA_CMP_STRIDE = 16
NSA_CMP_HIDDEN = 128
NSA_SEL_BLOCK = 64
NSA_TOP_N = 16
NSA_WINDOW = 512
NSA_N_BRANCH = 3
SC_CH = GROUP_WIDTH
SC_WIDTH = 3
SB_HEADS = GROUP_WIDTH // HEAD_DIM
D_FF = 2816
Q_BLOCK = 128
RMS_EPS = 1e-6
LN_EPS = 1e-5
NEG_INF = -1e30
SEL_FORCE = 1e4

kernel_name = "hybrid_conformer_nsa_shortconv_stickbreak"


def _rmsnorm(x, g):
    x32 = x.astype(jnp.float32)
    y = x32 * lax.rsqrt(jnp.mean(x32 * x32, axis=-1, keepdims=True) + RMS_EPS)
    return (y * g.astype(jnp.float32)).astype(x.dtype)


def _layernorm(x, g, b):
    x32 = x.astype(jnp.float32)
    mu = jnp.mean(x32, axis=-1, keepdims=True)
    var = jnp.mean(jnp.square(x32 - mu), axis=-1, keepdims=True)
    y = (x32 - mu) * lax.rsqrt(var + LN_EPS)
    return (y * g.astype(jnp.float32) + b.astype(jnp.float32)).astype(x.dtype)


def _swiglu(x, w_in, w_out):
    gate, up = jnp.split(x @ w_in, 2, axis=-1)
    return (jax.nn.silu(gate) * up) @ w_out


def _split(u, sizes):
    cuts = [int(c) for c in np.cumsum(sizes)[:-1]]
    return jnp.split(u, cuts, axis=-1)


def _causal_dwconv(x, w):
    k = w.shape[0]
    return lax.conv_general_dilated(
        x, w[:, None, :].astype(x.dtype), window_strides=(1,), padding=[(k - 1, 0)],
        dimension_numbers=('NWC', 'WIO', 'NWC'), feature_group_count=x.shape[-1])


def _alibi_slopes(n):
    return jnp.asarray(np.power(2.0, -8.0 * np.arange(1, n + 1) / n).astype(np.float32))


def _masked_softmax(s, mask):
    p = jax.nn.softmax(jnp.where(mask, s, NEG_INF), axis=-1)
    return jnp.where(mask, p, 0.0)


def _nsa_compress(kx, pe, w1, w2):
    b, t, g, d = kx.shape
    chunks = kx.reshape(b, t // NSA_CMP_STRIDE, NSA_CMP_STRIDE, g, d)
    blocks = jnp.concatenate([chunks[:, :-1], chunks[:, 1:]], axis=2)
    h = jax.nn.gelu(jnp.einsum('bnlgd,ldf->bngf', blocks + pe[:, None, :], w1))
    return jnp.einsum('bngf,fe->bnge', h, w2)


def _nsa(q, kc, vc, ks, vs, kw, vw, gates, pe_k, w1_k, w2_k, pe_v, w1_v, w2_v):
    B, T, H, D = q.shape
    G = kc.shape[2]
    R = H // G
    scale = D ** -0.5
    k_cmp = _nsa_compress(kc, pe_k, w1_k, w2_k)
    v_cmp = _nsa_compress(vc, pe_v, w1_v, w2_v)
    NC = k_cmp.shape[1]
    cmp_start = jnp.arange(NC) * NSA_CMP_STRIDE
    cmp_end = cmp_start + NSA_CMP_BLOCK - 1
    cmp_centre = cmp_start.astype(jnp.float32) + (NSA_CMP_BLOCK - 1) / 2
    NS = T // NSA_SEL_BLOCK
    n_top = min(NSA_TOP_N, NS)
    ratio = NSA_SEL_BLOCK // NSA_CMP_STRIDE
    blk = jnp.arange(NS)
    slopes = _alibi_slopes(H).reshape(G, R)[None, None, :, :, None]
    ks_t = ks.transpose(0, 2, 1, 3)
    vs_t = vs.transpose(0, 2, 1, 3)
    kw_pad = jnp.pad(kw, ((0, 0), (NSA_WINDOW, 0), (0, 0), (0, 0)))
    vw_pad = jnp.pad(vw, ((0, 0), (NSA_WINDOW, 0), (0, 0), (0, 0)))
    b_idx = jnp.arange(B)[:, None, None]
    g_idx = jnp.arange(G)[None, :, None]
    win_off = jnp.arange(NSA_WINDOW + Q_BLOCK) - NSA_WINDOW
    sel_off = jnp.arange(NSA_SEL_BLOCK)

    def block(i):
        q0 = i * Q_BLOCK
        t = q0 + jnp.arange(Q_BLOCK)
        tf = t.astype(jnp.float32)
        qg = lax.dynamic_slice_in_dim(q, q0, Q_BLOCK, axis=1).reshape(B, Q_BLOCK, G, R, D)
        gq = lax.dynamic_slice_in_dim(gates, q0, Q_BLOCK, axis=1).reshape(B, Q_BLOCK, G, R, NSA_N_BRANCH)
        s_c = jnp.einsum('bqgrd,bngd->bqgrn', qg, k_cmp).astype(jnp.float32) * scale
        s_c = s_c - slopes * (tf[:, None] - cmp_centre[None, :])[None, :, None, None, :]
        p_c = _masked_softmax(s_c, (cmp_end[None, :] <= t[:, None])[None, :, None, None, :])
        o_c = jnp.einsum('bqgrn,bngd->bqgrd', p_c.astype(v_cmp.dtype), v_cmp)
        imp = jnp.pad(p_c.sum(axis=3), ((0, 0), (0, 0), (0, 0), (1, 1)))
        imp = (imp[..., :ratio * NS].reshape(B, Q_BLOCK, G, NS, ratio).sum(-1)
               + imp[..., ratio::ratio][..., :NS])
        cur = t // NSA_SEL_BLOCK
        visible = blk[None, :] * NSA_SEL_BLOCK <= t[:, None]
        forced = (blk[None, :] == 0) | (blk[None, :] == cur[:, None]) | (blk[None, :] == cur[:, None] - 1)
        score = jnp.where(visible[None, :, None, :],
                          jnp.where(forced[None, :, None, :], SEL_FORCE, imp), -1.0)
        top_val, top_idx = lax.top_k(score, n_top)
        tok = (top_idx[..., None] * NSA_SEL_BLOCK + sel_off).reshape(B, Q_BLOCK, G, n_top * NSA_SEL_BLOCK)
        tok_valid = jnp.repeat(top_val >= 0, NSA_SEL_BLOCK, axis=-1) & (tok <= t[None, :, None, None])
        tok_g = tok.transpose(0, 2, 1, 3).reshape(B, G, -1)
        k_sel = ks_t[b_idx, g_idx, tok_g].reshape(B, G, Q_BLOCK, -1, D)
        v_sel = vs_t[b_idx, g_idx, tok_g].reshape(B, G, Q_BLOCK, -1, D)
        s_s = jnp.einsum('bqgrd,bgqnd->bqgrn', qg, k_sel).astype(jnp.float32) * scale
        s_s = s_s - slopes * (tf[None, :, None, None] - tok.astype(jnp.float32))[:, :, :, None, :]
        p_s = _masked_softmax(s_s, tok_valid[:, :, :, None, :])
        o_s = jnp.einsum('bqgrn,bgqnd->bqgrd', p_s.astype(v_sel.dtype), v_sel)
        kwb = lax.dynamic_slice_in_dim(kw_pad, q0, NSA_WINDOW + Q_BLOCK, axis=1)
        vwb = lax.dynamic_slice_in_dim(vw_pad, q0, NSA_WINDOW + Q_BLOCK, axis=1)
        spos = q0 + win_off
        dist = t[:, None] - spos[None, :]
        mask_w = (spos[None, :] >= 0) & (dist >= 0) & (dist < NSA_WINDOW)
        s_w = jnp.einsum('bqgrd,bsgd->bqgrs', qg, kwb).astype(jnp.float32) * scale
        s_w = s_w - slopes * dist.astype(jnp.float32)[None, :, None, None, :]
        p_w = _masked_softmax(s_w, mask_w[None, :, None, None, :])
        o_w = jnp.einsum('bqgrs,bsgd->bqgrd', p_w.astype(vwb.dtype), vwb)
        o = gq[..., 0:1] * o_c + gq[..., 1:2] * o_s + gq[..., 2:3] * o_w
        return o.reshape(B, Q_BLOCK, H, D)

    out = lax.map(block, jnp.arange(T // Q_BLOCK))
    return out.transpose(1, 0, 2, 3, 4).reshape(B, T, H * D)


def _stick_breaking(q, k, v):
    B, T, H, D = q.shape
    scale = D ** -0.5
    s_pos = jnp.arange(T)

    def block(i):
        q0 = i * Q_BLOCK
        t = q0 + jnp.arange(Q_BLOCK)
        qb = lax.dynamic_slice_in_dim(q, q0, Q_BLOCK, axis=1)
        z = jnp.einsum('bqhd,bshd->bhqs', qb, k).astype(jnp.float32) * scale
        mask = (s_pos[None, :] < t[:, None])[None, None]
        log_keep = jnp.where(mask, jax.nn.log_sigmoid(-z), 0.0)
        log_a = jax.nn.log_sigmoid(z) + lax.cumsum(log_keep, axis=3, reverse=True) - log_keep
        a = jnp.where(mask, jnp.exp(log_a), 0.0)
        return jnp.einsum('bhqs,bshd->bqhd', a.astype(v.dtype), v)

    out = lax.map(block, jnp.arange(T // Q_BLOCK))
    return out.transpose(1, 0, 2, 3, 4).reshape(B, T, H * D)


def _mixer_conv_nsa(h, w_in, dw_w, dw_b, ln_g, ln_b, pe_k, w1_k, w2_k, pe_v, w1_v, w2_v, w_out):
    B, T, _ = h.shape
    kv = NSA_KV_GROUPS * HEAD_DIM
    sizes = [CONV_CH, CONV_CH, NSA_HEADS * HEAD_DIM] + [kv] * 6 + [NSA_HEADS * NSA_N_BRANCH]
    a_val, a_gate, q, kc, vc, ks, vs, kw, vw, g = _split(h @ w_in, sizes)
    a = a_val * jax.nn.sigmoid(a_gate)
    a = _causal_dwconv(a, dw_w) + dw_b
    a = jax.nn.silu(_layernorm(a, ln_g, ln_b))
    heads = lambda u, n: u.reshape(B, T, n, HEAD_DIM)
    G = NSA_KV_GROUPS
    o_nsa = _nsa(heads(q, NSA_HEADS), heads(kc, G), heads(vc, G), heads(ks, G), heads(vs, G),
                 heads(kw, G), heads(vw, G),
                 jax.nn.sigmoid(g).reshape(B, T, NSA_HEADS, NSA_N_BRANCH),
                 pe_k, w1_k, w2_k, pe_v, w1_v, w2_v)
    return jnp.concatenate([a, o_nsa], axis=-1) @ w_out


def _mixer_shortconv_sb(h, w_in, sc_w, w_out):
    B, T, _ = h.shape
    sizes = [SC_CH] * 3 + [SB_HEADS * HEAD_DIM] * 3
    bg, cg, u, q, k, v = _split(h @ w_in, sizes)
    c_out = bg * _causal_dwconv(cg * u, sc_w)
    heads = lambda z: z.reshape(B, T, SB_HEADS, HEAD_DIM)
    o_sb = _stick_breaking(heads(q), heads(k), heads(v))
    return jnp.concatenate([c_out, o_sb], axis=-1) @ w_out


def setup_inputs(seed: int = 0) -> dict:
    key = jax.random.key(seed)
    keys = iter(jax.random.split(key, 40))
    n_even = (DEPTH + 1) // 2
    n_odd = DEPTH // 2
    def nrm(shape, fan_in):
        return jax.random.normal(next(keys), shape, jnp.float32) * (fan_in ** -0.5)
    def gain(shape):
        return 1.0 + 0.01 * jax.random.normal(next(keys), shape, jnp.float32)
    def small(shape, s=0.01):
        return s * jax.random.normal(next(keys), shape, jnp.float32)
    ab_in = 2 * CONV_CH + NSA_HEADS * HEAD_DIM + 6 * NSA_KV_GROUPS * HEAD_DIM + NSA_HEADS * NSA_N_BRANCH
    ab_out = CONV_CH + NSA_HEADS * HEAD_DIM
    cd_in = 3 * SC_CH + 3 * SB_HEADS * HEAD_DIM
    cd_out = SC_CH + SB_HEADS * HEAD_DIM
    cmp_fan = NSA_CMP_BLOCK * HEAD_DIM
    return {
        'x': jax.random.normal(next(keys), (BATCH, SEQ, D_MODEL), jnp.float32),
        'ffn1_norm': gain((DEPTH, D_MODEL)),
        'ffn1_w_in': nrm((DEPTH, D_MODEL, 2 * D_FF), D_MODEL),
        'ffn1_w_out': nrm((DEPTH, D_FF, D_MODEL), D_FF),
        'mix_norm': gain((DEPTH, D_MODEL)),
        'ffn2_norm': gain((DEPTH, D_MODEL)),
        'ffn2_w_in': nrm((DEPTH, D_MODEL, 2 * D_FF), D_MODEL),
        'ffn2_w_out': nrm((DEPTH, D_FF, D_MODEL), D_FF),
        'ab_w_in': nrm((n_even, D_MODEL, ab_in), D_MODEL),
        'conv_dw_w': nrm((n_even, CONV_WIDTH, CONV_CH), CONV_WIDTH),
        'conv_dw_b': small((n_even, CONV_CH)),
        'conv_ln_g': gain((n_even, CONV_CH)),
        'conv_ln_b': small((n_even, CONV_CH)),
        'nsa_pe_k': small((n_even, NSA_CMP_BLOCK, HEAD_DIM), 0.02),
        'nsa_w1_k': nrm((n_even, NSA_CMP_BLOCK, HEAD_DIM, NSA_CMP_HIDDEN), cmp_fan),
        'nsa_w2_k': nrm((n_even, NSA_CMP_HIDDEN, HEAD_DIM), NSA_CMP_HIDDEN),
        'nsa_pe_v': small((n_even, NSA_CMP_BLOCK, HEAD_DIM), 0.02),
        'nsa_w1_v': nrm((n_even, NSA_CMP_BLOCK, HEAD_DIM, NSA_CMP_HIDDEN), cmp_fan),
        'nsa_w2_v': nrm((n_even, NSA_CMP_HIDDEN, HEAD_DIM), NSA_CMP_HIDDEN),
        'ab_w_out': nrm((n_even, ab_out, D_MODEL), ab_out),
        'cd_w_in': nrm((n_odd, D_MODEL, cd_in), D_MODEL),
        'sc_conv_w': nrm((n_odd, SC_WIDTH, SC_CH), SC_WIDTH),
        'cd_w_out': nrm((n_odd, cd_out, D_MODEL), cd_out),
        'final_norm': gain((D_MODEL,)),
    }


def reference(x, ffn1_norm, ffn1_w_in, ffn1_w_out, mix_norm, ffn2_norm, ffn2_w_in, ffn2_w_out,
              ab_w_in, conv_dw_w, conv_dw_b, conv_ln_g, conv_ln_b,
              nsa_pe_k, nsa_w1_k, nsa_w2_k, nsa_pe_v, nsa_w1_v, nsa_w2_v, ab_w_out,
              cd_w_in, sc_conv_w, cd_w_out, final_norm):
    for layer in range(DEPTH):
        x = x + 0.5 * _swiglu(_rmsnorm(x, ffn1_norm[layer]), ffn1_w_in[layer], ffn1_w_out[layer])
        h = _rmsnorm(x, mix_norm[layer])
        if layer % 2 == 0:
            e = layer // 2
            y = _mixer_conv_nsa(h, ab_w_in[e], conv_dw_w[e], conv_dw_b[e], conv_ln_g[e], conv_ln_b[e],
                                nsa_pe_k[e], nsa_w1_k[e], nsa_w2_k[e],
                                nsa_pe_v[e], nsa_w1_v[e], nsa_w2_v[e], ab_w_out[e])
        else:
            o = layer // 2
            y = _mixer_shortconv_sb(h, cd_w_in[o], sc_conv_w[o], cd_w_out[o])
        x = x + y
        x = x + 0.5 * _swiglu(_rmsnorm(x, ffn2_norm[layer]), ffn2_w_in[layer], ffn2_w_out[layer])
    return _rmsnorm(x, final_norm)
```

```python
import functools

import numpy as np
import jax
import jax.numpy as jnp
from jax import lax
from jax.experimental import pallas as pl
from jax.experimental.pallas import tpu as pltpu

F32 = jnp.float32
BF16 = jnp.bfloat16

D_MODEL = 1024
HEAD_DIM = 64
GROUP_WIDTH = D_MODEL // 2
CONV_WIDTH = 31
NSA_HEADS = GROUP_WIDTH // HEAD_DIM
NSA_KV_GROUPS = 2
NSA_REP = NSA_HEADS // NSA_KV_GROUPS
NSA_CMP_BLOCK = 32
NSA_CMP_STRIDE = 16
NSA_CMP_HIDDEN = 128
NSA_SEL_BLOCK = 64
NSA_TOP_N = 16
NSA_WINDOW = 512
NSA_N_BRANCH = 3
SC_WIDTH = 3
SB_HEADS = GROUP_WIDTH // HEAD_DIM
D_FF = 2816
RMS_EPS = 1e-6
LN_EPS = 1e-5
NEG_INF = -1e30
SEL_FORCE = 1e4
QK_SCALE = HEAD_DIM ** -0.5

LANES = 128
VMEM_LIMIT = 48 * 1024 * 1024

FFN_TM = 1024
FFN_TF = 256
PROJ_TM = 512
OUT_TM = 1024
CONV_TT = 256
CONV_HALO = 32
CONV_CHUNK = 64
SC_TT = 512
SC_HALO = 8
NSA_TQ = 128
NSA_TK = 512
SB_T = 256


def _params(*sem):
    return pltpu.CompilerParams(dimension_semantics=sem, vmem_limit_bytes=VMEM_LIMIT)


def _dot(a, b):
    return jnp.dot(a, b, preferred_element_type=F32)


def _dot_nt(a, b):
    return lax.dot_general(a, b, (((1,), (1,)), ((), ())), preferred_element_type=F32)


def _sigmoid(x):
    return 1.0 / (1.0 + jnp.exp(-x))


def _rmsnorm_rows(x, g):
    return x * lax.rsqrt(jnp.mean(x * x, axis=-1, keepdims=True) + RMS_EPS) * g


def _ffn_kernel(x_ref, g_ref, wg_ref, wu_ref, wo_ref, *rest, n_ff, final_norm):
    if final_norm:
        fg_ref, o_ref, xn_ref, acc_ref = rest
    else:
        o_ref, xn_ref, acc_ref = rest
    j = pl.program_id(1)

    @pl.when(j == 0)
    def _():
        xn_ref[...] = _rmsnorm_rows(x_ref[...], g_ref[...]).astype(BF16)
        acc_ref[...] = jnp.zeros_like(acc_ref)

    xn = xn_ref[...]
    gate = _dot(xn, wg_ref[...])
    up = _dot(xn, wu_ref[...])
    h = gate * _sigmoid(gate) * up
    acc_ref[...] += _dot(h.astype(BF16), wo_ref[...])

    @pl.when(j == n_ff - 1)
    def _():
        y = x_ref[...] + 0.5 * acc_ref[...]
        if final_norm:
            y = _rmsnorm_rows(y, fg_ref[...])
        o_ref[...] = y


def _ffn(x, g, w_in, w_out, final_g=None):
    n, d = x.shape
    n_ff = D_FF // FFN_TF
    tm = min(FFN_TM, n)
    w_in = w_in.astype(BF16)
    w_out = w_out.astype(BF16)
    in_specs = [
        pl.BlockSpec((tm, d), lambda i, j: (i, 0)),
        pl.BlockSpec((1, d), lambda i, j: (0, 0)),
        pl.BlockSpec((d, FFN_TF), lambda i, j: (0, j)),
        pl.BlockSpec((d, FFN_TF), lambda i, j: (0, j + n_ff)),
        pl.BlockSpec((FFN_TF, d), lambda i, j: (j, 0)),
    ]
    args = [x, g.reshape(1, d), w_in, w_in, w_out]
    if final_g is not None:
        in_specs.append(pl.BlockSpec((1, d), lambda i, j: (0, 0)))
        args.append(final_g.reshape(1, d))
    return pl.pallas_call(
        functools.partial(_ffn_kernel, n_ff=n_ff, final_norm=final_g is not None),
        out_shape=jax.ShapeDtypeStruct((n, d), F32),
        grid=(n // tm, n_ff),
        in_specs=in_specs,
        out_specs=pl.BlockSpec((tm, d), lambda i, j: (i, 0)),
        scratch_shapes=[pltpu.VMEM((tm, d), BF16), pltpu.VMEM((tm, d), F32)],
        compiler_params=_params("parallel", "arbitrary"),
        name="ffn",
    )(*args)


AB_A = 0
AB_Q = 1024
AB_KC = 2048
AB_VC = 2176
AB_KS = 2304
AB_VS = 2560
AB_KW = 2816
AB_VW = 3072
AB_G = 3328
AB_COLS = 3584


def _proj_ab_kernel(x_ref, g_ref, w_ref, a_ref, q_ref, kc_ref, vc_ref, ks_ref, vs_ref, kw_ref, vw_ref,
                    gt_ref):
    xn = _rmsnorm_rows(x_ref[0], g_ref[...]).astype(BF16)

    def seg(lo, hi):
        return _dot(xn, w_ref[:, lo:hi])

    av = seg(AB_A, AB_A + 2 * GROUP_WIDTH)
    a_ref[0] = av[:, :GROUP_WIDTH] * _sigmoid(av[:, GROUP_WIDTH:])
    q_ref[0] = seg(AB_Q, AB_KC).astype(BF16)
    kc_ref[0] = seg(AB_KC, AB_VC)
    vc_ref[0] = seg(AB_VC, AB_KS)
    ks_ref[0] = seg(AB_KS, AB_VS).astype(BF16)
    vs_ref[0] = seg(AB_VS, AB_KW).astype(BF16)
    kw_ref[0] = seg(AB_KW, AB_VW).astype(BF16)
    vw_ref[0] = seg(AB_VW, AB_G).astype(BF16)
    gt_ref[0] = _sigmoid(seg(AB_G, AB_COLS))


def _arrange_ab_weight(w):
    d = w.shape[0]
    kvw = NSA_KV_GROUPS * HEAD_DIM
    o = 2 * GROUP_WIDTH
    a = w[:, :o]
    q = w[:, o:o + GROUP_WIDTH].reshape(d, NSA_HEADS, HEAD_DIM) * QK_SCALE
    o += GROUP_WIDTH
    kc, vc, ks, vs, kw, vw = [w[:, o + i * kvw:o + (i + 1) * kvw] for i in range(6)]
    o += 6 * kvw
    g = w[:, o:].reshape(d, NSA_KV_GROUPS, NSA_REP * NSA_N_BRANCH)

    zeros_h = jnp.zeros((d, NSA_HEADS, HEAD_DIM), w.dtype)
    q_pad = jnp.concatenate([q, zeros_h], axis=-1).reshape(d, NSA_HEADS * LANES)

    def k_pad(k):
        k = k.reshape(d, NSA_KV_GROUPS, HEAD_DIM)
        return jnp.concatenate([k, jnp.zeros_like(k)], axis=-1).reshape(d, NSA_KV_GROUPS * LANES)

    def v_dup(v):
        v = v.reshape(d, NSA_KV_GROUPS, HEAD_DIM)
        return jnp.concatenate([v, v], axis=-1).reshape(d, NSA_KV_GROUPS * LANES)

    g_pad = jnp.pad(g, ((0, 0), (0, 0), (0, LANES - g.shape[-1]))).reshape(d, NSA_KV_GROUPS * LANES)
    out = jnp.concatenate([a, q_pad, kc, vc, k_pad(ks), v_dup(vs), k_pad(kw), v_dup(vw), g_pad], axis=-1)
    assert out.shape[1] == AB_COLS
    return out.astype(BF16)


def _proj_ab(x, g, w):
    b, t, d = x.shape
    tm = min(PROJ_TM, t)
    wa = _arrange_ab_weight(w)

    def tok(width):
        return pl.BlockSpec((1, tm, width), lambda bi, i: (bi, i, 0))

    widths = [GROUP_WIDTH, NSA_HEADS * LANES, LANES, LANES] + [NSA_KV_GROUPS * LANES] * 5
    dtypes = [F32, BF16, F32, F32, BF16, BF16, BF16, BF16, F32]
    return pl.pallas_call(
        _proj_ab_kernel,
        out_shape=[jax.ShapeDtypeStruct((b, t, wd), dt) for wd, dt in zip(widths, dtypes)],
        grid=(b, t // tm),
        in_specs=[tok(d), pl.BlockSpec((1, d), lambda bi, i: (0, 0)),
                  pl.BlockSpec((d, AB_COLS), lambda bi, i: (0, 0))],
        out_specs=[tok(wd) for wd in widths],
        compiler_params=_params("parallel", "parallel"),
        name="proj_ab",
    )(x, g.reshape(1, d), wa)


CD_B = 0
CD_C = 512
CD_U = 1024
CD_Q = 1536
CD_K = 2560
CD_V = 3072
CD_COLS = 3584


def _proj_cd_kernel(x_ref, g_ref, w_ref, bg_ref, p_ref, q_ref, k_ref, v_ref):
    xn = _rmsnorm_rows(x_ref[0], g_ref[...]).astype(BF16)

    def seg(lo, hi):
        return _dot(xn, w_ref[:, lo:hi])

    bg_ref[0] = seg(CD_B, CD_C)
    p_ref[0] = seg(CD_C, CD_U) * seg(CD_U, CD_Q)
    q_ref[0] = seg(CD_Q, CD_K).astype(BF16)
    k_ref[0] = seg(CD_K, CD_V).astype(BF16)
    v_ref[0] = seg(CD_V, CD_COLS).astype(BF16)


def _arrange_cd_weight(w):
    d = w.shape[0]
    gw = GROUP_WIDTH
    q = w[:, 3 * gw:4 * gw].reshape(d, SB_HEADS // 2, 2, HEAD_DIM) * QK_SCALE
    z = jnp.zeros((d, SB_HEADS // 2, HEAD_DIM), w.dtype)
    q_even = jnp.concatenate([q[:, :, 0], z], axis=-1)
    q_odd = jnp.concatenate([z, q[:, :, 1]], axis=-1)
    q_pad = jnp.stack([q_even, q_odd], axis=2).reshape(d, SB_HEADS * LANES)
    out = jnp.concatenate([w[:, :3 * gw], q_pad, w[:, 4 * gw:]], axis=-1)
    assert out.shape[1] == CD_COLS
    return out.astype(BF16)


def _proj_cd(x, g, w):
    b, t, d = x.shape
    tm = min(PROJ_TM, t)
    wa = _arrange_cd_weight(w)

    def tok(width):
        return pl.BlockSpec((1, tm, width), lambda bi, i: (bi, i, 0))

    widths = [GROUP_WIDTH, GROUP_WIDTH, SB_HEADS * LANES, GROUP_WIDTH, GROUP_WIDTH]
    dtypes = [F32, F32, BF16, BF16, BF16]
    return pl.pallas_call(
        _proj_cd_kernel,
        out_shape=[jax.ShapeDtypeStruct((b, t, wd), dt) for wd, dt in zip(widths, dtypes)],
        grid=(b, t // tm),
        in_specs=[tok(d), pl.BlockSpec((1, d), lambda bi, i: (0, 0)),
                  pl.BlockSpec((d, CD_COLS), lambda bi, i: (0, 0))],
        out_specs=[tok(wd) for wd in widths],
        compiler_params=_params("parallel", "parallel"),
        name="proj_cd",
    )(x, g.reshape(1, d), wa)


def _conv_kernel(cur_ref, halo_ref, w_ref, b_ref, lg_ref, lb_ref, o_ref, ext_ref, *, tt):
    i = pl.program_id(1)
    halo = halo_ref[0]
    ext_ref[0:CONV_HALO, :] = jnp.where(i == 0, jnp.zeros_like(halo), halo)
    ext_ref[CONV_HALO:, :] = cur_ref[0]
    off = CONV_HALO - (CONV_WIDTH - 1)
    for c in range(tt // CONV_CHUNK):
        r0 = c * CONV_CHUNK
        acc = jnp.zeros((CONV_CHUNK, GROUP_WIDTH), F32)
        for k in range(CONV_WIDTH):
            acc = acc + w_ref[k:k + 1, :] * ext_ref[r0 + off + k:r0 + off + k + CONV_CHUNK, :]
        y = acc + b_ref[...]
        mu = jnp.mean(y, axis=-1, keepdims=True)
        yc = y - mu
        var = jnp.mean(yc * yc, axis=-1, keepdims=True)
        yn = yc * lax.rsqrt(var + LN_EPS) * lg_ref[...] + lb_ref[...]
        o_ref[0, r0:r0 + CONV_CHUNK, :] = (yn * _sigmoid(yn)).astype(o_ref.dtype)


def _conformer_conv(a, w, bias, ln_g, ln_b):
    b, t, c = a.shape
    tt = min(CONV_TT, t)
    hb = tt // CONV_HALO
    row = lambda v: v.reshape(1, c)
    const = lambda shape: pl.BlockSpec(shape, lambda bi, i: (0, 0))
    return pl.pallas_call(
        functools.partial(_conv_kernel, tt=tt),
        out_shape=jax.ShapeDtypeStruct((b, t, c), BF16),
        grid=(b, t // tt),
        in_specs=[pl.BlockSpec((1, tt, c), lambda bi, i: (bi, i, 0)),
                  pl.BlockSpec((1, CONV_HALO, c), lambda bi, i: (bi, jnp.maximum(i * hb - 1, 0), 0)),
                  const((CONV_WIDTH, c)), const((1, c)), const((1, c)), const((1, c))],
        out_specs=pl.BlockSpec((1, tt, c), lambda bi, i: (bi, i, 0)),
        scratch_shapes=[pltpu.VMEM((tt + CONV_HALO, c), F32)],
        compiler_params=_params("parallel", "parallel"),
        name="conformer_conv",
    )(a, a, w, row(bias), row(ln_g), row(ln_b))


def _sc_kernel(p_ref, halo_ref, bg_ref, w_ref, o_ref, ext_ref, *, tt):
    i = pl.program_id(1)
    halo = halo_ref[0]
    ext_ref[0:SC_HALO, :] = jnp.where(i == 0, jnp.zeros_like(halo), halo)
    ext_ref[SC_HALO:, :] = p_ref[0]
    off = SC_HALO - (SC_WIDTH - 1)
    acc = jnp.zeros((tt, GROUP_WIDTH), F32)
    for k in range(SC_WIDTH):
        acc = acc + w_ref[k:k + 1, :] * ext_ref[off + k:off + k + tt, :]
    o_ref[0] = (bg_ref[0] * acc).astype(o_ref.dtype)


def _short_conv(p, bg, w):
    b, t, c = p.shape
    tt = min(SC_TT, t)
    hb = tt // SC_HALO
    return pl.pallas_call(
        functools.partial(_sc_kernel, tt=tt),
        out_shape=jax.ShapeDtypeStruct((b, t, c), BF16),
        grid=(b, t // tt),
        in_specs=[pl.BlockSpec((1, tt, c), lambda bi, i: (bi, i, 0)),
                  pl.BlockSpec((1, SC_HALO, c), lambda bi, i: (bi, jnp.maximum(i * hb - 1, 0), 0)),
                  pl.BlockSpec((1, tt, c), lambda bi, i: (bi, i, 0)),
                  pl.BlockSpec((SC_WIDTH, c), lambda bi, i: (0, 0))],
        out_specs=pl.BlockSpec((1, tt, c), lambda bi, i: (bi, i, 0)),
        scratch_shapes=[pltpu.VMEM((tt + SC_HALO, c), F32)],
        compiler_params=_params("parallel", "parallel"),
        name="short_conv",
    )(p, p, bg, w)


def _outproj_kernel(x_ref, a_ref, o_ref, wa_ref, wo_ref, y_ref):
    y_ref[...] = x_ref[...] + _dot(a_ref[...], wa_ref[...]) + _dot(o_ref[...], wo_ref[...])


def _outproj(x, a, o, w):
    n, d = x.shape
    gw = a.shape[1]
    tm = min(OUT_TM, n)
    w = w.astype(BF16)
    tok = lambda width: pl.BlockSpec((tm, width), lambda i: (i, 0))
    return pl.pallas_call(
        _outproj_kernel,
        out_shape=jax.ShapeDtypeStruct((n, d), F32),
        grid=(n // tm,),
        in_specs=[tok(d), tok(gw), tok(gw),
                  pl.BlockSpec((gw, d), lambda i: (0, 0)), pl.BlockSpec((gw, d), lambda i: (1, 0))],
        out_specs=tok(d),
        compiler_params=_params("parallel"),
        name="outproj",
    )(x, a, o, w, w)


def _gelu_tanh(x):
    return 0.5 * x * (1.0 + jnp.tanh(np.sqrt(2.0 / np.pi).astype(np.float32) * (x + 0.044715 * (x * x * x))))


def _compress_kernel(x_ref, pe_ref, w1_ref, w2_ref, o_ref, *, nchunk):
    half = NSA_CMP_STRIDE * HEAD_DIM
    x = x_ref[0, 0]
    ha = _dot((x + pe_ref[:, :half]).astype(BF16), w1_ref[:half, :])
    hb = _dot((x + pe_ref[:, half:]).astype(BF16), w1_ref[half:, :])
    h = ha + pltpu.roll(hb, nchunk - 1, 0)
    y = _dot(_gelu_tanh(h).astype(BF16), w2_ref[...])
    row = lax.broadcasted_iota(jnp.int32, y.shape, 0)
    o_ref[0, 0] = jnp.where(row < nchunk - 1, y, 0.0)


def _compress(kx, pe, w1, w2, dup):
    b, t, _ = kx.shape
    g = NSA_KV_GROUPS
    nchunk = t // NSA_CMP_STRIDE
    feat = NSA_CMP_STRIDE * HEAD_DIM
    x = kx.reshape(b, nchunk, NSA_CMP_STRIDE, g, HEAD_DIM).transpose(0, 3, 1, 2, 4).reshape(b, g, nchunk, feat)
    w2p = jnp.concatenate([w2, w2 if dup else jnp.zeros_like(w2)], axis=-1).astype(BF16)
    y = pl.pallas_call(
        functools.partial(_compress_kernel, nchunk=nchunk),
        out_shape=jax.ShapeDtypeStruct((b, g, nchunk, LANES), F32),
        grid=(b, g),
        in_specs=[pl.BlockSpec((1, 1, nchunk, feat), lambda bi, gi: (bi, gi, 0, 0)),
                  pl.BlockSpec((1, 2 * feat), lambda bi, gi: (0, 0)),
                  pl.BlockSpec((2 * feat, NSA_CMP_HIDDEN), lambda bi, gi: (0, 0)),
                  pl.BlockSpec((NSA_CMP_HIDDEN, LANES), lambda bi, gi: (0, 0))],
        out_specs=pl.BlockSpec((1, 1, nchunk, LANES), lambda bi, gi: (bi, gi, 0, 0)),
        compiler_params=_params("parallel", "parallel"),
        name="nsa_compress",
    )(x, pe.reshape(1, 2 * feat), w1.reshape(2 * feat, NSA_CMP_HIDDEN).astype(BF16), w2p)
    ratio = NSA_SEL_BLOCK // NSA_CMP_STRIDE
    return y.reshape(b, g, nchunk // ratio, ratio, LANES).transpose(0, 1, 3, 2, 4).astype(BF16)


def _softmax_rows(s, mask):
    m = jnp.max(s, axis=-1, keepdims=True)
    p = jnp.where(mask, jnp.exp(s - m), 0.0)
    l = jnp.sum(p, axis=-1, keepdims=True)
    return p * jnp.where(l > 0.0, 1.0 / l, 0.0)


def _nsa_kernel(q_ref, kc_ref, vc_ref, ks_ref, vs_ref, kw_ref, vw_ref, gt_ref, o_ref, *, t_len, ns):
    tq, tk, rep = NSA_TQ, NSA_TK, NSA_REP
    rows = rep * tq
    gi = pl.program_id(1)
    i = pl.program_id(2)
    q0 = i * tq

    q2 = q_ref[0]
    qg = jnp.concatenate([q2[:, r * LANES:(r + 1) * LANES] for r in range(rep)], axis=0)
    rid = lax.broadcasted_iota(jnp.int32, (rows, 1), 0)
    t_i = q0 + (rid & (tq - 1))
    t_f = t_i.astype(F32)
    hr = lax.shift_right_logical(rid, int(np.log2(tq)))
    slope = jnp.zeros((rows, 1), F32)
    for r in range(rep):
        slope_r = F32(0.0)
        for g in range(NSA_KV_GROUPS):
            slope_r = jnp.where(gi == g, F32(2.0 ** -(g * rep + r + 1)), slope_r)
        slope = jnp.where(hr == r, slope_r, slope)

    lane_i = lax.broadcasted_iota(jnp.int32, (1, ns), 1)
    ratio = NSA_SEL_BLOCK // NSA_CMP_STRIDE
    n_cmp = t_len // NSA_CMP_STRIDE - 1
    s_list, mask_list = [], []
    for r in range(ratio):
        c_i = lane_i * ratio + r
        start = c_i * NSA_CMP_STRIDE
        centre = start.astype(F32) + (NSA_CMP_BLOCK - 1) / 2
        mask = ((start + (NSA_CMP_BLOCK - 1)) <= t_i) & (c_i < n_cmp)
        s = _dot_nt(qg, kc_ref[0, 0, r]) - slope * (t_f - centre)
        s_list.append(jnp.where(mask, s, NEG_INF))
        mask_list.append(mask)
    m = functools.reduce(jnp.maximum, [jnp.max(s, axis=-1, keepdims=True) for s in s_list])
    p_list = [jnp.where(mk, jnp.exp(s - m), 0.0) for s, mk in zip(s_list, mask_list)]
    l = functools.reduce(lambda a, b: a + b, [jnp.sum(p, axis=-1, keepdims=True) for p in p_list])
    inv = jnp.where(l > 0.0, 1.0 / l, 0.0)
    p_list = [p * inv for p in p_list]
    o_c = functools.reduce(lambda a, b: a + b,
                           [_dot(p.astype(BF16), vc_ref[0, 0, r]) for r, p in enumerate(p_list)])

    def head_sum(p):
        return functools.reduce(lambda a, b: a + b, [p[r * tq:(r + 1) * tq] for r in range(rep)])

    ps = [head_sum(p) for p in p_list]
    lane_q = lax.broadcasted_iota(jnp.int32, (tq, ns), 1)
    prev_last = jnp.where(lane_q == 0, 0.0, pltpu.roll(ps[ratio - 1], 1, 1))
    imp = prev_last + ps[0] + ps[1] + ps[2] + ps[3]
    tq_i = q0 + lax.broadcasted_iota(jnp.int32, (tq, 1), 0)
    cur = lax.shift_right_logical(tq_i, int(np.log2(NSA_SEL_BLOCK)))
    visible = lane_q * NSA_SEL_BLOCK <= tq_i
    forced = (lane_q == 0) | (lane_q == cur) | (lane_q == cur - 1)
    score = jnp.where(visible, jnp.where(forced, SEL_FORCE, imp), -1.0)

    lane_f = lane_q.astype(F32)
    work = score
    sel = jnp.zeros((tq, ns), F32)
    for _ in range(min(NSA_TOP_N, ns)):
        top = jnp.max(work, axis=-1, keepdims=True)
        idx = jnp.min(jnp.where(work == top, lane_f, float(ns)), axis=-1, keepdims=True)
        hit = lane_f == idx
        sel = jnp.where(hit, 1.0, sel)
        work = jnp.where(hit, -2.0, work)
    sel = jnp.where(score >= 0.0, sel, 0.0).astype(BF16)

    blk_row = lax.broadcasted_iota(jnp.int32, (ns, tk), 0)
    blk_col = lax.shift_right_logical(lax.broadcasted_iota(jnp.int32, (ns, tk), 1), int(np.log2(NSA_SEL_BLOCK)))
    col_i = lax.broadcasted_iota(jnp.int32, (1, tk), 1)

    def sel_body(kt, carry):
        m_run, l_run, acc = carry
        k0 = pl.multiple_of(kt * tk, tk)
        s = _dot_nt(qg, ks_ref[0, pl.ds(k0, tk), :])
        tok = k0 + col_i
        s = s - slope * (t_f - tok.astype(F32))
        expand = jnp.where(blk_row == blk_col + kt * (tk // NSA_SEL_BLOCK), 1.0, 0.0).astype(BF16)
        hit = _dot(sel, expand)
        hit = jnp.concatenate([hit] * rep, axis=0)
        valid = (hit > 0.5) & (tok <= t_i)
        s = jnp.where(valid, s, NEG_INF)
        m_new = jnp.maximum(m_run, jnp.max(s, axis=-1, keepdims=True))
        p = jnp.where(valid, jnp.exp(s - m_new), 0.0)
        alpha = jnp.exp(m_run - m_new)
        l_new = alpha * l_run + jnp.sum(p, axis=-1, keepdims=True)
        acc = alpha * acc + _dot(p.astype(BF16), vs_ref[0, pl.ds(k0, tk), :])
        return m_new, l_new, acc

    n_tiles = (q0 + tq + tk - 1) // tk
    init = (jnp.full((rows, 1), NEG_INF, F32), jnp.zeros((rows, 1), F32), jnp.zeros((rows, LANES), F32))
    _, l_s, acc_s = lax.fori_loop(0, n_tiles, sel_body, init)
    o_s = acc_s * jnp.where(l_s > 0.0, 1.0 / l_s, 0.0)

    wlen = NSA_WINDOW + tq
    w0 = pl.multiple_of(jnp.maximum(q0 - NSA_WINDOW, 0), tq)
    dist = t_i - (w0 + lax.broadcasted_iota(jnp.int32, (1, wlen), 1))
    mask_w = (dist >= 0) & (dist < NSA_WINDOW)
    s_w = _dot_nt(qg, kw_ref[0, pl.ds(w0, wlen), :]) - slope * dist.astype(F32)
    p_w = _softmax_rows(jnp.where(mask_w, s_w, NEG_INF), mask_w)
    o_w = _dot(p_w.astype(BF16), vw_ref[0, pl.ds(w0, wlen), :])

    gt = gt_ref[0]

    def gate(branch):
        return jnp.concatenate(
            [gt[:, r * NSA_N_BRANCH + branch:r * NSA_N_BRANCH + branch + 1] for r in range(rep)], axis=0)

    o = gate(0) * o_c + gate(1) * o_s + gate(2) * o_w
    low_half = lax.broadcasted_iota(jnp.int32, (tq, LANES), 1) < HEAD_DIM
    for c in range(rep // 2):
        even = o[(2 * c) * tq:(2 * c + 1) * tq]
        odd = o[(2 * c + 1) * tq:(2 * c + 2) * tq]
        o_ref[0, :, c * LANES:(c + 1) * LANES] = jnp.where(low_half, even, odd).astype(o_ref.dtype)


def _nsa(q, kc, vc, ks, vs, kw, vw, gates):
    b, t, _ = q.shape
    g = NSA_KV_GROUPS
    ns = t // NSA_SEL_BLOCK
    assert ns <= LANES and t >= NSA_WINDOW + NSA_TQ and t % NSA_TK == 0
    ratio = NSA_SEL_BLOCK // NSA_CMP_STRIDE
    qw = NSA_REP * LANES
    cmp_spec = pl.BlockSpec((1, 1, ratio, ns, LANES), lambda bi, gi, i: (bi, gi, 0, 0, 0))
    kv_spec = pl.BlockSpec((1, t, LANES), lambda bi, gi, i: (bi, 0, gi))
    return pl.pallas_call(
        functools.partial(_nsa_kernel, t_len=t, ns=ns),
        out_shape=jax.ShapeDtypeStruct((b, t, GROUP_WIDTH), BF16),
        grid=(b, g, t // NSA_TQ),
        in_specs=[pl.BlockSpec((1, NSA_TQ, qw), lambda bi, gi, i: (bi, i, gi)),
                  cmp_spec, cmp_spec, kv_spec, kv_spec, kv_spec, kv_spec,
                  pl.BlockSpec((1, NSA_TQ, LANES), lambda bi, gi, i: (bi, i, gi))],
        out_specs=pl.BlockSpec((1, NSA_TQ, NSA_REP * HEAD_DIM), lambda bi, gi, i: (bi, i, gi)),
        compiler_params=_params("parallel", "parallel", "arbitrary"),
        name="nsa_attention",
    )(q, kc, vc, ks, vs, kw, vw, gates)


def _sb_tile(qh, k, v, u, carry, diag):
    acc, c = carry
    z = _dot_nt(qh, k)
    soft = jnp.log(1.0 + jnp.exp(-jnp.abs(z)))
    log_beta = jnp.minimum(z, 0.0) - soft
    log_keep = jnp.minimum(-z, 0.0) - soft
    if diag:
        n = z.shape[0]
        mask = lax.broadcasted_iota(jnp.int32, (n, n), 1) < lax.broadcasted_iota(jnp.int32, (n, n), 0)
        log_keep = jnp.where(mask, log_keep, 0.0)
    hi = log_keep.astype(BF16)
    lo = (log_keep - hi.astype(F32)).astype(BF16)
    later = _dot(hi, u) + _dot(lo, u)
    a = jnp.exp(log_beta + later + c)
    if diag:
        a = jnp.where(mask, a, 0.0)
    acc = acc + _dot(a.astype(BF16), v)
    c = c + jnp.sum(log_keep, axis=-1, keepdims=True)
    return acc, c


def _sb_kernel(q_ref, k_ref, v_ref, u_ref, o_ref):
    tile = SB_T
    i = pl.program_id(2)
    q2 = q_ref[0]
    u = u_ref[...]
    qs = [q2[:, hh * LANES:(hh + 1) * LANES] for hh in range(2)]
    q0 = pl.multiple_of(i * tile, tile)
    k_diag, v_diag = k_ref[0, pl.ds(q0, tile), :], v_ref[0, pl.ds(q0, tile), :]
    zero = (jnp.zeros((tile, LANES), F32), jnp.zeros((tile, 1), F32))
    carry = tuple(_sb_tile(qh, k_diag, v_diag, u, zero, True) for qh in qs)

    def body(jj, carry):
        k0 = pl.multiple_of((i - 1 - jj) * tile, tile)
        k, v = k_ref[0, pl.ds(k0, tile), :], v_ref[0, pl.ds(k0, tile), :]
        return tuple(_sb_tile(qh, k, v, u, cr, False) for qh, cr in zip(qs, carry))

    (acc0, _), (acc1, _) = lax.fori_loop(0, i, body, carry)
    low_half = lax.broadcasted_iota(jnp.int32, (tile, LANES), 1) < HEAD_DIM
    o_ref[0] = jnp.where(low_half, acc0, acc1).astype(o_ref.dtype)


def _stick_breaking(q, k, v):
    b, t, _ = q.shape
    tile = min(SB_T, t)
    assert tile == SB_T
    u = (np.arange(tile)[:, None] > np.arange(tile)[None, :]).astype(np.float32)
    kv_spec = pl.BlockSpec((1, t, LANES), lambda bi, hp, i: (bi, 0, hp))
    return pl.pallas_call(
        _sb_kernel,
        out_shape=jax.ShapeDtypeStruct((b, t, GROUP_WIDTH), BF16),
        grid=(b, SB_HEADS // 2, t // tile),
        in_specs=[pl.BlockSpec((1, tile, 2 * LANES), lambda bi, hp, i: (bi, i, hp)),
                  kv_spec, kv_spec,
                  pl.BlockSpec((tile, tile), lambda bi, hp, i: (0, 0))],
        out_specs=pl.BlockSpec((1, tile, LANES), lambda bi, hp, i: (bi, i, hp)),
        compiler_params=_params("parallel", "parallel", "arbitrary"),
        name="stick_breaking",
    )(q, k, v, jnp.asarray(u, BF16))


def _mixer_conv_nsa(x, norm_g, w_in, dw_w, dw_b, ln_g, ln_b, pe_k, w1_k, w2_k, pe_v, w1_v, w2_v, w_out):
    b, t, d = x.shape
    a, q, kc, vc, ks, vs, kw, vw, gates = _proj_ab(x, norm_g, w_in)
    a = _conformer_conv(a, dw_w, dw_b, ln_g, ln_b)
    k_cmp = _compress(kc, pe_k, w1_k, w2_k, dup=False)
    v_cmp = _compress(vc, pe_v, w1_v, w2_v, dup=True)
    o = _nsa(q, k_cmp, v_cmp, ks, vs, kw, vw, gates)
    n = b * t
    return _outproj(x.reshape(n, d), a.reshape(n, -1), o.reshape(n, -1), w_out).reshape(b, t, d)


def _mixer_shortconv_sb(x, norm_g, w_in, sc_w, w_out):
    b, t, d = x.shape
    bg, p, q, k, v = _proj_cd(x, norm_g, w_in)
    c = _short_conv(p, bg, sc_w)
    o = _stick_breaking(q, k, v)
    n = b * t
    return _outproj(x.reshape(n, d), c.reshape(n, -1), o.reshape(n, -1), w_out).reshape(b, t, d)


def kernel(x, ffn1_norm, ffn1_w_in, ffn1_w_out, mix_norm, ffn2_norm, ffn2_w_in, ffn2_w_out, ab_w_in, conv_dw_w, conv_dw_b, conv_ln_g, conv_ln_b, nsa_pe_k, nsa_w1_k, nsa_w2_k, nsa_pe_v, nsa_w1_v, nsa_w2_v, ab_w_out, cd_w_in, sc_conv_w, cd_w_out, final_norm):
    b, t, d = x.shape
    depth = ffn1_norm.shape[0]
    n = b * t
    for layer in range(depth):
        x = _ffn(x.reshape(n, d), ffn1_norm[layer], ffn1_w_in[layer], ffn1_w_out[layer]).reshape(b, t, d)
        if layer % 2 == 0:
            e = layer // 2
            x = _mixer_conv_nsa(x, mix_norm[layer], ab_w_in[e], conv_dw_w[e], conv_dw_b[e], conv_ln_g[e],
                                conv_ln_b[e], nsa_pe_k[e], nsa_w1_k[e], nsa_w2_k[e],
                                nsa_pe_v[e], nsa_w1_v[e], nsa_w2_v[e], ab_w_out[e])
        else:
            o = layer // 2
            x = _mixer_shortconv_sb(x, mix_norm[layer], cd_w_in[o], sc_conv_w[o], cd_w_out[o])
        last = layer == depth - 1
        x = _ffn(x.reshape(n, d), ffn2_norm[layer], ffn2_w_in[layer], ffn2_w_out[layer],
                 final_g=final_norm if last else None).reshape(b, t, d)
    return x
```

```python
import functools

import numpy as np
import jax
import jax.numpy as jnp
from jax import lax
from jax.experimental import pallas as pl
from jax.experimental.pallas import tpu as pltpu

F32 = jnp.float32
BF16 = jnp.bfloat16

D_MODEL = 1024
HEAD_DIM = 64
GROUP_WIDTH = D_MODEL // 2
CONV_WIDTH = 31
NSA_HEADS = GROUP_WIDTH // HEAD_DIM
NSA_KV_GROUPS = 2
NSA_REP = NSA_HEADS // NSA_KV_GROUPS
NSA_CMP_BLOCK = 32
NSA_CMP_STRIDE = 16
NSA_CMP_HIDDEN = 128
NSA_SEL_BLOCK = 64
NSA_TOP_N = 16
NSA_WINDOW = 512
NSA_N_BRANCH = 3
SC_WIDTH = 3
SB_HEADS = GROUP_WIDTH // HEAD_DIM
D_FF = 2816
RMS_EPS = 1e-6
LN_EPS = 1e-5
NEG_INF = -1e30
SEL_FORCE = 1e4
QK_SCALE = HEAD_DIM ** -0.5
LOG2_E = float(np.log2(np.e))

LANES = 128
VMEM_LIMIT = 48 * 1024 * 1024

FFN_TM = 1024
FFN_TF = 256
PROJ_TM = 512
OUT_TM = 1024
CONV_TT = 256
CONV_HALO = 32
CONV_CHUNK = 64
SC_TT = 512
SC_HALO = 8
NSA_TQ = 128
NSA_TK = 512
SB_T = 256
SB_PAIRS = 2


def _params(*sem):
    return pltpu.CompilerParams(dimension_semantics=sem, vmem_limit_bytes=VMEM_LIMIT)


def _dot(a, b):
    return jnp.dot(a, b, preferred_element_type=F32)


def _dot_nt(a, b):
    return lax.dot_general(a, b, (((1,), (1,)), ((), ())), preferred_element_type=F32)


def _sigmoid(x):
    return 1.0 / (1.0 + jnp.exp(-x))


def _rmsnorm_rows(x, g):
    return x * lax.rsqrt(jnp.mean(x * x, axis=-1, keepdims=True) + RMS_EPS) * g


def _ffn_kernel(x_ref, g_ref, wg_ref, wu_ref, wo_ref, *rest, n_ff, final_norm):
    if final_norm:
        fg_ref, o_ref, xn_ref, acc_ref = rest
    else:
        o_ref, xn_ref, acc_ref = rest
    j = pl.program_id(1)

    @pl.when(j == 0)
    def _():
        xn_ref[...] = _rmsnorm_rows(x_ref[...], g_ref[...]).astype(BF16)
        acc_ref[...] = jnp.zeros_like(acc_ref)

    xn = xn_ref[...]
    gate = _dot(xn, wg_ref[...])
    up = _dot(xn, wu_ref[...])
    h = gate * _sigmoid(gate) * up
    acc_ref[...] += _dot(h.astype(BF16), wo_ref[...])

    @pl.when(j == n_ff - 1)
    def _():
        y = x_ref[...] + 0.5 * acc_ref[...]
        if final_norm:
            y = _rmsnorm_rows(y, fg_ref[...])
        o_ref[...] = y


def _ffn(x, g, w_in, w_out, final_g=None):
    n, d = x.shape
    n_ff = D_FF // FFN_TF
    tm = min(FFN_TM, n)
    w_in = w_in.astype(BF16)
    w_out = w_out.astype(BF16)
    in_specs = [
        pl.BlockSpec((tm, d), lambda i, j: (i, 0)),
        pl.BlockSpec((1, d), lambda i, j: (0, 0)),
        pl.BlockSpec((d, FFN_TF), lambda i, j: (0, j)),
        pl.BlockSpec((d, FFN_TF), lambda i, j: (0, j + n_ff)),
        pl.BlockSpec((FFN_TF, d), lambda i, j: (j, 0)),
    ]
    args = [x, g.reshape(1, d), w_in, w_in, w_out]
    if final_g is not None:
        in_specs.append(pl.BlockSpec((1, d), lambda i, j: (0, 0)))
        args.append(final_g.reshape(1, d))
    return pl.pallas_call(
        functools.partial(_ffn_kernel, n_ff=n_ff, final_norm=final_g is not None),
        out_shape=jax.ShapeDtypeStruct((n, d), F32),
        grid=(n // tm, n_ff),
        in_specs=in_specs,
        out_specs=pl.BlockSpec((tm, d), lambda i, j: (i, 0)),
        scratch_shapes=[pltpu.VMEM((tm, d), BF16), pltpu.VMEM((tm, d), F32)],
        compiler_params=_params("parallel", "arbitrary"),
        name="ffn",
    )(*args)


AB_A = 0
AB_Q = 1024
AB_KC = 2048
AB_VC = 2176
AB_KS = 2304
AB_VS = 2560
AB_KW = 2816
AB_VW = 3072
AB_G = 3328
AB_COLS = 3584

POS_SPLIT = 3
POS_TOK = HEAD_DIM
POS_CMP = POS_TOK + 2 * POS_SPLIT
MASK_BIG = 2.0 ** 100


def _pos_lanes(hi, lo, base):
    lane = lax.broadcasted_iota(jnp.int32, (1, LANES), 1)
    in_hi = (lane >= base) & (lane < base + POS_SPLIT)
    in_lo = (lane >= base + POS_SPLIT) & (lane < base + 2 * POS_SPLIT)
    return jnp.where(in_hi, hi, jnp.where(in_lo, lo, 0.0))


def _proj_ab_kernel(x_ref, g_ref, w_ref, qb_ref, a_ref, q_ref, kc_ref, vc_ref, ks_ref, vs_ref, kw_ref, vw_ref,
                    gt_ref, *, tm):
    xn = _rmsnorm_rows(x_ref[0], g_ref[...]).astype(BF16)

    def seg(lo, hi):
        return _dot(xn, w_ref[:, lo:hi])

    av = seg(AB_A, AB_A + 2 * GROUP_WIDTH)
    a_ref[0] = av[:, :GROUP_WIDTH] * _sigmoid(av[:, GROUP_WIDTH:])
    q_ref[0] = (seg(AB_Q, AB_KC) + qb_ref[...]).astype(BF16)
    kc_ref[0] = seg(AB_KC, AB_VC)
    vc_ref[0] = seg(AB_VC, AB_KS)
    vs_ref[0] = seg(AB_VS, AB_KW).astype(BF16)
    vw_ref[0] = seg(AB_VW, AB_G).astype(BF16)
    gt_ref[0] = _sigmoid(seg(AB_G, AB_COLS))

    tok = pl.program_id(1) * tm + lax.broadcasted_iota(jnp.int32, (tm, 1), 0)
    blk = lax.shift_right_logical(tok, int(np.log2(NSA_SEL_BLOCK)))
    pos = _pos_lanes(blk.astype(F32), (tok & (NSA_SEL_BLOCK - 1)).astype(F32), POS_TOK)
    onehot = jnp.where(lax.broadcasted_iota(jnp.int32, (1, LANES), 1) == blk, 1.0, 0.0).astype(BF16)
    ks = seg(AB_KS, AB_VS)
    kw = seg(AB_KW, AB_VW)
    for g in range(NSA_KV_GROUPS):
        ks_ref[0, :, 2 * g * LANES:(2 * g + 1) * LANES] = (ks[:, g * LANES:(g + 1) * LANES] + pos).astype(BF16)
        ks_ref[0, :, (2 * g + 1) * LANES:(2 * g + 2) * LANES] = onehot
        kw_ref[0, :, g * LANES:(g + 1) * LANES] = (kw[:, g * LANES:(g + 1) * LANES] + pos).astype(BF16)


def _arrange_ab_weight(w):
    d = w.shape[0]
    kvw = NSA_KV_GROUPS * HEAD_DIM
    o = 2 * GROUP_WIDTH
    a = w[:, :o]
    q = w[:, o:o + GROUP_WIDTH].reshape(d, NSA_HEADS, HEAD_DIM) * (QK_SCALE * LOG2_E)
    o += GROUP_WIDTH
    kc, vc, ks, vs, kw, vw = [w[:, o + i * kvw:o + (i + 1) * kvw] for i in range(6)]
    o += 6 * kvw
    g = w[:, o:].reshape(d, NSA_KV_GROUPS, NSA_REP * NSA_N_BRANCH)

    zeros_h = jnp.zeros((d, NSA_HEADS, HEAD_DIM), w.dtype)
    q_pad = jnp.concatenate([q, zeros_h], axis=-1).reshape(d, NSA_HEADS * LANES)

    def k_pad(k):
        k = k.reshape(d, NSA_KV_GROUPS, HEAD_DIM)
        return jnp.concatenate([k, jnp.zeros_like(k)], axis=-1).reshape(d, NSA_KV_GROUPS * LANES)

    def v_dup(v):
        v = v.reshape(d, NSA_KV_GROUPS, HEAD_DIM)
        return jnp.concatenate([v, v], axis=-1).reshape(d, NSA_KV_GROUPS * LANES)

    g_pad = jnp.pad(g, ((0, 0), (0, 0), (0, LANES - g.shape[-1]))).reshape(d, NSA_KV_GROUPS * LANES)
    out = jnp.concatenate([a, q_pad, kc, vc, k_pad(ks), v_dup(vs), k_pad(kw), v_dup(vw), g_pad], axis=-1)
    assert out.shape[1] == AB_COLS
    return out.astype(BF16)


def _bf16_terms(x, n):
    terms, rest = [], np.asarray(x, np.float64)
    for _ in range(n):
        term = rest.astype(BF16).astype(np.float64)
        terms.append(term)
        rest = rest - term
    return terms


def _alibi_query_lanes():
    row = np.zeros((NSA_HEADS, LANES), np.float64)
    for h in range(NSA_HEADS):
        slope = 2.0 ** (-8.0 * (h + 1) / NSA_HEADS)
        for i, term in enumerate(_bf16_terms(slope * np.log2(np.e), POS_SPLIT)):
            row[h, POS_TOK + i] = NSA_SEL_BLOCK * term
            row[h, POS_TOK + POS_SPLIT + i] = term
            row[h, POS_CMP + i] = 2 * NSA_CMP_STRIDE * term
            row[h, POS_CMP + POS_SPLIT + i] = NSA_CMP_STRIDE * term
    return jnp.asarray(row.reshape(1, NSA_HEADS * LANES), F32)


def _proj_ab(x, g, w):
    b, t, d = x.shape
    tm = min(PROJ_TM, t)
    wa = _arrange_ab_weight(w)

    def tok(width):
        return pl.BlockSpec((1, tm, width), lambda bi, i: (bi, i, 0))

    const = lambda width: pl.BlockSpec((1, width), lambda bi, i: (0, 0))
    kv = NSA_KV_GROUPS * LANES
    widths = [GROUP_WIDTH, NSA_HEADS * LANES, LANES, LANES, 2 * kv, kv, kv, kv, kv]
    dtypes = [F32, BF16, F32, F32, BF16, BF16, BF16, BF16, F32]
    return pl.pallas_call(
        functools.partial(_proj_ab_kernel, tm=tm),
        out_shape=[jax.ShapeDtypeStruct((b, t, wd), dt) for wd, dt in zip(widths, dtypes)],
        grid=(b, t // tm),
        in_specs=[tok(d), const(d), pl.BlockSpec((d, AB_COLS), lambda bi, i: (0, 0)), const(NSA_HEADS * LANES)],
        out_specs=[tok(wd) for wd in widths],
        compiler_params=_params("parallel", "parallel"),
        name="proj_ab",
    )(x, g.reshape(1, d), wa, _alibi_query_lanes())


CD_B = 0
CD_C = 512
CD_U = 1024
CD_Q = 1536
CD_K = 2560
CD_V = 3072
CD_COLS = 3584


def _proj_cd_kernel(x_ref, g_ref, w_ref, bg_ref, p_ref, q_ref, k_ref, v_ref):
    xn = _rmsnorm_rows(x_ref[0], g_ref[...]).astype(BF16)

    def seg(lo, hi):
        return _dot(xn, w_ref[:, lo:hi])

    bg_ref[0] = seg(CD_B, CD_C)
    p_ref[0] = seg(CD_C, CD_U) * seg(CD_U, CD_Q)
    q_ref[0] = seg(CD_Q, CD_K).astype(BF16)
    k_ref[0] = seg(CD_K, CD_V).astype(BF16)
    v_ref[0] = seg(CD_V, CD_COLS).astype(BF16)


def _arrange_cd_weight(w):
    d = w.shape[0]
    gw = GROUP_WIDTH
    q = w[:, 3 * gw:4 * gw].reshape(d, SB_HEADS // 2, 2, HEAD_DIM) * (QK_SCALE * LOG2_E)
    z = jnp.zeros((d, SB_HEADS // 2, HEAD_DIM), w.dtype)
    q_even = jnp.concatenate([q[:, :, 0], z], axis=-1)
    q_odd = jnp.concatenate([z, q[:, :, 1]], axis=-1)
    q_pad = jnp.stack([q_even, q_odd], axis=2).reshape(d, SB_HEADS * LANES)
    out = jnp.concatenate([w[:, :3 * gw], q_pad, w[:, 4 * gw:]], axis=-1)
    assert out.shape[1] == CD_COLS
    return out.astype(BF16)


def _proj_cd(x, g, w):
    b, t, d = x.shape
    tm = min(PROJ_TM, t)
    wa = _arrange_cd_weight(w)

    def tok(width):
        return pl.BlockSpec((1, tm, width), lambda bi, i: (bi, i, 0))

    widths = [GROUP_WIDTH, GROUP_WIDTH, SB_HEADS * LANES, GROUP_WIDTH, GROUP_WIDTH]
    dtypes = [F32, F32, BF16, BF16, BF16]
    return pl.pallas_call(
        _proj_cd_kernel,
        out_shape=[jax.ShapeDtypeStruct((b, t, wd), dt) for wd, dt in zip(widths, dtypes)],
        grid=(b, t // tm),
        in_specs=[tok(d), pl.BlockSpec((1, d), lambda bi, i: (0, 0)),
                  pl.BlockSpec((d, CD_COLS), lambda bi, i: (0, 0))],
        out_specs=[tok(wd) for wd in widths],
        compiler_params=_params("parallel", "parallel"),
        name="proj_cd",
    )(x, g.reshape(1, d), wa)


def _conv_kernel(cur_ref, halo_ref, w_ref, b_ref, lg_ref, lb_ref, o_ref, ext_ref, *, tt):
    i = pl.program_id(1)
    halo = halo_ref[0]
    ext_ref[0:CONV_HALO, :] = jnp.where(i == 0, jnp.zeros_like(halo), halo)
    ext_ref[CONV_HALO:, :] = cur_ref[0]
    off = CONV_HALO - (CONV_WIDTH - 1)
    for c in range(tt // CONV_CHUNK):
        r0 = c * CONV_CHUNK
        acc = jnp.zeros((CONV_CHUNK, GROUP_WIDTH), F32)
        for k in range(CONV_WIDTH):
            acc = acc + w_ref[k:k + 1, :] * ext_ref[r0 + off + k:r0 + off + k + CONV_CHUNK, :]
        y = acc + b_ref[...]
        mu = jnp.mean(y, axis=-1, keepdims=True)
        yc = y - mu
        var = jnp.mean(yc * yc, axis=-1, keepdims=True)
        yn = yc * lax.rsqrt(var + LN_EPS) * lg_ref[...] + lb_ref[...]
        o_ref[0, r0:r0 + CONV_CHUNK, :] = (yn * _sigmoid(yn)).astype(o_ref.dtype)


def _conformer_conv(a, w, bias, ln_g, ln_b):
    b, t, c = a.shape
    tt = min(CONV_TT, t)
    hb = tt // CONV_HALO
    row = lambda v: v.reshape(1, c)
    const = lambda shape: pl.BlockSpec(shape, lambda bi, i: (0, 0))
    return pl.pallas_call(
        functools.partial(_conv_kernel, tt=tt),
        out_shape=jax.ShapeDtypeStruct((b, t, c), BF16),
        grid=(b, t // tt),
        in_specs=[pl.BlockSpec((1, tt, c), lambda bi, i: (bi, i, 0)),
                  pl.BlockSpec((1, CONV_HALO, c), lambda bi, i: (bi, jnp.maximum(i * hb - 1, 0), 0)),
                  const((CONV_WIDTH, c)), const((1, c)), const((1, c)), const((1, c))],
        out_specs=pl.BlockSpec((1, tt, c), lambda bi, i: (bi, i, 0)),
        scratch_shapes=[pltpu.VMEM((tt + CONV_HALO, c), F32)],
        compiler_params=_params("parallel", "parallel"),
        name="conformer_conv",
    )(a, a, w, row(bias), row(ln_g), row(ln_b))


def _sc_kernel(p_ref, halo_ref, bg_ref, w_ref, o_ref, ext_ref, *, tt):
    i = pl.program_id(1)
    halo = halo_ref[0]
    ext_ref[0:SC_HALO, :] = jnp.where(i == 0, jnp.zeros_like(halo), halo)
    ext_ref[SC_HALO:, :] = p_ref[0]
    off = SC_HALO - (SC_WIDTH - 1)
    acc = jnp.zeros((tt, GROUP_WIDTH), F32)
    for k in range(SC_WIDTH):
        acc = acc + w_ref[k:k + 1, :] * ext_ref[off + k:off + k + tt, :]
    o_ref[0] = (bg_ref[0] * acc).astype(o_ref.dtype)


def _short_conv(p, bg, w):
    b, t, c = p.shape
    tt = min(SC_TT, t)
    hb = tt // SC_HALO
    return pl.pallas_call(
        functools.partial(_sc_kernel, tt=tt),
        out_shape=jax.ShapeDtypeStruct((b, t, c), BF16),
        grid=(b, t // tt),
        in_specs=[pl.BlockSpec((1, tt, c), lambda bi, i: (bi, i, 0)),
                  pl.BlockSpec((1, SC_HALO, c), lambda bi, i: (bi, jnp.maximum(i * hb - 1, 0), 0)),
                  pl.BlockSpec((1, tt, c), lambda bi, i: (bi, i, 0)),
                  pl.BlockSpec((SC_WIDTH, c), lambda bi, i: (0, 0))],
        out_specs=pl.BlockSpec((1, tt, c), lambda bi, i: (bi, i, 0)),
        scratch_shapes=[pltpu.VMEM((tt + SC_HALO, c), F32)],
        compiler_params=_params("parallel", "parallel"),
        name="short_conv",
    )(p, p, bg, w)


def _outproj_kernel(x_ref, a_ref, o_ref, wa_ref, wo_ref, y_ref):
    y_ref[...] = x_ref[...] + _dot(a_ref[...], wa_ref[...]) + _dot(o_ref[...], wo_ref[...])


def _outproj(x, a, o, w):
    n, d = x.shape
    gw = a.shape[1]
    tm = min(OUT_TM, n)
    w = w.astype(BF16)
    tok = lambda width: pl.BlockSpec((tm, width), lambda i: (i, 0))
    return pl.pallas_call(
        _outproj_kernel,
        out_shape=jax.ShapeDtypeStruct((n, d), F32),
        grid=(n // tm,),
        in_specs=[tok(d), tok(gw), tok(gw),
                  pl.BlockSpec((gw, d), lambda i: (0, 0)), pl.BlockSpec((gw, d), lambda i: (1, 0))],
        out_specs=tok(d),
        compiler_params=_params("parallel"),
        name="outproj",
    )(x, a, o, w, w)


def _gelu_tanh(x):
    return 0.5 * x * (1.0 + jnp.tanh(np.sqrt(2.0 / np.pi).astype(np.float32) * (x + 0.044715 * (x * x * x))))


def _compress_kernel(x_ref, pe_ref, w1_ref, w2_ref, o_ref, *, nchunk, keys):
    half = NSA_CMP_STRIDE * HEAD_DIM
    x = x_ref[0, 0]
    ha = _dot((x + pe_ref[:, :half]).astype(BF16), w1_ref[:half, :])
    hb = _dot((x + pe_ref[:, half:]).astype(BF16), w1_ref[half:, :])
    h = ha + pltpu.roll(hb, nchunk - 1, 0)
    y = _dot(_gelu_tanh(h).astype(BF16), w2_ref[...])
    row = lax.broadcasted_iota(jnp.int32, (nchunk, 1), 0)
    y = jnp.where(row < nchunk - 1, y, 0.0)
    if keys:
        y = y + _pos_lanes(lax.shift_right_logical(row, 1).astype(F32), (row & 1).astype(F32), POS_CMP)
    o_ref[0, 0] = y


def _compress(kx, pe, w1, w2, dup):
    b, t, _ = kx.shape
    g = NSA_KV_GROUPS
    nchunk = t // NSA_CMP_STRIDE
    feat = NSA_CMP_STRIDE * HEAD_DIM
    x = kx.reshape(b, nchunk, NSA_CMP_STRIDE, g, HEAD_DIM).transpose(0, 3, 1, 2, 4).reshape(b, g, nchunk, feat)
    w2p = jnp.concatenate([w2, w2 if dup else jnp.zeros_like(w2)], axis=-1).astype(BF16)
    y = pl.pallas_call(
        functools.partial(_compress_kernel, nchunk=nchunk, keys=not dup),
        out_shape=jax.ShapeDtypeStruct((b, g, nchunk, LANES), F32),
        grid=(b, g),
        in_specs=[pl.BlockSpec((1, 1, nchunk, feat), lambda bi, gi: (bi, gi, 0, 0)),
                  pl.BlockSpec((1, 2 * feat), lambda bi, gi: (0, 0)),
                  pl.BlockSpec((2 * feat, NSA_CMP_HIDDEN), lambda bi, gi: (0, 0)),
                  pl.BlockSpec((NSA_CMP_HIDDEN, LANES), lambda bi, gi: (0, 0))],
        out_specs=pl.BlockSpec((1, 1, nchunk, LANES), lambda bi, gi: (bi, gi, 0, 0)),
        compiler_params=_params("parallel", "parallel"),
        name="nsa_compress",
    )(x, pe.reshape(1, 2 * feat), w1.reshape(2 * feat, NSA_CMP_HIDDEN).astype(BF16), w2p)
    ratio = NSA_SEL_BLOCK // NSA_CMP_STRIDE
    return y.reshape(b, g, nchunk // ratio, ratio, LANES).transpose(0, 1, 3, 2, 4).astype(BF16)


def _softmax2_rows(s, mask):
    m = jnp.max(s, axis=-1, keepdims=True)
    p = jnp.where(mask, jnp.exp2(s - m), 0.0)
    l = jnp.sum(p, axis=-1, keepdims=True)
    return p * jnp.where(l > 0.0, 1.0 / l, 0.0)


def _nsa_kernel(q_ref, kc_ref, vc_ref, ks_ref, vs_ref, kw_ref, vw_ref, gt_ref, o_ref, *, t_len, ns):
    tq, tk, rep = NSA_TQ, NSA_TK, NSA_REP
    rows = rep * tq
    i = pl.program_id(2)
    q0 = i * tq

    q2 = q_ref[0]
    qg = jnp.concatenate([q2[:, r * LANES:(r + 1) * LANES] for r in range(rep)], axis=0)
    rid = lax.broadcasted_iota(jnp.int32, (rows, 1), 0)
    t_i = q0 + (rid & (tq - 1))

    lane_i = lax.broadcasted_iota(jnp.int32, (1, ns), 1)
    ratio = NSA_SEL_BLOCK // NSA_CMP_STRIDE
    n_cmp = t_len // NSA_CMP_STRIDE - 1
    s_list, mask_list = [], []
    for r in range(ratio):
        c_i = lane_i * ratio + r
        mask = ((c_i * NSA_CMP_STRIDE + (NSA_CMP_BLOCK - 1)) <= t_i) & (c_i < n_cmp)
        s_list.append(jnp.where(mask, _dot_nt(qg, kc_ref[0, 0, r]), NEG_INF))
        mask_list.append(mask)
    m = functools.reduce(jnp.maximum, [jnp.max(s, axis=-1, keepdims=True) for s in s_list])
    p_list = [jnp.where(mk, jnp.exp2(s - m), 0.0) for s, mk in zip(s_list, mask_list)]
    l = functools.reduce(lambda a, b: a + b, [jnp.sum(p, axis=-1, keepdims=True) for p in p_list])
    inv = jnp.where(l > 0.0, 1.0 / l, 0.0)
    p_list = [p * inv for p in p_list]
    o_c = functools.reduce(lambda a, b: a + b,
                           [_dot(p.astype(BF16), vc_ref[0, 0, r]) for r, p in enumerate(p_list)])

    def head_sum(p):
        return functools.reduce(lambda a, b: a + b, [p[r * tq:(r + 1) * tq] for r in range(rep)])

    ps = [head_sum(p) for p in p_list]
    lane_q = lax.broadcasted_iota(jnp.int32, (tq, ns), 1)
    prev_last = jnp.where(lane_q == 0, 0.0, pltpu.roll(ps[ratio - 1], 1, 1))
    imp = prev_last + ps[0] + ps[1] + ps[2] + ps[3]
    tq_i = q0 + lax.broadcasted_iota(jnp.int32, (tq, 1), 0)
    cur = lax.shift_right_logical(tq_i, int(np.log2(NSA_SEL_BLOCK)))
    visible = lane_q * NSA_SEL_BLOCK <= tq_i
    forced = (lane_q == 0) | (lane_q == cur) | (lane_q == cur - 1)
    score = jnp.where(visible, jnp.where(forced, SEL_FORCE, imp), -1.0)

    blk_f = lax.broadcasted_iota(jnp.int32, (ns, tq), 0).astype(F32)
    work = score.T
    sel_t = jnp.zeros((ns, tq), F32)
    for _ in range(min(NSA_TOP_N, ns)):
        top = jnp.max(work, axis=0, keepdims=True)
        idx = jnp.min(jnp.where(work == top, blk_f, float(ns)), axis=0, keepdims=True)
        hit = blk_f == idx
        sel_t = jnp.where(hit, 1.0, sel_t)
        work = jnp.where(hit, -2.0, work)
    sel = jnp.where(score >= 0.0, sel_t.T, 0.0)

    unsel = ((sel - 1.0) * MASK_BIG).astype(BF16)
    if ns < LANES:
        unsel = jnp.concatenate([unsel, jnp.zeros((tq, LANES - ns), BF16)], axis=1)
    q_aug = jnp.concatenate([qg, jnp.concatenate([unsel] * rep, axis=0)], axis=1)

    def sel_tile(k0, causal):
        s = _dot_nt(q_aug, ks_ref[0, pl.ds(k0, tk), :])
        if causal:
            tok = k0 + lax.broadcasted_iota(jnp.int32, (1, tk), 1)
            s = jnp.where(tok <= t_i, s, -MASK_BIG)
        return s

    kd = pl.multiple_of(lax.shift_right_logical(q0, int(np.log2(tk))) * tk, tk)
    s = sel_tile(kd, True)
    m_s = jnp.max(s, axis=-1, keepdims=True)
    p = jnp.exp2(s - m_s)
    l_s = jnp.sum(p, axis=-1, keepdims=True)
    acc_s = _dot(p.astype(BF16), vs_ref[0, pl.ds(kd, tk), :])

    def sel_body(kt, carry):
        m_run, l_run, acc = carry
        k0 = pl.multiple_of(kt * tk, tk)
        s = sel_tile(k0, False)
        m_new = jnp.maximum(m_run, jnp.max(s, axis=-1, keepdims=True))
        p = jnp.exp2(s - m_new)
        alpha = jnp.exp2(m_run - m_new)
        l_new = alpha * l_run + jnp.sum(p, axis=-1, keepdims=True)
        acc = alpha * acc + _dot(p.astype(BF16), vs_ref[0, pl.ds(k0, tk), :])
        return m_new, l_new, acc

    n_past = lax.shift_right_logical(q0, int(np.log2(tk)))
    _, l_s, acc_s = lax.fori_loop(0, n_past, sel_body, (m_s, l_s, acc_s))
    o_s = acc_s / l_s

    wlen = NSA_WINDOW + tq
    w0 = pl.multiple_of(jnp.maximum(q0 - NSA_WINDOW, 0), tq)
    dist = t_i - (w0 + lax.broadcasted_iota(jnp.int32, (1, wlen), 1))
    mask_w = (dist >= 0) & (dist < NSA_WINDOW)
    s_w = _dot_nt(qg, kw_ref[0, pl.ds(w0, wlen), :])
    p_w = _softmax2_rows(jnp.where(mask_w, s_w, NEG_INF), mask_w)
    o_w = _dot(p_w.astype(BF16), vw_ref[0, pl.ds(w0, wlen), :])

    gt = gt_ref[0]

    def gate(branch):
        return jnp.concatenate(
            [gt[:, r * NSA_N_BRANCH + branch:r * NSA_N_BRANCH + branch + 1] for r in range(rep)], axis=0)

    o = gate(0) * o_c + gate(1) * o_s + gate(2) * o_w
    low_half = lax.broadcasted_iota(jnp.int32, (tq, LANES), 1) < HEAD_DIM
    for c in range(rep // 2):
        even = o[(2 * c) * tq:(2 * c + 1) * tq]
        odd = o[(2 * c + 1) * tq:(2 * c + 2) * tq]
        o_ref[0, :, c * LANES:(c + 1) * LANES] = jnp.where(low_half, even, odd).astype(o_ref.dtype)


def _nsa(q, kc, vc, ks, vs, kw, vw, gates):
    b, t, _ = q.shape
    g = NSA_KV_GROUPS
    ns = t // NSA_SEL_BLOCK
    assert ns <= LANES and t >= NSA_WINDOW + NSA_TQ and t % NSA_TK == 0 and NSA_TK % NSA_TQ == 0
    ratio = NSA_SEL_BLOCK // NSA_CMP_STRIDE
    qw = NSA_REP * LANES
    cmp_spec = pl.BlockSpec((1, 1, ratio, ns, LANES), lambda bi, gi, i: (bi, gi, 0, 0, 0))
    kv_spec = pl.BlockSpec((1, t, LANES), lambda bi, gi, i: (bi, 0, gi))
    ks_spec = pl.BlockSpec((1, t, 2 * LANES), lambda bi, gi, i: (bi, 0, gi))
    return pl.pallas_call(
        functools.partial(_nsa_kernel, t_len=t, ns=ns),
        out_shape=jax.ShapeDtypeStruct((b, t, GROUP_WIDTH), BF16),
        grid=(b, g, t // NSA_TQ),
        in_specs=[pl.BlockSpec((1, NSA_TQ, qw), lambda bi, gi, i: (bi, i, gi)),
                  cmp_spec, cmp_spec, ks_spec, kv_spec, kv_spec, kv_spec,
                  pl.BlockSpec((1, NSA_TQ, LANES), lambda bi, gi, i: (bi, i, gi))],
        out_specs=pl.BlockSpec((1, NSA_TQ, NSA_REP * HEAD_DIM), lambda bi, gi, i: (bi, i, gi)),
        compiler_params=_params("parallel", "parallel", "arbitrary"),
        name="nsa_attention",
    )(q, kc, vc, ks, vs, kw, vw, gates)


def _sb_tiles(qs, k, v, u, carries, mask):
    stage1 = []
    for h, qh in enumerate(qs):
        p = h // 2
        z = _dot_nt(qh, k[:, p * LANES:(p + 1) * LANES])
        soft = jnp.log2(1.0 + jnp.exp2(-jnp.abs(z)))
        log_beta = jnp.minimum(z, 0.0) - soft
        log_keep = log_beta - z
        if mask is not None:
            log_keep = jnp.where(mask, log_keep, 0.0)
        hi = log_keep.astype(BF16)
        lo = (log_keep - hi.astype(F32)).astype(BF16)
        stage1.append((log_beta, log_keep, hi, lo))
    laters = [_dot(hi, u) + _dot(lo, u) for (_, _, hi, lo) in stage1]
    out = []
    for h, ((log_beta, log_keep, _, _), later, (acc, c)) in enumerate(zip(stage1, laters, carries)):
        p = h // 2
        a = jnp.exp2(log_beta + later + c)
        if mask is not None:
            a = jnp.where(mask, a, 0.0)
        acc = acc + _dot(a.astype(BF16), v[:, p * LANES:(p + 1) * LANES])
        out.append((acc, c + jnp.sum(log_keep, axis=-1, keepdims=True)))
    return tuple(out)


def _sb_kernel(q_ref, k_ref, v_ref, u_ref, o_ref):
    tile = SB_T
    i = pl.program_id(2)
    q2 = q_ref[0]
    u = u_ref[...]
    qs = [q2[:, h * LANES:(h + 1) * LANES] for h in range(2 * SB_PAIRS)]
    q0 = pl.multiple_of(i * tile, tile)
    zero = (jnp.zeros((tile, LANES), F32), jnp.zeros((tile, 1), F32))
    mask = lax.broadcasted_iota(jnp.int32, (tile, tile), 1) < lax.broadcasted_iota(jnp.int32, (tile, tile), 0)
    carry = _sb_tiles(qs, k_ref[0, pl.ds(q0, tile), :], v_ref[0, pl.ds(q0, tile), :], u,
                      tuple(zero for _ in qs), mask)

    def body(jj, carry):
        k0 = pl.multiple_of((i - 1 - jj) * tile, tile)
        return _sb_tiles(qs, k_ref[0, pl.ds(k0, tile), :], v_ref[0, pl.ds(k0, tile), :], u, carry, None)

    carry = lax.fori_loop(0, i, body, carry)
    low_half = lax.broadcasted_iota(jnp.int32, (tile, LANES), 1) < HEAD_DIM
    for p in range(SB_PAIRS):
        o_ref[0, :, p * LANES:(p + 1) * LANES] = jnp.where(
            low_half, carry[2 * p][0], carry[2 * p + 1][0]).astype(o_ref.dtype)


def _stick_breaking(q, k, v):
    b, t, _ = q.shape
    tile = SB_T
    assert t % tile == 0
    u = (np.arange(tile)[:, None] > np.arange(tile)[None, :]).astype(np.float32)
    kv_spec = pl.BlockSpec((1, t, SB_PAIRS * LANES), lambda bi, hp, i: (bi, 0, hp))
    return pl.pallas_call(
        _sb_kernel,
        out_shape=jax.ShapeDtypeStruct((b, t, GROUP_WIDTH), BF16),
        grid=(b, SB_HEADS // (2 * SB_PAIRS), t // tile),
        in_specs=[pl.BlockSpec((1, tile, 2 * SB_PAIRS * LANES), lambda bi, hp, i: (bi, i, hp)),
                  kv_spec, kv_spec,
                  pl.BlockSpec((tile, tile), lambda bi, hp, i: (0, 0))],
        out_specs=pl.BlockSpec((1, tile, SB_PAIRS * LANES), lambda bi, hp, i: (bi, i, hp)),
        compiler_params=_params("parallel", "parallel", "arbitrary"),
        name="stick_breaking",
    )(q, k, v, jnp.asarray(u, BF16))


def _mixer_conv_nsa(x, norm_g, w_in, dw_w, dw_b, ln_g, ln_b, pe_k, w1_k, w2_k, pe_v, w1_v, w2_v, w_out):
    b, t, d = x.shape
    a, q, kc, vc, ks, vs, kw, vw, gates = _proj_ab(x, norm_g, w_in)
    a = _conformer_conv(a, dw_w, dw_b, ln_g, ln_b)
    k_cmp = _compress(kc, pe_k, w1_k, w2_k, dup=False)
    v_cmp = _compress(vc, pe_v, w1_v, w2_v, dup=True)
    o = _nsa(q, k_cmp, v_cmp, ks, vs, kw, vw, gates)
    n = b * t
    return _outproj(x.reshape(n, d), a.reshape(n, -1), o.reshape(n, -1), w_out).reshape(b, t, d)


def _mixer_shortconv_sb(x, norm_g, w_in, sc_w, w_out):
    b, t, d = x.shape
    bg, p, q, k, v = _proj_cd(x, norm_g, w_in)
    c = _short_conv(p, bg, sc_w)
    o = _stick_breaking(q, k, v)
    n = b * t
    return _outproj(x.reshape(n, d), c.reshape(n, -1), o.reshape(n, -1), w_out).reshape(b, t, d)


def kernel(x, ffn1_norm, ffn1_w_in, ffn1_w_out, mix_norm, ffn2_norm, ffn2_w_in, ffn2_w_out, ab_w_in, conv_dw_w, conv_dw_b, conv_ln_g, conv_ln_b, nsa_pe_k, nsa_w1_k, nsa_w2_k, nsa_pe_v, nsa_w1_v, nsa_w2_v, ab_w_out, cd_w_in, sc_conv_w, cd_w_out, final_norm):
    b, t, d = x.shape
    depth = ffn1_norm.shape[0]
    n = b * t
    for layer in range(depth):
        x = _ffn(x.reshape(n, d), ffn1_norm[layer], ffn1_w_in[layer], ffn1_w_out[layer]).reshape(b, t, d)
        if layer % 2 == 0:
            e = layer // 2
            x = _mixer_conv_nsa(x, mix_norm[layer], ab_w_in[e], conv_dw_w[e], conv_dw_b[e], conv_ln_g[e],
                                conv_ln_b[e], nsa_pe_k[e], nsa_w1_k[e], nsa_w2_k[e],
                                nsa_pe_v[e], nsa_w1_v[e], nsa_w2_v[e], ab_w_out[e])
        else:
            o = layer // 2
            x = _mixer_shortconv_sb(x, mix_norm[layer], cd_w_in[o], sc_conv_w[o], cd_w_out[o])
        last = layer == depth - 1
        x = _ffn(x.reshape(n, d), ffn2_norm[layer], ffn2_w_in[layer], ffn2_w_out[layer],
                 final_g=final_norm if last else None).reshape(b, t, d)
    return x
```

```python
import functools

import numpy as np
import jax
import jax.numpy as jnp
from jax import lax
from jax.experimental import pallas as pl
from jax.experimental.pallas import tpu as pltpu

F32 = jnp.float32
BF16 = jnp.bfloat16

D_MODEL = 1024
HEAD_DIM = 64
GROUP_WIDTH = D_MODEL // 2
CONV_WIDTH = 31
NSA_HEADS = GROUP_WIDTH // HEAD_DIM
NSA_KV_GROUPS = 2
NSA_REP = NSA_HEADS // NSA_KV_GROUPS
NSA_CMP_BLOCK = 32
NSA_CMP_STRIDE = 16
NSA_CMP_HIDDEN = 128
NSA_SEL_BLOCK = 64
NSA_TOP_N = 16
NSA_WINDOW = 512
NSA_N_BRANCH = 3
SC_WIDTH = 3
SB_HEADS = GROUP_WIDTH // HEAD_DIM
D_FF = 2816
RMS_EPS = 1e-6
LN_EPS = 1e-5
NEG_INF = -1e30
SEL_FORCE = 1e4
QK_SCALE = HEAD_DIM ** -0.5
LOG2_E = float(np.log2(np.e))

LANES = 128
VMEM_LIMIT = 48 * 1024 * 1024

FFN_TM = 1024
FFN_TF = 256
PROJ_TM = 512
OUT_TM = 1024
CONV_TT = 256
CONV_HALO = 32
CONV_CHUNK = 64
SC_TT = 512
SC_HALO = 8
NSA_TQ = 128
NSA_TK = 512
SB_T = 256
SB_PAIRS = 2


def _params(*sem):
    return pltpu.CompilerParams(dimension_semantics=sem, vmem_limit_bytes=VMEM_LIMIT)


def _dot(a, b):
    return jnp.dot(a, b, preferred_element_type=F32)


def _dot_nt(a, b):
    return lax.dot_general(a, b, (((1,), (1,)), ((), ())), preferred_element_type=F32)


def _sigmoid(x):
    return 1.0 / (1.0 + jnp.exp(-x))


def _rmsnorm_rows(x, g):
    return x * lax.rsqrt(jnp.mean(x * x, axis=-1, keepdims=True) + RMS_EPS) * g


def _ffn_kernel(x_ref, g_ref, wg_ref, wu_ref, wo_ref, *rest, n_ff, final_norm):
    if final_norm:
        fg_ref, o_ref, xn_ref, acc_ref = rest
    else:
        o_ref, xn_ref, acc_ref = rest
    j = pl.program_id(1)

    @pl.when(j == 0)
    def _():
        xn_ref[...] = _rmsnorm_rows(x_ref[...], g_ref[...]).astype(BF16)
        acc_ref[...] = jnp.zeros_like(acc_ref)

    xn = xn_ref[...]
    gate = _dot(xn, wg_ref[...])
    up = _dot(xn, wu_ref[...])
    h = gate * _sigmoid(gate) * up
    acc_ref[...] += _dot(h.astype(BF16), wo_ref[...])

    @pl.when(j == n_ff - 1)
    def _():
        y = x_ref[...] + 0.5 * acc_ref[...]
        if final_norm:
            y = _rmsnorm_rows(y, fg_ref[...])
        o_ref[...] = y


def _ffn(x, g, w_in, w_out, final_g=None):
    n, d = x.shape
    n_ff = D_FF // FFN_TF
    tm = min(FFN_TM, n)
    w_in = w_in.astype(BF16)
    w_out = w_out.astype(BF16)
    in_specs = [
        pl.BlockSpec((tm, d), lambda i, j: (i, 0)),
        pl.BlockSpec((1, d), lambda i, j: (0, 0)),
        pl.BlockSpec((d, FFN_TF), lambda i, j: (0, j)),
        pl.BlockSpec((d, FFN_TF), lambda i, j: (0, j + n_ff)),
        pl.BlockSpec((FFN_TF, d), lambda i, j: (j, 0)),
    ]
    args = [x, g.reshape(1, d), w_in, w_in, w_out]
    if final_g is not None:
        in_specs.append(pl.BlockSpec((1, d), lambda i, j: (0, 0)))
        args.append(final_g.reshape(1, d))
    return pl.pallas_call(
        functools.partial(_ffn_kernel, n_ff=n_ff, final_norm=final_g is not None),
        out_shape=jax.ShapeDtypeStruct((n, d), F32),
        grid=(n // tm, n_ff),
        in_specs=in_specs,
        out_specs=pl.BlockSpec((tm, d), lambda i, j: (i, 0)),
        scratch_shapes=[pltpu.VMEM((tm, d), BF16), pltpu.VMEM((tm, d), F32)],
        compiler_params=_params("parallel", "arbitrary"),
        name="ffn",
    )(*args)


AB_A = 0
AB_Q = 1024
AB_KC = 2048
AB_VC = 2176
AB_KS = 2304
AB_VS = 2560
AB_KW = 2816
AB_VW = 3072
AB_G = 3328
AB_COLS = 3584

POS_SPLIT = 3
POS_TOK = HEAD_DIM
POS_CMP = POS_TOK + 2 * POS_SPLIT
MASK_BIG = 2.0 ** 100


def _pos_lanes(hi, lo, base):
    lane = lax.broadcasted_iota(jnp.int32, (1, LANES), 1)
    in_hi = (lane >= base) & (lane < base + POS_SPLIT)
    in_lo = (lane >= base + POS_SPLIT) & (lane < base + 2 * POS_SPLIT)
    return jnp.where(in_hi, hi, jnp.where(in_lo, lo, 0.0))


def _proj_ab_kernel(x_ref, g_ref, w_ref, qb_ref, a_ref, q_ref, kc_ref, vc_ref, ks_ref, vs_ref, kw_ref, vw_ref,
                    gt_ref, *, tm):
    xn = _rmsnorm_rows(x_ref[0], g_ref[...]).astype(BF16)

    def seg(lo, hi):
        return _dot(xn, w_ref[:, lo:hi])

    av = seg(AB_A, AB_A + 2 * GROUP_WIDTH)
    a_ref[0] = av[:, :GROUP_WIDTH] * _sigmoid(av[:, GROUP_WIDTH:])
    q_ref[0] = (seg(AB_Q, AB_KC) + qb_ref[...]).astype(BF16)
    kc_ref[0] = seg(AB_KC, AB_VC)
    vc_ref[0] = seg(AB_VC, AB_KS)
    vs_ref[0] = seg(AB_VS, AB_KW).astype(BF16)
    vw_ref[0] = seg(AB_VW, AB_G).astype(BF16)
    gt_ref[0] = _sigmoid(seg(AB_G, AB_COLS))

    tok = pl.program_id(1) * tm + lax.broadcasted_iota(jnp.int32, (tm, 1), 0)
    blk = lax.shift_right_logical(tok, int(np.log2(NSA_SEL_BLOCK)))
    pos = _pos_lanes(blk.astype(F32), (tok & (NSA_SEL_BLOCK - 1)).astype(F32), POS_TOK)
    onehot = jnp.where(lax.broadcasted_iota(jnp.int32, (1, LANES), 1) == blk, 1.0, 0.0).astype(BF16)
    ks = seg(AB_KS, AB_VS)
    kw = seg(AB_KW, AB_VW)
    for g in range(NSA_KV_GROUPS):
        ks_ref[0, :, 2 * g * LANES:(2 * g + 1) * LANES] = (ks[:, g * LANES:(g + 1) * LANES] + pos).astype(BF16)
        ks_ref[0, :, (2 * g + 1) * LANES:(2 * g + 2) * LANES] = onehot
        kw_ref[0, :, g * LANES:(g + 1) * LANES] = (kw[:, g * LANES:(g + 1) * LANES] + pos).astype(BF16)


def _arrange_ab_weight(w):
    d = w.shape[0]
    kvw = NSA_KV_GROUPS * HEAD_DIM
    o = 2 * GROUP_WIDTH
    a = w[:, :o]
    q = w[:, o:o + GROUP_WIDTH].reshape(d, NSA_HEADS, HEAD_DIM) * (QK_SCALE * LOG2_E)
    o += GROUP_WIDTH
    kc, vc, ks, vs, kw, vw = [w[:, o + i * kvw:o + (i + 1) * kvw] for i in range(6)]
    o += 6 * kvw
    g = w[:, o:].reshape(d, NSA_KV_GROUPS, NSA_REP * NSA_N_BRANCH)

    zeros_h = jnp.zeros((d, NSA_HEADS, HEAD_DIM), w.dtype)
    q_pad = jnp.concatenate([q, zeros_h], axis=-1).reshape(d, NSA_HEADS * LANES)

    def k_pad(k):
        k = k.reshape(d, NSA_KV_GROUPS, HEAD_DIM)
        return jnp.concatenate([k, jnp.zeros_like(k)], axis=-1).reshape(d, NSA_KV_GROUPS * LANES)

    def v_dup(v):
        v = v.reshape(d, NSA_KV_GROUPS, HEAD_DIM)
        return jnp.concatenate([v, v], axis=-1).reshape(d, NSA_KV_GROUPS * LANES)

    g_pad = jnp.pad(g, ((0, 0), (0, 0), (0, LANES - g.shape[-1]))).reshape(d, NSA_KV_GROUPS * LANES)
    out = jnp.concatenate([a, q_pad, kc, vc, k_pad(ks), v_dup(vs), k_pad(kw), v_dup(vw), g_pad], axis=-1)
    assert out.shape[1] == AB_COLS
    return out.astype(BF16)


def _bf16_terms(x, n):
    terms, rest = [], np.asarray(x, np.float64)
    for _ in range(n):
        term = rest.astype(BF16).astype(np.float64)
        terms.append(term)
        rest = rest - term
    return terms


def _alibi_query_lanes():
    row = np.zeros((NSA_HEADS, LANES), np.float64)
    for h in range(NSA_HEADS):
        slope = 2.0 ** (-8.0 * (h + 1) / NSA_HEADS)
        for i, term in enumerate(_bf16_terms(slope * np.log2(np.e), POS_SPLIT)):
            row[h, POS_TOK + i] = NSA_SEL_BLOCK * term
            row[h, POS_TOK + POS_SPLIT + i] = term
            row[h, POS_CMP + i] = 2 * NSA_CMP_STRIDE * term
            row[h, POS_CMP + POS_SPLIT + i] = NSA_CMP_STRIDE * term
    return jnp.asarray(row.reshape(1, NSA_HEADS * LANES), F32)


def _proj_ab(x, g, w):
    b, t, d = x.shape
    tm = min(PROJ_TM, t)
    wa = _arrange_ab_weight(w)

    def tok(width):
        return pl.BlockSpec((1, tm, width), lambda bi, i: (bi, i, 0))

    const = lambda width: pl.BlockSpec((1, width), lambda bi, i: (0, 0))
    kv = NSA_KV_GROUPS * LANES
    widths = [GROUP_WIDTH, NSA_HEADS * LANES, LANES, LANES, 2 * kv, kv, kv, kv, kv]
    dtypes = [F32, BF16, F32, F32, BF16, BF16, BF16, BF16, F32]
    return pl.pallas_call(
        functools.partial(_proj_ab_kernel, tm=tm),
        out_shape=[jax.ShapeDtypeStruct((b, t, wd), dt) for wd, dt in zip(widths, dtypes)],
        grid=(b, t // tm),
        in_specs=[tok(d), const(d), pl.BlockSpec((d, AB_COLS), lambda bi, i: (0, 0)), const(NSA_HEADS * LANES)],
        out_specs=[tok(wd) for wd in widths],
        compiler_params=_params("parallel", "parallel"),
        name="proj_ab",
    )(x, g.reshape(1, d), wa, _alibi_query_lanes())


CD_B = 0
CD_C = 512
CD_U = 1024
CD_Q = 1536
CD_K = 2560
CD_V = 3072
CD_COLS = 3584


def _proj_cd_kernel(x_ref, g_ref, w_ref, bg_ref, p_ref, q_ref, k_ref, v_ref):
    xn = _rmsnorm_rows(x_ref[0], g_ref[...]).astype(BF16)

    def seg(lo, hi):
        return _dot(xn, w_ref[:, lo:hi])

    bg_ref[0] = seg(CD_B, CD_C)
    p_ref[0] = seg(CD_C, CD_U) * seg(CD_U, CD_Q)
    q_ref[0] = seg(CD_Q, CD_K).astype(BF16)
    k_ref[0] = seg(CD_K, CD_V).astype(BF16)
    v_ref[0] = seg(CD_V, CD_COLS).astype(BF16)


def _arrange_cd_weight(w):
    d = w.shape[0]
    gw = GROUP_WIDTH
    q = w[:, 3 * gw:4 * gw].reshape(d, SB_HEADS // 2, 2, HEAD_DIM) * (QK_SCALE * LOG2_E)
    z = jnp.zeros((d, SB_HEADS // 2, HEAD_DIM), w.dtype)
    q_even = jnp.concatenate([q[:, :, 0], z], axis=-1)
    q_odd = jnp.concatenate([z, q[:, :, 1]], axis=-1)
    q_pad = jnp.stack([q_even, q_odd], axis=2).reshape(d, SB_HEADS * LANES)
    out = jnp.concatenate([w[:, :3 * gw], q_pad, w[:, 4 * gw:]], axis=-1)
    assert out.shape[1] == CD_COLS
    return out.astype(BF16)


def _proj_cd(x, g, w):
    b, t, d = x.shape
    tm = min(PROJ_TM, t)
    wa = _arrange_cd_weight(w)

    def tok(width):
        return pl.BlockSpec((1, tm, width), lambda bi, i: (bi, i, 0))

    widths = [GROUP_WIDTH, GROUP_WIDTH, SB_HEADS * LANES, GROUP_WIDTH, GROUP_WIDTH]
    dtypes = [F32, F32, BF16, BF16, BF16]
    return pl.pallas_call(
        _proj_cd_kernel,
        out_shape=[jax.ShapeDtypeStruct((b, t, wd), dt) for wd, dt in zip(widths, dtypes)],
        grid=(b, t // tm),
        in_specs=[tok(d), pl.BlockSpec((1, d), lambda bi, i: (0, 0)),
                  pl.BlockSpec((d, CD_COLS), lambda bi, i: (0, 0))],
        out_specs=[tok(wd) for wd in widths],
        compiler_params=_params("parallel", "parallel"),
        name="proj_cd",
    )(x, g.reshape(1, d), wa)


def _conv_kernel(cur_ref, halo_ref, w_ref, b_ref, lg_ref, lb_ref, o_ref, ext_ref, *, tt):
    i = pl.program_id(1)
    halo = halo_ref[0]
    ext_ref[0:CONV_HALO, :] = jnp.where(i == 0, jnp.zeros_like(halo), halo)
    ext_ref[CONV_HALO:, :] = cur_ref[0]
    off = CONV_HALO - (CONV_WIDTH - 1)
    for c in range(tt // CONV_CHUNK):
        r0 = c * CONV_CHUNK
        acc = jnp.zeros((CONV_CHUNK, GROUP_WIDTH), F32)
        for k in range(CONV_WIDTH):
            acc = acc + w_ref[k:k + 1, :] * ext_ref[r0 + off + k:r0 + off + k + CONV_CHUNK, :]
        y = acc + b_ref[...]
        mu = jnp.mean(y, axis=-1, keepdims=True)
        yc = y - mu
        var = jnp.mean(yc * yc, axis=-1, keepdims=True)
        yn = yc * lax.rsqrt(var + LN_EPS) * lg_ref[...] + lb_ref[...]
        o_ref[0, r0:r0 + CONV_CHUNK, :] = (yn * _sigmoid(yn)).astype(o_ref.dtype)


def _conformer_conv(a, w, bias, ln_g, ln_b):
    b, t, c = a.shape
    tt = min(CONV_TT, t)
    hb = tt // CONV_HALO
    row = lambda v: v.reshape(1, c)
    const = lambda shape: pl.BlockSpec(shape, lambda bi, i: (0, 0))
    return pl.pallas_call(
        functools.partial(_conv_kernel, tt=tt),
        out_shape=jax.ShapeDtypeStruct((b, t, c), BF16),
        grid=(b, t // tt),
        in_specs=[pl.BlockSpec((1, tt, c), lambda bi, i: (bi, i, 0)),
                  pl.BlockSpec((1, CONV_HALO, c), lambda bi, i: (bi, jnp.maximum(i * hb - 1, 0), 0)),
                  const((CONV_WIDTH, c)), const((1, c)), const((1, c)), const((1, c))],
        out_specs=pl.BlockSpec((1, tt, c), lambda bi, i: (bi, i, 0)),
        scratch_shapes=[pltpu.VMEM((tt + CONV_HALO, c), F32)],
        compiler_params=_params("parallel", "parallel"),
        name="conformer_conv",
    )(a, a, w, row(bias), row(ln_g), row(ln_b))


def _sc_kernel(p_ref, halo_ref, bg_ref, w_ref, o_ref, ext_ref, *, tt):
    i = pl.program_id(1)
    halo = halo_ref[0]
    ext_ref[0:SC_HALO, :] = jnp.where(i == 0, jnp.zeros_like(halo), halo)
    ext_ref[SC_HALO:, :] = p_ref[0]
    off = SC_HALO - (SC_WIDTH - 1)
    acc = jnp.zeros((tt, GROUP_WIDTH), F32)
    for k in range(SC_WIDTH):
        acc = acc + w_ref[k:k + 1, :] * ext_ref[off + k:off + k + tt, :]
    o_ref[0] = (bg_ref[0] * acc).astype(o_ref.dtype)


def _short_conv(p, bg, w):
    b, t, c = p.shape
    tt = min(SC_TT, t)
    hb = tt // SC_HALO
    return pl.pallas_call(
        functools.partial(_sc_kernel, tt=tt),
        out_shape=jax.ShapeDtypeStruct((b, t, c), BF16),
        grid=(b, t // tt),
        in_specs=[pl.BlockSpec((1, tt, c), lambda bi, i: (bi, i, 0)),
                  pl.BlockSpec((1, SC_HALO, c), lambda bi, i: (bi, jnp.maximum(i * hb - 1, 0), 0)),
                  pl.BlockSpec((1, tt, c), lambda bi, i: (bi, i, 0)),
                  pl.BlockSpec((SC_WIDTH, c), lambda bi, i: (0, 0))],
        out_specs=pl.BlockSpec((1, tt, c), lambda bi, i: (bi, i, 0)),
        scratch_shapes=[pltpu.VMEM((tt + SC_HALO, c), F32)],
        compiler_params=_params("parallel", "parallel"),
        name="short_conv",
    )(p, p, bg, w)


def _outproj_kernel(x_ref, a_ref, o_ref, wa_ref, wo_ref, y_ref):
    y_ref[...] = x_ref[...] + _dot(a_ref[...], wa_ref[...]) + _dot(o_ref[...], wo_ref[...])


def _outproj(x, a, o, w):
    n, d = x.shape
    gw = a.shape[1]
    tm = min(OUT_TM, n)
    w = w.astype(BF16)
    tok = lambda width: pl.BlockSpec((tm, width), lambda i: (i, 0))
    return pl.pallas_call(
        _outproj_kernel,
        out_shape=jax.ShapeDtypeStruct((n, d), F32),
        grid=(n // tm,),
        in_specs=[tok(d), tok(gw), tok(gw),
                  pl.BlockSpec((gw, d), lambda i: (0, 0)), pl.BlockSpec((gw, d), lambda i: (1, 0))],
        out_specs=tok(d),
        compiler_params=_params("parallel"),
        name="outproj",
    )(x, a, o, w, w)


def _gelu_tanh(x):
    return 0.5 * x * (1.0 + jnp.tanh(np.sqrt(2.0 / np.pi).astype(np.float32) * (x + 0.044715 * (x * x * x))))


def _compress_kernel(x_ref, pe_ref, w1_ref, w2_ref, o_ref, *, nchunk, keys):
    half = NSA_CMP_STRIDE * HEAD_DIM
    x = x_ref[0, 0]
    ha = _dot((x + pe_ref[:, :half]).astype(BF16), w1_ref[:half, :])
    hb = _dot((x + pe_ref[:, half:]).astype(BF16), w1_ref[half:, :])
    h = ha + pltpu.roll(hb, nchunk - 1, 0)
    y = _dot(_gelu_tanh(h).astype(BF16), w2_ref[...])
    row = lax.broadcasted_iota(jnp.int32, (nchunk, 1), 0)
    y = jnp.where(row < nchunk - 1, y, 0.0)
    if keys:
        y = y + _pos_lanes(lax.shift_right_logical(row, 1).astype(F32), (row & 1).astype(F32), POS_CMP)
    o_ref[0, 0] = y


def _compress(kx, pe, w1, w2, dup):
    b, t, _ = kx.shape
    g = NSA_KV_GROUPS
    nchunk = t // NSA_CMP_STRIDE
    feat = NSA_CMP_STRIDE * HEAD_DIM
    x = kx.reshape(b, nchunk, NSA_CMP_STRIDE, g, HEAD_DIM).transpose(0, 3, 1, 2, 4).reshape(b, g, nchunk, feat)
    w2p = jnp.concatenate([w2, w2 if dup else jnp.zeros_like(w2)], axis=-1).astype(BF16)
    y = pl.pallas_call(
        functools.partial(_compress_kernel, nchunk=nchunk, keys=not dup),
        out_shape=jax.ShapeDtypeStruct((b, g, nchunk, LANES), F32),
        grid=(b, g),
        in_specs=[pl.BlockSpec((1, 1, nchunk, feat), lambda bi, gi: (bi, gi, 0, 0)),
                  pl.BlockSpec((1, 2 * feat), lambda bi, gi: (0, 0)),
                  pl.BlockSpec((2 * feat, NSA_CMP_HIDDEN), lambda bi, gi: (0, 0)),
                  pl.BlockSpec((NSA_CMP_HIDDEN, LANES), lambda bi, gi: (0, 0))],
        out_specs=pl.BlockSpec((1, 1, nchunk, LANES), lambda bi, gi: (bi, gi, 0, 0)),
        compiler_params=_params("parallel", "parallel"),
        name="nsa_compress",
    )(x, pe.reshape(1, 2 * feat), w1.reshape(2 * feat, NSA_CMP_HIDDEN).astype(BF16), w2p)
    ratio = NSA_SEL_BLOCK // NSA_CMP_STRIDE
    return y.reshape(b, g, nchunk // ratio, ratio, LANES).transpose(0, 1, 3, 2, 4).astype(BF16)


def _softmax2_rows(s, mask):
    m = jnp.max(s, axis=-1, keepdims=True)
    p = jnp.where(mask, jnp.exp2(s - m), 0.0)
    l = jnp.sum(p, axis=-1, keepdims=True)
    return p * jnp.where(l > 0.0, 1.0 / l, 0.0)


def _nsa_kernel(q_ref, kc_ref, vc_ref, ks_ref, vs_ref, kw_ref, vw_ref, gt_ref, o_ref, *, t_len, ns):
    tq, tk, rep = NSA_TQ, NSA_TK, NSA_REP
    rows = rep * tq
    i = pl.program_id(2)
    q0 = i * tq

    q2 = q_ref[0]
    qg = jnp.concatenate([q2[:, r * LANES:(r + 1) * LANES] for r in range(rep)], axis=0)
    rid = lax.broadcasted_iota(jnp.int32, (rows, 1), 0)
    t_i = q0 + (rid & (tq - 1))

    lane_i = lax.broadcasted_iota(jnp.int32, (1, ns), 1)
    ratio = NSA_SEL_BLOCK // NSA_CMP_STRIDE
    n_cmp = t_len // NSA_CMP_STRIDE - 1
    s_list, mask_list = [], []
    for r in range(ratio):
        c_i = lane_i * ratio + r
        mask = ((c_i * NSA_CMP_STRIDE + (NSA_CMP_BLOCK - 1)) <= t_i) & (c_i < n_cmp)
        s_list.append(jnp.where(mask, _dot_nt(qg, kc_ref[0, 0, r]), NEG_INF))
        mask_list.append(mask)
    m = functools.reduce(jnp.maximum, [jnp.max(s, axis=-1, keepdims=True) for s in s_list])
    p_list = [jnp.where(mk, jnp.exp2(s - m), 0.0) for s, mk in zip(s_list, mask_list)]
    l = functools.reduce(lambda a, b: a + b, [jnp.sum(p, axis=-1, keepdims=True) for p in p_list])
    inv = jnp.where(l > 0.0, 1.0 / l, 0.0)
    p_list = [p * inv for p in p_list]
    o_c = functools.reduce(lambda a, b: a + b,
                           [_dot(p.astype(BF16), vc_ref[0, 0, r]) for r, p in enumerate(p_list)])

    def head_sum(p):
        return functools.reduce(lambda a, b: a + b, [p[r * tq:(r + 1) * tq] for r in range(rep)])

    ps = [head_sum(p) for p in p_list]
    lane_q = lax.broadcasted_iota(jnp.int32, (tq, ns), 1)
    prev_last = jnp.where(lane_q == 0, 0.0, pltpu.roll(ps[ratio - 1], 1, 1))
    imp = prev_last + ps[0] + ps[1] + ps[2] + ps[3]
    tq_i = q0 + lax.broadcasted_iota(jnp.int32, (tq, 1), 0)
    cur = lax.shift_right_logical(tq_i, int(np.log2(NSA_SEL_BLOCK)))
    visible = lane_q * NSA_SEL_BLOCK <= tq_i
    forced = (lane_q == 0) | (lane_q == cur) | (lane_q == cur - 1)
    score = jnp.where(visible, jnp.where(forced, SEL_FORCE, imp), -1.0)

    blk_f = lax.broadcasted_iota(jnp.int32, (ns, tq), 0).astype(F32)
    work = score.T
    sel_t = jnp.zeros((ns, tq), F32)
    for _ in range(min(NSA_TOP_N, ns)):
        top = jnp.max(work, axis=0, keepdims=True)
        idx = jnp.min(jnp.where(work == top, blk_f, float(ns)), axis=0, keepdims=True)
        hit = blk_f == idx
        sel_t = jnp.where(hit, 1.0, sel_t)
        work = jnp.where(hit, -2.0, work)
    sel = jnp.where(score >= 0.0, sel_t.T, 0.0)

    unsel = ((sel - 1.0) * MASK_BIG).astype(BF16)
    if ns < LANES:
        unsel = jnp.concatenate([unsel, jnp.zeros((tq, LANES - ns), BF16)], axis=1)
    q_aug = jnp.concatenate([qg, jnp.concatenate([unsel] * rep, axis=0)], axis=1)

    def sel_tile(k0, causal):
        s = _dot_nt(q_aug, ks_ref[0, pl.ds(k0, tk), :])
        if causal:
            tok = k0 + lax.broadcasted_iota(jnp.int32, (1, tk), 1)
            s = jnp.where(tok <= t_i, s, -MASK_BIG)
        return s

    blocks_per_tile = tk // NSA_SEL_BLOCK
    tile_of_blk = lax.shift_right_logical(lax.broadcasted_iota(jnp.int32, (ns, LANES), 0),
                                          int(np.log2(blocks_per_tile)))
    lane_t = lax.broadcasted_iota(jnp.int32, (ns, LANES), 1)
    per_tile = _dot(sel.astype(BF16), jnp.where(tile_of_blk == lane_t, 1.0, 0.0).astype(BF16))
    tile_any = jnp.max(per_tile, axis=0, keepdims=True) > 0.0
    pow2 = lax.shift_left(jnp.ones((1, LANES), jnp.int32), lane_t[:1] & 15).astype(F32)
    tile_bits = jnp.sum(jnp.where(tile_any, pow2, 0.0), axis=-1, keepdims=True).astype(jnp.int32)[0, 0]

    n_past = lax.shift_right_logical(q0, int(np.log2(tk)))
    kd = pl.multiple_of(n_past * tk, tk)
    s = sel_tile(kd, True)
    m_s = jnp.max(s, axis=-1, keepdims=True)
    p = jnp.exp2(s - m_s)
    l_s = jnp.sum(p, axis=-1, keepdims=True)
    acc_s = _dot(p.astype(BF16), vs_ref[0, pl.ds(kd, tk), :])

    def sel_update(kt, carry):
        m_run, l_run, acc = carry
        k0 = pl.multiple_of(kt * tk, tk)
        s = sel_tile(k0, False)
        m_new = jnp.maximum(m_run, jnp.max(s, axis=-1, keepdims=True))
        p = jnp.exp2(s - m_new)
        alpha = jnp.exp2(m_run - m_new)
        l_new = alpha * l_run + jnp.sum(p, axis=-1, keepdims=True)
        acc = alpha * acc + _dot(p.astype(BF16), vs_ref[0, pl.ds(k0, tk), :])
        return m_new, l_new, acc

    def next_active(state):
        return lax.while_loop(lambda st: (st[0] & 1) == 0,
                              lambda st: (lax.shift_right_logical(st[0], 1), st[1] + 1), state)

    def sel_body(state):
        pending, kt = next_active(state[:2])
        return (lax.shift_right_logical(pending, 1), kt + 1) + sel_update(kt, state[2:])

    past_bits = tile_bits & (lax.shift_left(jnp.int32(1), n_past) - 1)
    _, _, _, l_s, acc_s = lax.while_loop(lambda st: st[0] != 0, sel_body,
                                         (past_bits, jnp.int32(0), m_s, l_s, acc_s))
    o_s = acc_s / l_s

    wlen = NSA_WINDOW + tq
    w0 = pl.multiple_of(jnp.maximum(q0 - NSA_WINDOW, 0), tq)
    dist = t_i - (w0 + lax.broadcasted_iota(jnp.int32, (1, wlen), 1))
    mask_w = (dist >= 0) & (dist < NSA_WINDOW)
    s_w = _dot_nt(qg, kw_ref[0, pl.ds(w0, wlen), :])
    p_w = _softmax2_rows(jnp.where(mask_w, s_w, NEG_INF), mask_w)
    o_w = _dot(p_w.astype(BF16), vw_ref[0, pl.ds(w0, wlen), :])

    gt = gt_ref[0]

    def gate(branch):
        return jnp.concatenate(
            [gt[:, r * NSA_N_BRANCH + branch:r * NSA_N_BRANCH + branch + 1] for r in range(rep)], axis=0)

    o = gate(0) * o_c + gate(1) * o_s + gate(2) * o_w
    low_half = lax.broadcasted_iota(jnp.int32, (tq, LANES), 1) < HEAD_DIM
    for c in range(rep // 2):
        even = o[(2 * c) * tq:(2 * c + 1) * tq]
        odd = o[(2 * c + 1) * tq:(2 * c + 2) * tq]
        o_ref[0, :, c * LANES:(c + 1) * LANES] = jnp.where(low_half, even, odd).astype(o_ref.dtype)


def _nsa(q, kc, vc, ks, vs, kw, vw, gates):
    b, t, _ = q.shape
    g = NSA_KV_GROUPS
    ns = t // NSA_SEL_BLOCK
    assert ns <= LANES and t >= NSA_WINDOW + NSA_TQ and t % NSA_TK == 0 and NSA_TK % NSA_TQ == 0
    assert t // NSA_TK <= 16
    ratio = NSA_SEL_BLOCK // NSA_CMP_STRIDE
    qw = NSA_REP * LANES
    cmp_spec = pl.BlockSpec((1, 1, ratio, ns, LANES), lambda bi, gi, i: (bi, gi, 0, 0, 0))
    kv_spec = pl.BlockSpec((1, t, LANES), lambda bi, gi, i: (bi, 0, gi))
    ks_spec = pl.BlockSpec((1, t, 2 * LANES), lambda bi, gi, i: (bi, 0, gi))
    return pl.pallas_call(
        functools.partial(_nsa_kernel, t_len=t, ns=ns),
        out_shape=jax.ShapeDtypeStruct((b, t, GROUP_WIDTH), BF16),
        grid=(b, g, t // NSA_TQ),
        in_specs=[pl.BlockSpec((1, NSA_TQ, qw), lambda bi, gi, i: (bi, i, gi)),
                  cmp_spec, cmp_spec, ks_spec, kv_spec, kv_spec, kv_spec,
                  pl.BlockSpec((1, NSA_TQ, LANES), lambda bi, gi, i: (bi, i, gi))],
        out_specs=pl.BlockSpec((1, NSA_TQ, NSA_REP * HEAD_DIM), lambda bi, gi, i: (bi, i, gi)),
        compiler_params=_params("parallel", "parallel", "arbitrary"),
        name="nsa_attention",
    )(q, kc, vc, ks, vs, kw, vw, gates)


def _sb_tiles(qs, k, v, u2, carries, mask):
    stage1 = []
    for h, qh in enumerate(qs):
        p = h // 2
        z = _dot_nt(qh, k[:, p * LANES:(p + 1) * LANES])
        nk = jnp.maximum(z, 0.0) + jnp.log2(1.0 + jnp.exp2(-jnp.abs(z)))
        if mask is not None:
            nk = jnp.where(mask, nk, 0.0)
        hi = nk.astype(BF16)
        lo = (nk - hi.astype(F32)).astype(BF16)
        stage1.append((z, jnp.concatenate([hi, lo], axis=1)))
    laters = [_dot(hilo, u2) for (_, hilo) in stage1]
    out = []
    for h, ((z, _), later, (acc, c)) in enumerate(zip(stage1, laters, carries)):
        p = h // 2
        a = jnp.exp2(z + later)
        if mask is not None:
            a = jnp.where(mask, a, 0.0)
        acc = acc + jnp.exp2(c) * _dot(a.astype(BF16), v[:, p * LANES:(p + 1) * LANES])
        out.append((acc, c + later[:, 0:1]))
    return tuple(out)


def _sb_kernel(q_ref, k_ref, v_ref, u_ref, o_ref):
    tile = SB_T
    i = pl.program_id(2)
    q2 = q_ref[0]
    u = u_ref[...]
    qs = [q2[:, h * LANES:(h + 1) * LANES] for h in range(2 * SB_PAIRS)]
    q0 = pl.multiple_of(i * tile, tile)
    zero = (jnp.zeros((tile, LANES), F32), jnp.zeros((tile, 1), F32))
    mask = lax.broadcasted_iota(jnp.int32, (tile, tile), 1) < lax.broadcasted_iota(jnp.int32, (tile, tile), 0)
    carry = _sb_tiles(qs, k_ref[0, pl.ds(q0, tile), :], v_ref[0, pl.ds(q0, tile), :], u,
                      tuple(zero for _ in qs), mask)

    def body(jj, carry):
        k0 = pl.multiple_of((i - 1 - jj) * tile, tile)
        return _sb_tiles(qs, k_ref[0, pl.ds(k0, tile), :], v_ref[0, pl.ds(k0, tile), :], u, carry, None)

    carry = lax.fori_loop(0, i // 2, lambda jj, c: body(2 * jj + 1, body(2 * jj, c)), carry)
    carry = lax.fori_loop(0, i & 1, lambda jj, c: body(i - 1, c), carry)
    low_half = lax.broadcasted_iota(jnp.int32, (tile, LANES), 1) < HEAD_DIM
    for p in range(SB_PAIRS):
        o_ref[0, :, p * LANES:(p + 1) * LANES] = jnp.where(
            low_half, carry[2 * p][0], carry[2 * p + 1][0]).astype(o_ref.dtype)


def _stick_breaking(q, k, v):
    b, t, _ = q.shape
    tile = SB_T
    assert t % tile == 0
    u = -(np.arange(tile)[:, None] >= np.arange(tile)[None, :]).astype(np.float32)
    u2 = np.concatenate([u, u], axis=0)
    kv_spec = pl.BlockSpec((1, t, SB_PAIRS * LANES), lambda bi, hp, i: (bi, 0, hp))
    return pl.pallas_call(
        _sb_kernel,
        out_shape=jax.ShapeDtypeStruct((b, t, GROUP_WIDTH), BF16),
        grid=(b, SB_HEADS // (2 * SB_PAIRS), t // tile),
        in_specs=[pl.BlockSpec((1, tile, 2 * SB_PAIRS * LANES), lambda bi, hp, i: (bi, i, hp)),
                  kv_spec, kv_spec,
                  pl.BlockSpec((2 * tile, tile), lambda bi, hp, i: (0, 0))],
        out_specs=pl.BlockSpec((1, tile, SB_PAIRS * LANES), lambda bi, hp, i: (bi, i, hp)),
        compiler_params=_params("parallel", "parallel", "arbitrary"),
        name="stick_breaking",
    )(q, k, v, jnp.asarray(u2, BF16))


def _mixer_conv_nsa(x, norm_g, w_in, dw_w, dw_b, ln_g, ln_b, pe_k, w1_k, w2_k, pe_v, w1_v, w2_v, w_out):
    b, t, d = x.shape
    a, q, kc, vc, ks, vs, kw, vw, gates = _proj_ab(x, norm_g, w_in)
    a = _conformer_conv(a, dw_w, dw_b, ln_g, ln_b)
    k_cmp = _compress(kc, pe_k, w1_k, w2_k, dup=False)
    v_cmp = _compress(vc, pe_v, w1_v, w2_v, dup=True)
    o = _nsa(q, k_cmp, v_cmp, ks, vs, kw, vw, gates)
    n = b * t
    return _outproj(x.reshape(n, d), a.reshape(n, -1), o.reshape(n, -1), w_out).reshape(b, t, d)


def _mixer_shortconv_sb(x, norm_g, w_in, sc_w, w_out):
    b, t, d = x.shape
    bg, p, q, k, v = _proj_cd(x, norm_g, w_in)
    c = _short_conv(p, bg, sc_w)
    o = _stick_breaking(q, k, v)
    n = b * t
    return _outproj(x.reshape(n, d), c.reshape(n, -1), o.reshape(n, -1), w_out).reshape(b, t, d)


def kernel(x, ffn1_norm, ffn1_w_in, ffn1_w_out, mix_norm, ffn2_norm, ffn2_w_in, ffn2_w_out, ab_w_in, conv_dw_w, conv_dw_b, conv_ln_g, conv_ln_b, nsa_pe_k, nsa_w1_k, nsa_w2_k, nsa_pe_v, nsa_w1_v, nsa_w2_v, ab_w_out, cd_w_in, sc_conv_w, cd_w_out, final_norm):
    b, t, d = x.shape
    depth = ffn1_norm.shape[0]
    n = b * t
    for layer in range(depth):
        x = _ffn(x.reshape(n, d), ffn1_norm[layer], ffn1_w_in[layer], ffn1_w_out[layer]).reshape(b, t, d)
        if layer % 2 == 0:
            e = layer // 2
            x = _mixer_conv_nsa(x, mix_norm[layer], ab_w_in[e], conv_dw_w[e], conv_dw_b[e], conv_ln_g[e],
                                conv_ln_b[e], nsa_pe_k[e], nsa_w1_k[e], nsa_w2_k[e],
                                nsa_pe_v[e], nsa_w1_v[e], nsa_w2_v[e], ab_w_out[e])
        else:
            o = layer // 2
            x = _mixer_shortconv_sb(x, mix_norm[layer], cd_w_in[o], sc_conv_w[o], cd_w_out[o])
        last = layer == depth - 1
        x = _ffn(x.reshape(n, d), ffn2_norm[layer], ffn2_w_in[layer], ffn2_w_out[layer],
                 final_g=final_norm if last else None).reshape(b, t, d)
    return x
```

```python
import functools

import numpy as np
import jax
import jax.numpy as jnp
from jax import lax
from jax.experimental import pallas as pl
from jax.experimental.pallas import tpu as pltpu

F32 = jnp.float32
BF16 = jnp.bfloat16

D_MODEL = 1024
HEAD_DIM = 64
GROUP_WIDTH = D_MODEL // 2
CONV_WIDTH = 31
NSA_HEADS = GROUP_WIDTH // HEAD_DIM
NSA_KV_GROUPS = 2
NSA_REP = NSA_HEADS // NSA_KV_GROUPS
NSA_CMP_BLOCK = 32
NSA_CMP_STRIDE = 16
NSA_CMP_HIDDEN = 128
NSA_SEL_BLOCK = 64
NSA_TOP_N = 16
NSA_WINDOW = 512
NSA_N_BRANCH = 3
SC_WIDTH = 3
SB_HEADS = GROUP_WIDTH // HEAD_DIM
D_FF = 2816
RMS_EPS = 1e-6
LN_EPS = 1e-5
NEG_INF = -1e30
SEL_FORCE = 1e4
QK_SCALE = HEAD_DIM ** -0.5
LOG2_E = float(np.log2(np.e))

LANES = 128
SUBLANES = 8
VMEM_LIMIT = 48 * 1024 * 1024

FFN_TM = 1024
FFN_TF = 256
PROJ_TM = 512
OUT_TM = 1024
CONV_TT = 256
CONV_HALO = 32
CONV_CHUNK = 64
SC_TT = 512
SC_HALO = 8
NSA_TQ = 128
NSA_TK = 512
SB_T = 256
SB_PAIRS = 2
SB_DEAD_LOG2 = -160.0


def _params(*sem):
    return pltpu.CompilerParams(dimension_semantics=sem, vmem_limit_bytes=VMEM_LIMIT)


def _dot(a, b):
    return jnp.dot(a, b, preferred_element_type=F32)


def _dot_nt(a, b):
    return lax.dot_general(a, b, (((1,), (1,)), ((), ())), preferred_element_type=F32)


def _sigmoid(x):
    return 1.0 / (1.0 + jnp.exp(-x))


def _rmsnorm_rows(x, g):
    return x * lax.rsqrt(jnp.mean(x * x, axis=-1, keepdims=True) + RMS_EPS) * g


def _ffn_kernel(x_ref, g_ref, wg_ref, wu_ref, wo_ref, *rest, n_ff, final_norm):
    if final_norm:
        fg_ref, o_ref, xn_ref, acc_ref = rest
    else:
        o_ref, xn_ref, acc_ref = rest
    j = pl.program_id(1)

    @pl.when(j == 0)
    def _():
        xn_ref[...] = _rmsnorm_rows(x_ref[...], g_ref[...]).astype(BF16)
        acc_ref[...] = jnp.zeros_like(acc_ref)

    xn = xn_ref[...]
    gate = _dot(xn, wg_ref[...])
    up = _dot(xn, wu_ref[...])
    h = gate * _sigmoid(gate) * up
    acc_ref[...] += _dot(h.astype(BF16), wo_ref[...])

    @pl.when(j == n_ff - 1)
    def _():
        y = x_ref[...] + 0.5 * acc_ref[...]
        if final_norm:
            y = _rmsnorm_rows(y, fg_ref[...])
        o_ref[...] = y


def _ffn(x, g, w_in, w_out, final_g=None):
    n, d = x.shape
    n_ff = D_FF // FFN_TF
    tm = min(FFN_TM, n)
    w_in = w_in.astype(BF16)
    w_out = w_out.astype(BF16)
    in_specs = [
        pl.BlockSpec((tm, d), lambda i, j: (i, 0)),
        pl.BlockSpec((1, d), lambda i, j: (0, 0)),
        pl.BlockSpec((d, FFN_TF), lambda i, j: (0, j)),
        pl.BlockSpec((d, FFN_TF), lambda i, j: (0, j + n_ff)),
        pl.BlockSpec((FFN_TF, d), lambda i, j: (j, 0)),
    ]
    args = [x, g.reshape(1, d), w_in, w_in, w_out]
    if final_g is not None:
        in_specs.append(pl.BlockSpec((1, d), lambda i, j: (0, 0)))
        args.append(final_g.reshape(1, d))
    return pl.pallas_call(
        functools.partial(_ffn_kernel, n_ff=n_ff, final_norm=final_g is not None),
        out_shape=jax.ShapeDtypeStruct((n, d), F32),
        grid=(n // tm, n_ff),
        in_specs=in_specs,
        out_specs=pl.BlockSpec((tm, d), lambda i, j: (i, 0)),
        scratch_shapes=[pltpu.VMEM((tm, d), BF16), pltpu.VMEM((tm, d), F32)],
        compiler_params=_params("parallel", "arbitrary"),
        name="ffn",
    )(*args)


AB_A = 0
AB_Q = 1024
AB_KC = 2048
AB_VC = 2176
AB_KS = 2304
AB_VS = 2560
AB_KW = 2816
AB_VW = 3072
AB_G = 3328
AB_COLS = 3584

POS_SPLIT = 3
POS_TOK = HEAD_DIM
POS_CMP = POS_TOK + 2 * POS_SPLIT
MASK_BIG = 2.0 ** 100


def _pos_lanes(hi, lo, base):
    lane = lax.broadcasted_iota(jnp.int32, (1, LANES), 1)
    in_hi = (lane >= base) & (lane < base + POS_SPLIT)
    in_lo = (lane >= base + POS_SPLIT) & (lane < base + 2 * POS_SPLIT)
    return jnp.where(in_hi, hi, jnp.where(in_lo, lo, 0.0))


def _proj_ab_kernel(x_ref, g_ref, w_ref, qb_ref, a_ref, q_ref, kc_ref, vc_ref, ks_ref, vs_ref, kw_ref, vw_ref,
                    gt_ref, *, tm):
    xn = _rmsnorm_rows(x_ref[0], g_ref[...]).astype(BF16)

    def seg(lo, hi):
        return _dot(xn, w_ref[:, lo:hi])

    av = seg(AB_A, AB_A + 2 * GROUP_WIDTH)
    a_ref[0] = av[:, :GROUP_WIDTH] * _sigmoid(av[:, GROUP_WIDTH:])
    q_ref[0] = (seg(AB_Q, AB_KC) + qb_ref[...]).astype(BF16)
    kc_ref[0] = seg(AB_KC, AB_VC)
    vc_ref[0] = seg(AB_VC, AB_KS)
    vs_ref[0] = seg(AB_VS, AB_KW).astype(BF16)
    vw_ref[0] = seg(AB_VW, AB_G).astype(BF16)
    gt_ref[0] = _sigmoid(seg(AB_G, AB_COLS))

    tok = pl.program_id(1) * tm + lax.broadcasted_iota(jnp.int32, (tm, 1), 0)
    blk = lax.shift_right_logical(tok, int(np.log2(NSA_SEL_BLOCK)))
    pos = _pos_lanes(blk.astype(F32), (tok & (NSA_SEL_BLOCK - 1)).astype(F32), POS_TOK)
    onehot = jnp.where(lax.broadcasted_iota(jnp.int32, (1, LANES), 1) == blk, 1.0, 0.0).astype(BF16)
    ks = seg(AB_KS, AB_VS)
    kw = seg(AB_KW, AB_VW)
    for g in range(NSA_KV_GROUPS):
        ks_ref[0, :, 2 * g * LANES:(2 * g + 1) * LANES] = (ks[:, g * LANES:(g + 1) * LANES] + pos).astype(BF16)
        ks_ref[0, :, (2 * g + 1) * LANES:(2 * g + 2) * LANES] = onehot
        kw_ref[0, :, g * LANES:(g + 1) * LANES] = (kw[:, g * LANES:(g + 1) * LANES] + pos).astype(BF16)


def _arrange_ab_weight(w):
    d = w.shape[0]
    kvw = NSA_KV_GROUPS * HEAD_DIM
    o = 2 * GROUP_WIDTH
    a = w[:, :o]
    q = w[:, o:o + GROUP_WIDTH].reshape(d, NSA_HEADS, HEAD_DIM) * (QK_SCALE * LOG2_E)
    o += GROUP_WIDTH
    kc, vc, ks, vs, kw, vw = [w[:, o + i * kvw:o + (i + 1) * kvw] for i in range(6)]
    o += 6 * kvw
    g = w[:, o:].reshape(d, NSA_KV_GROUPS, NSA_REP * NSA_N_BRANCH)

    zeros_h = jnp.zeros((d, NSA_HEADS, HEAD_DIM), w.dtype)
    q_pad = jnp.concatenate([q, zeros_h], axis=-1).reshape(d, NSA_HEADS * LANES)

    def k_pad(k):
        k = k.reshape(d, NSA_KV_GROUPS, HEAD_DIM)
        return jnp.concatenate([k, jnp.zeros_like(k)], axis=-1).reshape(d, NSA_KV_GROUPS * LANES)

    def v_dup(v):
        v = v.reshape(d, NSA_KV_GROUPS, HEAD_DIM)
        return jnp.concatenate([v, v], axis=-1).reshape(d, NSA_KV_GROUPS * LANES)

    g_pad = jnp.pad(g, ((0, 0), (0, 0), (0, LANES - g.shape[-1]))).reshape(d, NSA_KV_GROUPS * LANES)
    out = jnp.concatenate([a, q_pad, kc, vc, k_pad(ks), v_dup(vs), k_pad(kw), v_dup(vw), g_pad], axis=-1)
    assert out.shape[1] == AB_COLS
    return out.astype(BF16)


def _bf16_terms(x, n):
    terms, rest = [], np.asarray(x, np.float64)
    for _ in range(n):
        term = rest.astype(BF16).astype(np.float64)
        terms.append(term)
        rest = rest - term
    return terms


def _alibi_query_lanes():
    row = np.zeros((NSA_HEADS, LANES), np.float64)
    for h in range(NSA_HEADS):
        slope = 2.0 ** (-8.0 * (h + 1) / NSA_HEADS)
        for i, term in enumerate(_bf16_terms(slope * np.log2(np.e), POS_SPLIT)):
            row[h, POS_TOK + i] = NSA_SEL_BLOCK * term
            row[h, POS_TOK + POS_SPLIT + i] = term
            row[h, POS_CMP + i] = 2 * NSA_CMP_STRIDE * term
            row[h, POS_CMP + POS_SPLIT + i] = NSA_CMP_STRIDE * term
    return jnp.asarray(row.reshape(1, NSA_HEADS * LANES), F32)


def _proj_ab(x, g, w):
    b, t, d = x.shape
    tm = min(PROJ_TM, t)
    wa = _arrange_ab_weight(w)

    def tok(width):
        return pl.BlockSpec((1, tm, width), lambda bi, i: (bi, i, 0))

    const = lambda width: pl.BlockSpec((1, width), lambda bi, i: (0, 0))
    kv = NSA_KV_GROUPS * LANES
    widths = [GROUP_WIDTH, NSA_HEADS * LANES, LANES, LANES, 2 * kv, kv, kv, kv, kv]
    dtypes = [F32, BF16, F32, F32, BF16, BF16, BF16, BF16, F32]
    return pl.pallas_call(
        functools.partial(_proj_ab_kernel, tm=tm),
        out_shape=[jax.ShapeDtypeStruct((b, t, wd), dt) for wd, dt in zip(widths, dtypes)],
        grid=(b, t // tm),
        in_specs=[tok(d), const(d), pl.BlockSpec((d, AB_COLS), lambda bi, i: (0, 0)), const(NSA_HEADS * LANES)],
        out_specs=[tok(wd) for wd in widths],
        compiler_params=_params("parallel", "parallel"),
        name="proj_ab",
    )(x, g.reshape(1, d), wa, _alibi_query_lanes())


CD_B = 0
CD_C = 512
CD_U = 1024
CD_Q = 1536
CD_K = 2560
CD_V = 3072
CD_COLS = 3584


def _proj_cd_kernel(x_ref, g_ref, w_ref, bg_ref, p_ref, q_ref, k_ref, v_ref):
    xn = _rmsnorm_rows(x_ref[0], g_ref[...]).astype(BF16)

    def seg(lo, hi):
        return _dot(xn, w_ref[:, lo:hi])

    bg_ref[0] = seg(CD_B, CD_C)
    p_ref[0] = seg(CD_C, CD_U) * seg(CD_U, CD_Q)
    q_ref[0] = seg(CD_Q, CD_K).astype(BF16)
    k_ref[0] = seg(CD_K, CD_V).astype(BF16)
    v_ref[0] = seg(CD_V, CD_COLS).astype(BF16)


def _arrange_cd_weight(w):
    d = w.shape[0]
    gw = GROUP_WIDTH
    q = w[:, 3 * gw:4 * gw].reshape(d, SB_HEADS // 2, 2, HEAD_DIM) * (QK_SCALE * LOG2_E)
    z = jnp.zeros((d, SB_HEADS // 2, HEAD_DIM), w.dtype)
    q_even = jnp.concatenate([q[:, :, 0], z], axis=-1)
    q_odd = jnp.concatenate([z, q[:, :, 1]], axis=-1)
    q_pad = jnp.stack([q_even, q_odd], axis=2).reshape(d, SB_HEADS * LANES)
    out = jnp.concatenate([w[:, :3 * gw], q_pad, w[:, 4 * gw:]], axis=-1)
    assert out.shape[1] == CD_COLS
    return out.astype(BF16)


def _proj_cd(x, g, w):
    b, t, d = x.shape
    tm = min(PROJ_TM, t)
    wa = _arrange_cd_weight(w)

    def tok(width):
        return pl.BlockSpec((1, tm, width), lambda bi, i: (bi, i, 0))

    widths = [GROUP_WIDTH, GROUP_WIDTH, SB_HEADS * LANES, GROUP_WIDTH, GROUP_WIDTH]
    dtypes = [F32, F32, BF16, BF16, BF16]
    return pl.pallas_call(
        _proj_cd_kernel,
        out_shape=[jax.ShapeDtypeStruct((b, t, wd), dt) for wd, dt in zip(widths, dtypes)],
        grid=(b, t // tm),
        in_specs=[tok(d), pl.BlockSpec((1, d), lambda bi, i: (0, 0)),
                  pl.BlockSpec((d, CD_COLS), lambda bi, i: (0, 0))],
        out_specs=[tok(wd) for wd in widths],
        compiler_params=_params("parallel", "parallel"),
        name="proj_cd",
    )(x, g.reshape(1, d), wa)


def _conv_kernel(cur_ref, halo_ref, w_ref, b_ref, lg_ref, lb_ref, o_ref, ext_ref, shift_ref, *, tt):
    i = pl.program_id(1)
    halo = halo_ref[0]
    ext_ref[0:CONV_HALO, :] = jnp.where(i == 0, jnp.zeros_like(halo), halo)
    ext_ref[CONV_HALO:, :] = cur_ref[0]
    off = CONV_HALO - (CONV_WIDTH - 1)
    last = off + CONV_WIDTH - 1
    steps = {phase: [m for m in range((last - phase) // SUBLANES + 1) if off <= SUBLANES * m + phase]
             for phase in range(SUBLANES)}
    for phase in range(SUBLANES):
        rows = SUBLANES * steps[phase][-1] + tt
        shift_ref[phase, 0:rows, :] = ext_ref[phase:phase + rows, :]
    for c in range(tt // CONV_CHUNK):
        r0 = c * CONV_CHUNK
        acc = jnp.zeros((CONV_CHUNK, GROUP_WIDTH), F32)
        for phase in range(SUBLANES):
            for m in steps[phase]:
                k = SUBLANES * m + phase - off
                lo = r0 + SUBLANES * m
                acc = acc + w_ref[k:k + 1, :] * shift_ref[phase, lo:lo + CONV_CHUNK, :]
        y = acc + b_ref[...]
        mu = jnp.mean(y, axis=-1, keepdims=True)
        yc = y - mu
        var = jnp.mean(yc * yc, axis=-1, keepdims=True)
        yn = yc * lax.rsqrt(var + LN_EPS) * lg_ref[...] + lb_ref[...]
        o_ref[0, r0:r0 + CONV_CHUNK, :] = (yn * _sigmoid(yn)).astype(o_ref.dtype)


def _conformer_conv(a, w, bias, ln_g, ln_b):
    b, t, c = a.shape
    tt = min(CONV_TT, t)
    hb = tt // CONV_HALO
    row = lambda v: v.reshape(1, c)
    const = lambda shape: pl.BlockSpec(shape, lambda bi, i: (0, 0))
    return pl.pallas_call(
        functools.partial(_conv_kernel, tt=tt),
        out_shape=jax.ShapeDtypeStruct((b, t, c), BF16),
        grid=(b, t // tt),
        in_specs=[pl.BlockSpec((1, tt, c), lambda bi, i: (bi, i, 0)),
                  pl.BlockSpec((1, CONV_HALO, c), lambda bi, i: (bi, jnp.maximum(i * hb - 1, 0), 0)),
                  const((CONV_WIDTH, c)), const((1, c)), const((1, c)), const((1, c))],
        out_specs=pl.BlockSpec((1, tt, c), lambda bi, i: (bi, i, 0)),
        scratch_shapes=[pltpu.VMEM((tt + CONV_HALO, c), F32), pltpu.VMEM((SUBLANES, tt + CONV_HALO, c), F32)],
        compiler_params=_params("parallel", "parallel"),
        name="conformer_conv",
    )(a, a, w, row(bias), row(ln_g), row(ln_b))


def _sc_kernel(p_ref, halo_ref, bg_ref, w_ref, o_ref, ext_ref, *, tt):
    i = pl.program_id(1)
    halo = halo_ref[0]
    ext_ref[0:SC_HALO, :] = jnp.where(i == 0, jnp.zeros_like(halo), halo)
    ext_ref[SC_HALO:, :] = p_ref[0]
    off = SC_HALO - (SC_WIDTH - 1)
    acc = jnp.zeros((tt, GROUP_WIDTH), F32)
    for k in range(SC_WIDTH):
        acc = acc + w_ref[k:k + 1, :] * ext_ref[off + k:off + k + tt, :]
    o_ref[0] = (bg_ref[0] * acc).astype(o_ref.dtype)


def _short_conv(p, bg, w):
    b, t, c = p.shape
    tt = min(SC_TT, t)
    hb = tt // SC_HALO
    return pl.pallas_call(
        functools.partial(_sc_kernel, tt=tt),
        out_shape=jax.ShapeDtypeStruct((b, t, c), BF16),
        grid=(b, t // tt),
        in_specs=[pl.BlockSpec((1, tt, c), lambda bi, i: (bi, i, 0)),
                  pl.BlockSpec((1, SC_HALO, c), lambda bi, i: (bi, jnp.maximum(i * hb - 1, 0), 0)),
                  pl.BlockSpec((1, tt, c), lambda bi, i: (bi, i, 0)),
                  pl.BlockSpec((SC_WIDTH, c), lambda bi, i: (0, 0))],
        out_specs=pl.BlockSpec((1, tt, c), lambda bi, i: (bi, i, 0)),
        scratch_shapes=[pltpu.VMEM((tt + SC_HALO, c), F32)],
        compiler_params=_params("parallel", "parallel"),
        name="short_conv",
    )(p, p, bg, w)


def _outproj_kernel(x_ref, a_ref, o_ref, wa_ref, wo_ref, y_ref):
    y_ref[...] = x_ref[...] + _dot(a_ref[...], wa_ref[...]) + _dot(o_ref[...], wo_ref[...])


def _outproj(x, a, o, w):
    n, d = x.shape
    gw = a.shape[1]
    tm = min(OUT_TM, n)
    w = w.astype(BF16)
    tok = lambda width: pl.BlockSpec((tm, width), lambda i: (i, 0))
    return pl.pallas_call(
        _outproj_kernel,
        out_shape=jax.ShapeDtypeStruct((n, d), F32),
        grid=(n // tm,),
        in_specs=[tok(d), tok(gw), tok(gw),
                  pl.BlockSpec((gw, d), lambda i: (0, 0)), pl.BlockSpec((gw, d), lambda i: (1, 0))],
        out_specs=tok(d),
        compiler_params=_params("parallel"),
        name="outproj",
    )(x, a, o, w, w)


def _gelu_tanh(x):
    return 0.5 * x * (1.0 + jnp.tanh(np.sqrt(2.0 / np.pi).astype(np.float32) * (x + 0.044715 * (x * x * x))))


def _compress_kernel(x_ref, pe_ref, w1_ref, w2_ref, o_ref, *, nchunk, keys):
    half = NSA_CMP_STRIDE * HEAD_DIM
    x = x_ref[0, 0]
    ha = _dot((x + pe_ref[:, :half]).astype(BF16), w1_ref[:half, :])
    hb = _dot((x + pe_ref[:, half:]).astype(BF16), w1_ref[half:, :])
    h = ha + pltpu.roll(hb, nchunk - 1, 0)
    y = _dot(_gelu_tanh(h).astype(BF16), w2_ref[...])
    row = lax.broadcasted_iota(jnp.int32, (nchunk, 1), 0)
    y = jnp.where(row < nchunk - 1, y, 0.0)
    if keys:
        y = y + _pos_lanes(lax.shift_right_logical(row, 1).astype(F32), (row & 1).astype(F32), POS_CMP)
    o_ref[0, 0] = y


def _compress(kx, pe, w1, w2, dup):
    b, t, _ = kx.shape
    g = NSA_KV_GROUPS
    nchunk = t // NSA_CMP_STRIDE
    feat = NSA_CMP_STRIDE * HEAD_DIM
    x = kx.reshape(b, nchunk, NSA_CMP_STRIDE, g, HEAD_DIM).transpose(0, 3, 1, 2, 4).reshape(b, g, nchunk, feat)
    w2p = jnp.concatenate([w2, w2 if dup else jnp.zeros_like(w2)], axis=-1).astype(BF16)
    y = pl.pallas_call(
        functools.partial(_compress_kernel, nchunk=nchunk, keys=not dup),
        out_shape=jax.ShapeDtypeStruct((b, g, nchunk, LANES), F32),
        grid=(b, g),
        in_specs=[pl.BlockSpec((1, 1, nchunk, feat), lambda bi, gi: (bi, gi, 0, 0)),
                  pl.BlockSpec((1, 2 * feat), lambda bi, gi: (0, 0)),
                  pl.BlockSpec((2 * feat, NSA_CMP_HIDDEN), lambda bi, gi: (0, 0)),
                  pl.BlockSpec((NSA_CMP_HIDDEN, LANES), lambda bi, gi: (0, 0))],
        out_specs=pl.BlockSpec((1, 1, nchunk, LANES), lambda bi, gi: (bi, gi, 0, 0)),
        compiler_params=_params("parallel", "parallel"),
        name="nsa_compress",
    )(x, pe.reshape(1, 2 * feat), w1.reshape(2 * feat, NSA_CMP_HIDDEN).astype(BF16), w2p)
    ratio = NSA_SEL_BLOCK // NSA_CMP_STRIDE
    return y.reshape(b, g, nchunk // ratio, ratio, LANES).transpose(0, 1, 3, 2, 4).astype(BF16)


def _softmax2_rows(s, mask):
    m = jnp.max(s, axis=-1, keepdims=True)
    p = jnp.where(mask, jnp.exp2(s - m), 0.0)
    l = jnp.sum(p, axis=-1, keepdims=True)
    return p * jnp.where(l > 0.0, 1.0 / l, 0.0)


def _nsa_kernel(q_ref, kc_ref, vc_ref, ks_ref, vs_ref, kw_ref, vw_ref, gt_ref, o_ref, *, t_len, ns):
    tq, tk, rep = NSA_TQ, NSA_TK, NSA_REP
    rows = rep * tq
    i = pl.program_id(2)
    q0 = i * tq

    q2 = q_ref[0]
    qg = jnp.concatenate([q2[:, r * LANES:(r + 1) * LANES] for r in range(rep)], axis=0)
    rid = lax.broadcasted_iota(jnp.int32, (rows, 1), 0)
    t_i = q0 + (rid & (tq - 1))

    lane_i = lax.broadcasted_iota(jnp.int32, (1, ns), 1)
    ratio = NSA_SEL_BLOCK // NSA_CMP_STRIDE
    n_cmp = t_len // NSA_CMP_STRIDE - 1
    s_list, mask_list = [], []
    for r in range(ratio):
        c_i = lane_i * ratio + r
        mask = ((c_i * NSA_CMP_STRIDE + (NSA_CMP_BLOCK - 1)) <= t_i) & (c_i < n_cmp)
        s_list.append(jnp.where(mask, _dot_nt(qg, kc_ref[0, 0, r]), NEG_INF))
        mask_list.append(mask)
    m = functools.reduce(jnp.maximum, [jnp.max(s, axis=-1, keepdims=True) for s in s_list])
    p_list = [jnp.where(mk, jnp.exp2(s - m), 0.0) for s, mk in zip(s_list, mask_list)]
    l = functools.reduce(lambda a, b: a + b, [jnp.sum(p, axis=-1, keepdims=True) for p in p_list])
    inv = jnp.where(l > 0.0, 1.0 / l, 0.0)
    p_list = [p * inv for p in p_list]
    o_c = functools.reduce(lambda a, b: a + b,
                           [_dot(p.astype(BF16), vc_ref[0, 0, r]) for r, p in enumerate(p_list)])

    def head_sum(p):
        return functools.reduce(lambda a, b: a + b, [p[r * tq:(r + 1) * tq] for r in range(rep)])

    ps = [head_sum(p) for p in p_list]
    lane_q = lax.broadcasted_iota(jnp.int32, (tq, ns), 1)
    prev_last = jnp.where(lane_q == 0, 0.0, pltpu.roll(ps[ratio - 1], 1, 1))
    imp = prev_last + ps[0] + ps[1] + ps[2] + ps[3]
    tq_i = q0 + lax.broadcasted_iota(jnp.int32, (tq, 1), 0)
    cur = lax.shift_right_logical(tq_i, int(np.log2(NSA_SEL_BLOCK)))
    visible = lane_q * NSA_SEL_BLOCK <= tq_i
    forced = (lane_q == 0) | (lane_q == cur) | (lane_q == cur - 1)
    score = jnp.where(visible, jnp.where(forced, SEL_FORCE, imp), -1.0)

    blk_f = lax.broadcasted_iota(jnp.int32, (ns, tq), 0).astype(F32)
    work = score.T
    sel_t = jnp.zeros((ns, tq), F32)
    for _ in range(min(NSA_TOP_N, ns)):
        top = jnp.max(work, axis=0, keepdims=True)
        idx = jnp.min(jnp.where(work == top, blk_f, float(ns)), axis=0, keepdims=True)
        hit = blk_f == idx
        sel_t = jnp.where(hit, 1.0, sel_t)
        work = jnp.where(hit, -2.0, work)
    sel = jnp.where(score >= 0.0, sel_t.T, 0.0)

    unsel = ((sel - 1.0) * MASK_BIG).astype(BF16)
    if ns < LANES:
        unsel = jnp.concatenate([unsel, jnp.zeros((tq, LANES - ns), BF16)], axis=1)
    q_aug = jnp.concatenate([qg, jnp.concatenate([unsel] * rep, axis=0)], axis=1)

    def sel_tile(k0, causal):
        s = _dot_nt(q_aug, ks_ref[0, pl.ds(k0, tk), :])
        if causal:
            tok = k0 + lax.broadcasted_iota(jnp.int32, (1, tk), 1)
            s = jnp.where(tok <= t_i, s, -MASK_BIG)
        return s

    blocks_per_tile = tk // NSA_SEL_BLOCK
    tile_of_blk = lax.shift_right_logical(lax.broadcasted_iota(jnp.int32, (ns, LANES), 0),
                                          int(np.log2(blocks_per_tile)))
    lane_t = lax.broadcasted_iota(jnp.int32, (ns, LANES), 1)
    per_tile = _dot(sel.astype(BF16), jnp.where(tile_of_blk == lane_t, 1.0, 0.0).astype(BF16))
    tile_any = jnp.max(per_tile, axis=0, keepdims=True) > 0.0
    pow2 = lax.shift_left(jnp.ones((1, LANES), jnp.int32), lane_t[:1] & 15).astype(F32)
    tile_bits = jnp.sum(jnp.where(tile_any, pow2, 0.0), axis=-1, keepdims=True).astype(jnp.int32)[0, 0]

    n_past = lax.shift_right_logical(q0, int(np.log2(tk)))
    kd = pl.multiple_of(n_past * tk, tk)
    s = sel_tile(kd, True)
    m_s = jnp.max(s, axis=-1, keepdims=True)
    p = jnp.exp2(s - m_s)
    l_s = jnp.sum(p, axis=-1, keepdims=True)
    acc_s = _dot(p.astype(BF16), vs_ref[0, pl.ds(kd, tk), :])

    def sel_update(kt, carry):
        m_run, l_run, acc = carry
        k0 = pl.multiple_of(kt * tk, tk)
        s = sel_tile(k0, False)
        m_new = jnp.maximum(m_run, jnp.max(s, axis=-1, keepdims=True))
        p = jnp.exp2(s - m_new)
        alpha = jnp.exp2(m_run - m_new)
        l_new = alpha * l_run + jnp.sum(p, axis=-1, keepdims=True)
        acc = alpha * acc + _dot(p.astype(BF16), vs_ref[0, pl.ds(k0, tk), :])
        return m_new, l_new, acc

    def next_active(state):
        return lax.while_loop(lambda st: (st[0] & 1) == 0,
                              lambda st: (lax.shift_right_logical(st[0], 1), st[1] + 1), state)

    def sel_body(state):
        pending, kt = next_active(state[:2])
        return (lax.shift_right_logical(pending, 1), kt + 1) + sel_update(kt, state[2:])

    past_bits = tile_bits & (lax.shift_left(jnp.int32(1), n_past) - 1)
    _, _, _, l_s, acc_s = lax.while_loop(lambda st: st[0] != 0, sel_body,
                                         (past_bits, jnp.int32(0), m_s, l_s, acc_s))
    o_s = acc_s / l_s

    wlen = NSA_WINDOW + tq
    w0 = pl.multiple_of(jnp.maximum(q0 - NSA_WINDOW, 0), tq)
    dist = t_i - (w0 + lax.broadcasted_iota(jnp.int32, (1, wlen), 1))
    mask_w = (dist >= 0) & (dist < NSA_WINDOW)
    s_w = _dot_nt(qg, kw_ref[0, pl.ds(w0, wlen), :])
    p_w = _softmax2_rows(jnp.where(mask_w, s_w, NEG_INF), mask_w)
    o_w = _dot(p_w.astype(BF16), vw_ref[0, pl.ds(w0, wlen), :])

    gt = gt_ref[0]

    def gate(branch):
        return jnp.concatenate(
            [gt[:, r * NSA_N_BRANCH + branch:r * NSA_N_BRANCH + branch + 1] for r in range(rep)], axis=0)

    o = gate(0) * o_c + gate(1) * o_s + gate(2) * o_w
    low_half = lax.broadcasted_iota(jnp.int32, (tq, LANES), 1) < HEAD_DIM
    for c in range(rep // 2):
        even = o[(2 * c) * tq:(2 * c + 1) * tq]
        odd = o[(2 * c + 1) * tq:(2 * c + 2) * tq]
        o_ref[0, :, c * LANES:(c + 1) * LANES] = jnp.where(low_half, even, odd).astype(o_ref.dtype)


def _nsa(q, kc, vc, ks, vs, kw, vw, gates):
    b, t, _ = q.shape
    g = NSA_KV_GROUPS
    ns = t // NSA_SEL_BLOCK
    assert ns <= LANES and t >= NSA_WINDOW + NSA_TQ and t % NSA_TK == 0 and NSA_TK % NSA_TQ == 0
    assert t // NSA_TK <= 16
    ratio = NSA_SEL_BLOCK // NSA_CMP_STRIDE
    qw = NSA_REP * LANES
    cmp_spec = pl.BlockSpec((1, 1, ratio, ns, LANES), lambda bi, gi, i: (bi, gi, 0, 0, 0))
    kv_spec = pl.BlockSpec((1, t, LANES), lambda bi, gi, i: (bi, 0, gi))
    ks_spec = pl.BlockSpec((1, t, 2 * LANES), lambda bi, gi, i: (bi, 0, gi))
    return pl.pallas_call(
        functools.partial(_nsa_kernel, t_len=t, ns=ns),
        out_shape=jax.ShapeDtypeStruct((b, t, GROUP_WIDTH), BF16),
        grid=(b, g, t // NSA_TQ),
        in_specs=[pl.BlockSpec((1, NSA_TQ, qw), lambda bi, gi, i: (bi, i, gi)),
                  cmp_spec, cmp_spec, ks_spec, kv_spec, kv_spec, kv_spec,
                  pl.BlockSpec((1, NSA_TQ, LANES), lambda bi, gi, i: (bi, i, gi))],
        out_specs=pl.BlockSpec((1, NSA_TQ, NSA_REP * HEAD_DIM), lambda bi, gi, i: (bi, i, gi)),
        compiler_params=_params("parallel", "parallel", "arbitrary"),
        name="nsa_attention",
    )(q, kc, vc, ks, vs, kw, vw, gates)


def _sb_tiles(qs, k, v, u2, carries, mask):
    stage1 = []
    for h, qh in enumerate(qs):
        p = h // 2
        z = _dot_nt(qh, k[:, p * LANES:(p + 1) * LANES])
        nk = jnp.maximum(z, 0.0) + jnp.log2(1.0 + jnp.exp2(-jnp.abs(z)))
        if mask is not None:
            nk = jnp.where(mask, nk, 0.0)
        hi = nk.astype(BF16)
        lo = (nk - hi.astype(F32)).astype(BF16)
        stage1.append((z, jnp.concatenate([hi, lo], axis=1)))
    laters = [_dot(hilo, u2) for (_, hilo) in stage1]
    out = []
    for h, ((z, _), later, (acc, c)) in enumerate(zip(stage1, laters, carries)):
        p = h // 2
        a = jnp.exp2(z + later)
        if mask is not None:
            a = jnp.where(mask, a, 0.0)
        acc = acc + jnp.exp2(c) * _dot(a.astype(BF16), v[:, p * LANES:(p + 1) * LANES])
        out.append((acc, c + later[:, 0:1]))
    return tuple(out)


def _sb_kernel(q_ref, k_ref, v_ref, u_ref, o_ref):
    tile = SB_T
    i = pl.program_id(2)
    q2 = q_ref[0]
    u = u_ref[...]
    qs = [q2[:, h * LANES:(h + 1) * LANES] for h in range(2 * SB_PAIRS)]
    q0 = pl.multiple_of(i * tile, tile)
    zero = (jnp.zeros((tile, LANES), F32), jnp.zeros((tile, 1), F32))
    mask = lax.broadcasted_iota(jnp.int32, (tile, tile), 1) < lax.broadcasted_iota(jnp.int32, (tile, tile), 0)
    carry = _sb_tiles(qs, k_ref[0, pl.ds(q0, tile), :], v_ref[0, pl.ds(q0, tile), :], u,
                      tuple(zero for _ in qs), mask)

    def body(jj, carry):
        k0 = pl.multiple_of((i - 1 - jj) * tile, tile)
        return _sb_tiles(qs, k_ref[0, pl.ds(k0, tile), :], v_ref[0, pl.ds(k0, tile), :], u, carry, None)

    def alive(carry):
        c_max = functools.reduce(jnp.maximum, [c for _, c in carry])
        return jnp.max(c_max) > SB_DEAD_LOG2

    def step(state):
        jj, _, carry = state
        carry = body(jj, carry)
        return jj + 1, alive(carry), carry

    _, _, carry = lax.while_loop(lambda st: (st[0] < i) & st[1], step, (jnp.int32(0), alive(carry), carry))
    low_half = lax.broadcasted_iota(jnp.int32, (tile, LANES), 1) < HEAD_DIM
    for p in range(SB_PAIRS):
        o_ref[0, :, p * LANES:(p + 1) * LANES] = jnp.where(
            low_half, carry[2 * p][0], carry[2 * p + 1][0]).astype(o_ref.dtype)


def _stick_breaking(q, k, v):
    b, t, _ = q.shape
    tile = SB_T
    assert t % tile == 0
    u = -(np.arange(tile)[:, None] >= np.arange(tile)[None, :]).astype(np.float32)
    u2 = np.concatenate([u, u], axis=0)
    kv_spec = pl.BlockSpec((1, t, SB_PAIRS * LANES), lambda bi, hp, i: (bi, 0, hp))
    return pl.pallas_call(
        _sb_kernel,
        out_shape=jax.ShapeDtypeStruct((b, t, GROUP_WIDTH), BF16),
        grid=(b, SB_HEADS // (2 * SB_PAIRS), t // tile),
        in_specs=[pl.BlockSpec((1, tile, 2 * SB_PAIRS * LANES), lambda bi, hp, i: (bi, i, hp)),
                  kv_spec, kv_spec,
                  pl.BlockSpec((2 * tile, tile), lambda bi, hp, i: (0, 0))],
        out_specs=pl.BlockSpec((1, tile, SB_PAIRS * LANES), lambda bi, hp, i: (bi, i, hp)),
        compiler_params=_params("parallel", "parallel", "arbitrary"),
        name="stick_breaking",
    )(q, k, v, jnp.asarray(u2, BF16))


def _mixer_conv_nsa(x, norm_g, w_in, dw_w, dw_b, ln_g, ln_b, pe_k, w1_k, w2_k, pe_v, w1_v, w2_v, w_out):
    b, t, d = x.shape
    a, q, kc, vc, ks, vs, kw, vw, gates = _proj_ab(x, norm_g, w_in)
    a = _conformer_conv(a, dw_w, dw_b, ln_g, ln_b)
    k_cmp = _compress(kc, pe_k, w1_k, w2_k, dup=False)
    v_cmp = _compress(vc, pe_v, w1_v, w2_v, dup=True)
    o = _nsa(q, k_cmp, v_cmp, ks, vs, kw, vw, gates)
    n = b * t
    return _outproj(x.reshape(n, d), a.reshape(n, -1), o.reshape(n, -1), w_out).reshape(b, t, d)


def _mixer_shortconv_sb(x, norm_g, w_in, sc_w, w_out):
    b, t, d = x.shape
    bg, p, q, k, v = _proj_cd(x, norm_g, w_in)
    c = _short_conv(p, bg, sc_w)
    o = _stick_breaking(q, k, v)
    n = b * t
    return _outproj(x.reshape(n, d), c.reshape(n, -1), o.reshape(n, -1), w_out).reshape(b, t, d)


def kernel(x, ffn1_norm, ffn1_w_in, ffn1_w_out, mix_norm, ffn2_norm, ffn2_w_in, ffn2_w_out, ab_w_in, conv_dw_w, conv_dw_b, conv_ln_g, conv_ln_b, nsa_pe_k, nsa_w1_k, nsa_w2_k, nsa_pe_v, nsa_w1_v, nsa_w2_v, ab_w_out, cd_w_in, sc_conv_w, cd_w_out, final_norm):
    b, t, d = x.shape
    depth = ffn1_norm.shape[0]
    n = b * t
    for layer in range(depth):
        x = _ffn(x.reshape(n, d), ffn1_norm[layer], ffn1_w_in[layer], ffn1_w_out[layer]).reshape(b, t, d)
        if layer % 2 == 0:
            e = layer // 2
            x = _mixer_conv_nsa(x, mix_norm[layer], ab_w_in[e], conv_dw_w[e], conv_dw_b[e], conv_ln_g[e],
                                conv_ln_b[e], nsa_pe_k[e], nsa_w1_k[e], nsa_w2_k[e],
                                nsa_pe_v[e], nsa_w1_v[e], nsa_w2_v[e], ab_w_out[e])
        else:
            o = layer // 2
            x = _mixer_shortconv_sb(x, mix_norm[layer], cd_w_in[o], sc_conv_w[o], cd_w_out[o])
        last = layer == depth - 1
        x = _ffn(x.reshape(n, d), ffn2_norm[layer], ffn2_w_in[layer], ffn2_w_out[layer],
                 final_g=final_norm if last else None).reshape(b, t, d)
    return x
```

```python
import functools

import numpy as np
import jax
import jax.numpy as jnp
from jax import lax
from jax.experimental import pallas as pl
from jax.experimental.pallas import tpu as pltpu

F32 = jnp.float32
BF16 = jnp.bfloat16

D_MODEL = 1024
HEAD_DIM = 64
GROUP_WIDTH = D_MODEL // 2
CONV_WIDTH = 31
NSA_HEADS = GROUP_WIDTH // HEAD_DIM
NSA_KV_GROUPS = 2
NSA_REP = NSA_HEADS // NSA_KV_GROUPS
NSA_CMP_BLOCK = 32
NSA_CMP_STRIDE = 16
NSA_CMP_HIDDEN = 128
NSA_SEL_BLOCK = 64
NSA_TOP_N = 16
NSA_FORCED = 3
NSA_WINDOW = 512
NSA_N_BRANCH = 3
SC_WIDTH = 3
SB_HEADS = GROUP_WIDTH // HEAD_DIM
D_FF = 2816
RMS_EPS = 1e-6
LN_EPS = 1e-5
NEG_INF = -1e30
SEL_FORCE = 1e4
QK_SCALE = HEAD_DIM ** -0.5
LOG2_E = float(np.log2(np.e))

LANES = 128
SUBLANES = 8
VMEM_LIMIT = 48 * 1024 * 1024

FFN_TM = 1024
FFN_TF = 256
PROJ_TM = 512
OUT_TM = 1024
CONV_TT = 256
CONV_HALO = 32
CONV_CHUNK = 64
SC_TT = 512
SC_HALO = 8
NSA_TQ = 256
NSA_TK = 512
SB_T = 256
SB_PAIRS = 2
SB_DEAD_LOG2 = -160.0


def _params(*sem):
    return pltpu.CompilerParams(dimension_semantics=sem, vmem_limit_bytes=VMEM_LIMIT)


def _dot(a, b):
    return jnp.dot(a, b, preferred_element_type=F32)


def _dot_nt(a, b):
    return lax.dot_general(a, b, (((1,), (1,)), ((), ())), preferred_element_type=F32)


def _sigmoid(x):
    return 1.0 / (1.0 + jnp.exp(-x))


def _rmsnorm_rows(x, g):
    return x * lax.rsqrt(jnp.mean(x * x, axis=-1, keepdims=True) + RMS_EPS) * g


def _ffn_kernel(x_ref, g_ref, wg_ref, wu_ref, wo_ref, *rest, n_ff, final_norm):
    if final_norm:
        fg_ref, o_ref, xn_ref, acc_ref = rest
    else:
        o_ref, xn_ref, acc_ref = rest
    j = pl.program_id(1)

    @pl.when(j == 0)
    def _():
        xn_ref[...] = _rmsnorm_rows(x_ref[...], g_ref[...]).astype(BF16)
        acc_ref[...] = jnp.zeros_like(acc_ref)

    xn = xn_ref[...]
    gate = _dot(xn, wg_ref[...])
    up = _dot(xn, wu_ref[...])
    h = gate * _sigmoid(gate) * up
    acc_ref[...] += _dot(h.astype(BF16), wo_ref[...])

    @pl.when(j == n_ff - 1)
    def _():
        y = x_ref[...] + 0.5 * acc_ref[...]
        if final_norm:
            y = _rmsnorm_rows(y, fg_ref[...])
        o_ref[...] = y


def _ffn(x, g, w_in, w_out, final_g=None):
    n, d = x.shape
    n_ff = D_FF // FFN_TF
    tm = min(FFN_TM, n)
    w_in = w_in.astype(BF16).reshape(d, 2 * n_ff, FFN_TF).transpose(1, 0, 2)
    w_out = w_out.astype(BF16)
    in_specs = [
        pl.BlockSpec((tm, d), lambda i, j: (i, 0)),
        pl.BlockSpec((1, d), lambda i, j: (0, 0)),
        pl.BlockSpec((None, d, FFN_TF), lambda i, j: (j, 0, 0)),
        pl.BlockSpec((None, d, FFN_TF), lambda i, j: (j + n_ff, 0, 0)),
        pl.BlockSpec((FFN_TF, d), lambda i, j: (j, 0)),
    ]
    args = [x, g.reshape(1, d), w_in, w_in, w_out]
    if final_g is not None:
        in_specs.append(pl.BlockSpec((1, d), lambda i, j: (0, 0)))
        args.append(final_g.reshape(1, d))
    return pl.pallas_call(
        functools.partial(_ffn_kernel, n_ff=n_ff, final_norm=final_g is not None),
        out_shape=jax.ShapeDtypeStruct((n, d), F32),
        grid=(n // tm, n_ff),
        in_specs=in_specs,
        out_specs=pl.BlockSpec((tm, d), lambda i, j: (i, 0)),
        scratch_shapes=[pltpu.VMEM((tm, d), BF16), pltpu.VMEM((tm, d), F32)],
        compiler_params=_params("parallel", "arbitrary"),
        name="ffn",
    )(*args)


AB_A = 0
AB_Q = 1024
AB_KC = 2048
AB_VC = 2176
AB_KS = 2304
AB_VS = 2560
AB_KW = 2816
AB_VW = 3072
AB_G = 3328
AB_COLS = 3584

POS_SPLIT = 3
POS_TOK = HEAD_DIM
POS_CMP = POS_TOK + 2 * POS_SPLIT
MASK_BIG = 2.0 ** 100


def _pos_lanes(hi, lo, base):
    lane = lax.broadcasted_iota(jnp.int32, (1, LANES), 1)
    in_hi = (lane >= base) & (lane < base + POS_SPLIT)
    in_lo = (lane >= base + POS_SPLIT) & (lane < base + 2 * POS_SPLIT)
    return jnp.where(in_hi, hi, jnp.where(in_lo, lo, 0.0))


def _proj_ab_kernel(x_ref, g_ref, w_ref, qb_ref, a_ref, q_ref, kc_ref, vc_ref, ks_ref, vs_ref, kw_ref, vw_ref,
                    gt_ref, *, tm):
    xn = _rmsnorm_rows(x_ref[0], g_ref[...]).astype(BF16)

    def seg(lo, hi):
        return _dot(xn, w_ref[:, lo:hi])

    av = seg(AB_A, AB_A + 2 * GROUP_WIDTH)
    a_ref[0] = av[:, :GROUP_WIDTH] * _sigmoid(av[:, GROUP_WIDTH:])
    q_ref[0] = (seg(AB_Q, AB_KC) + qb_ref[...]).astype(BF16)
    kc_ref[0] = seg(AB_KC, AB_VC)
    vc_ref[0] = seg(AB_VC, AB_KS)
    vs_ref[0] = seg(AB_VS, AB_KW).astype(BF16)
    vw_ref[0] = seg(AB_VW, AB_G).astype(BF16)
    gt_ref[0] = _sigmoid(seg(AB_G, AB_COLS))

    tok = pl.program_id(1) * tm + lax.broadcasted_iota(jnp.int32, (tm, 1), 0)
    blk = lax.shift_right_logical(tok, int(np.log2(NSA_SEL_BLOCK)))
    pos = _pos_lanes(blk.astype(F32), (tok & (NSA_SEL_BLOCK - 1)).astype(F32), POS_TOK)
    onehot = jnp.where(lax.broadcasted_iota(jnp.int32, (1, LANES), 1) == blk, 1.0, 0.0).astype(BF16)
    ks = seg(AB_KS, AB_VS)
    kw = seg(AB_KW, AB_VW)
    for g in range(NSA_KV_GROUPS):
        ks_ref[0, :, 2 * g * LANES:(2 * g + 1) * LANES] = (ks[:, g * LANES:(g + 1) * LANES] + pos).astype(BF16)
        ks_ref[0, :, (2 * g + 1) * LANES:(2 * g + 2) * LANES] = onehot
        kw_ref[0, :, g * LANES:(g + 1) * LANES] = (kw[:, g * LANES:(g + 1) * LANES] + pos).astype(BF16)


def _arrange_ab_weight(w):
    d = w.shape[0]
    kvw = NSA_KV_GROUPS * HEAD_DIM
    o = 2 * GROUP_WIDTH
    a = w[:, :o]
    q = w[:, o:o + GROUP_WIDTH].reshape(d, NSA_HEADS, HEAD_DIM) * (QK_SCALE * LOG2_E)
    o += GROUP_WIDTH
    kc, vc, ks, vs, kw, vw = [w[:, o + i * kvw:o + (i + 1) * kvw] for i in range(6)]
    o += 6 * kvw
    g = w[:, o:].reshape(d, NSA_KV_GROUPS, NSA_REP * NSA_N_BRANCH)

    zeros_h = jnp.zeros((d, NSA_HEADS, HEAD_DIM), w.dtype)
    q_pad = jnp.concatenate([q, zeros_h], axis=-1).reshape(d, NSA_HEADS * LANES)

    def k_pad(k):
        k = k.reshape(d, NSA_KV_GROUPS, HEAD_DIM)
        return jnp.concatenate([k, jnp.zeros_like(k)], axis=-1).reshape(d, NSA_KV_GROUPS * LANES)

    def v_dup(v):
        v = v.reshape(d, NSA_KV_GROUPS, HEAD_DIM)
        return jnp.concatenate([v, v], axis=-1).reshape(d, NSA_KV_GROUPS * LANES)

    g_pad = jnp.pad(g, ((0, 0), (0, 0), (0, LANES - g.shape[-1]))).reshape(d, NSA_KV_GROUPS * LANES)
    out = jnp.concatenate([a, q_pad, kc, vc, k_pad(ks), v_dup(vs), k_pad(kw), v_dup(vw), g_pad], axis=-1)
    assert out.shape[1] == AB_COLS
    return out.astype(BF16)


def _bf16_terms(x, n):
    terms, rest = [], np.asarray(x, np.float64)
    for _ in range(n):
        term = rest.astype(BF16).astype(np.float64)
        terms.append(term)
        rest = rest - term
    return terms


def _alibi_query_lanes():
    row = np.zeros((NSA_HEADS, LANES), np.float64)
    for h in range(NSA_HEADS):
        slope = 2.0 ** (-8.0 * (h + 1) / NSA_HEADS)
        for i, term in enumerate(_bf16_terms(slope * np.log2(np.e), POS_SPLIT)):
            row[h, POS_TOK + i] = NSA_SEL_BLOCK * term
            row[h, POS_TOK + POS_SPLIT + i] = term
            row[h, POS_CMP + i] = 2 * NSA_CMP_STRIDE * term
            row[h, POS_CMP + POS_SPLIT + i] = NSA_CMP_STRIDE * term
    return jnp.asarray(row.reshape(1, NSA_HEADS * LANES), F32)


def _proj_ab(x, g, w):
    b, t, d = x.shape
    tm = min(PROJ_TM, t)
    wa = _arrange_ab_weight(w)

    def tok(width):
        return pl.BlockSpec((1, tm, width), lambda bi, i: (bi, i, 0))

    const = lambda width: pl.BlockSpec((1, width), lambda bi, i: (0, 0))
    kv = NSA_KV_GROUPS * LANES
    widths = [GROUP_WIDTH, NSA_HEADS * LANES, LANES, LANES, 2 * kv, kv, kv, kv, kv]
    dtypes = [F32, BF16, F32, F32, BF16, BF16, BF16, BF16, F32]
    return pl.pallas_call(
        functools.partial(_proj_ab_kernel, tm=tm),
        out_shape=[jax.ShapeDtypeStruct((b, t, wd), dt) for wd, dt in zip(widths, dtypes)],
        grid=(b, t // tm),
        in_specs=[tok(d), const(d), pl.BlockSpec((d, AB_COLS), lambda bi, i: (0, 0)), const(NSA_HEADS * LANES)],
        out_specs=[tok(wd) for wd in widths],
        compiler_params=_params("parallel", "parallel"),
        name="proj_ab",
    )(x, g.reshape(1, d), wa, _alibi_query_lanes())


CD_B = 0
CD_C = 512
CD_U = 1024
CD_Q = 1536
CD_K = 2560
CD_V = 3072
CD_COLS = 3584


def _proj_cd_kernel(x_ref, g_ref, w_ref, bg_ref, p_ref, q_ref, k_ref, v_ref):
    xn = _rmsnorm_rows(x_ref[0], g_ref[...]).astype(BF16)

    def seg(lo, hi):
        return _dot(xn, w_ref[:, lo:hi])

    bg_ref[0] = seg(CD_B, CD_C)
    p_ref[0] = seg(CD_C, CD_U) * seg(CD_U, CD_Q)
    q_ref[0] = seg(CD_Q, CD_K).astype(BF16)
    k_ref[0] = seg(CD_K, CD_V).astype(BF16)
    v_ref[0] = seg(CD_V, CD_COLS).astype(BF16)


def _arrange_cd_weight(w):
    d = w.shape[0]
    gw = GROUP_WIDTH
    q = w[:, 3 * gw:4 * gw].reshape(d, SB_HEADS // 2, 2, HEAD_DIM) * (QK_SCALE * LOG2_E)
    z = jnp.zeros((d, SB_HEADS // 2, HEAD_DIM), w.dtype)
    q_even = jnp.concatenate([q[:, :, 0], z], axis=-1)
    q_odd = jnp.concatenate([z, q[:, :, 1]], axis=-1)
    q_pad = jnp.stack([q_even, q_odd], axis=2).reshape(d, SB_HEADS * LANES)
    out = jnp.concatenate([w[:, :3 * gw], q_pad, w[:, 4 * gw:]], axis=-1)
    assert out.shape[1] == CD_COLS
    return out.astype(BF16)


def _proj_cd(x, g, w):
    b, t, d = x.shape
    tm = min(PROJ_TM, t)
    wa = _arrange_cd_weight(w)

    def tok(width):
        return pl.BlockSpec((1, tm, width), lambda bi, i: (bi, i, 0))

    widths = [GROUP_WIDTH, GROUP_WIDTH, SB_HEADS * LANES, GROUP_WIDTH, GROUP_WIDTH]
    dtypes = [F32, F32, BF16, BF16, BF16]
    return pl.pallas_call(
        _proj_cd_kernel,
        out_shape=[jax.ShapeDtypeStruct((b, t, wd), dt) for wd, dt in zip(widths, dtypes)],
        grid=(b, t // tm),
        in_specs=[tok(d), pl.BlockSpec((1, d), lambda bi, i: (0, 0)),
                  pl.BlockSpec((d, CD_COLS), lambda bi, i: (0, 0))],
        out_specs=[tok(wd) for wd in widths],
        compiler_params=_params("parallel", "parallel"),
        name="proj_cd",
    )(x, g.reshape(1, d), wa)


def _conv_kernel(cur_ref, halo_ref, w_ref, b_ref, lg_ref, lb_ref, o_ref, ext_ref, shift_ref, *, tt):
    i = pl.program_id(1)
    halo = halo_ref[0]
    ext_ref[0:CONV_HALO, :] = jnp.where(i == 0, jnp.zeros_like(halo), halo)
    ext_ref[CONV_HALO:, :] = cur_ref[0]
    off = CONV_HALO - (CONV_WIDTH - 1)
    last = off + CONV_WIDTH - 1
    steps = {phase: [m for m in range((last - phase) // SUBLANES + 1) if off <= SUBLANES * m + phase]
             for phase in range(SUBLANES)}
    for phase in range(SUBLANES):
        rows = SUBLANES * steps[phase][-1] + tt
        shift_ref[phase, 0:rows, :] = ext_ref[phase:phase + rows, :]
    for c in range(tt // CONV_CHUNK):
        r0 = c * CONV_CHUNK
        acc = jnp.zeros((CONV_CHUNK, GROUP_WIDTH), F32)
        for phase in range(SUBLANES):
            for m in steps[phase]:
                k = SUBLANES * m + phase - off
                lo = r0 + SUBLANES * m
                acc = acc + w_ref[k:k + 1, :] * shift_ref[phase, lo:lo + CONV_CHUNK, :]
        y = acc + b_ref[...]
        mu = jnp.mean(y, axis=-1, keepdims=True)
        yc = y - mu
        var = jnp.mean(yc * yc, axis=-1, keepdims=True)
        yn = yc * lax.rsqrt(var + LN_EPS) * lg_ref[...] + lb_ref[...]
        o_ref[0, r0:r0 + CONV_CHUNK, :] = (yn * _sigmoid(yn)).astype(o_ref.dtype)


def _conformer_conv(a, w, bias, ln_g, ln_b):
    b, t, c = a.shape
    tt = min(CONV_TT, t)
    hb = tt // CONV_HALO
    row = lambda v: v.reshape(1, c)
    const = lambda shape: pl.BlockSpec(shape, lambda bi, i: (0, 0))
    return pl.pallas_call(
        functools.partial(_conv_kernel, tt=tt),
        out_shape=jax.ShapeDtypeStruct((b, t, c), BF16),
        grid=(b, t // tt),
        in_specs=[pl.BlockSpec((1, tt, c), lambda bi, i: (bi, i, 0)),
                  pl.BlockSpec((1, CONV_HALO, c), lambda bi, i: (bi, jnp.maximum(i * hb - 1, 0), 0)),
                  const((CONV_WIDTH, c)), const((1, c)), const((1, c)), const((1, c))],
        out_specs=pl.BlockSpec((1, tt, c), lambda bi, i: (bi, i, 0)),
        scratch_shapes=[pltpu.VMEM((tt + CONV_HALO, c), F32), pltpu.VMEM((SUBLANES, tt + CONV_HALO, c), F32)],
        compiler_params=_params("parallel", "parallel"),
        name="conformer_conv",
    )(a, a, w, row(bias), row(ln_g), row(ln_b))


def _sc_kernel(p_ref, halo_ref, bg_ref, w_ref, o_ref, ext_ref, *, tt):
    i = pl.program_id(1)
    halo = halo_ref[0]
    ext_ref[0:SC_HALO, :] = jnp.where(i == 0, jnp.zeros_like(halo), halo)
    ext_ref[SC_HALO:, :] = p_ref[0]
    off = SC_HALO - (SC_WIDTH - 1)
    acc = jnp.zeros((tt, GROUP_WIDTH), F32)
    for k in range(SC_WIDTH):
        acc = acc + w_ref[k:k + 1, :] * ext_ref[off + k:off + k + tt, :]
    o_ref[0] = (bg_ref[0] * acc).astype(o_ref.dtype)


def _short_conv(p, bg, w):
    b, t, c = p.shape
    tt = min(SC_TT, t)
    hb = tt // SC_HALO
    return pl.pallas_call(
        functools.partial(_sc_kernel, tt=tt),
        out_shape=jax.ShapeDtypeStruct((b, t, c), BF16),
        grid=(b, t // tt),
        in_specs=[pl.BlockSpec((1, tt, c), lambda bi, i: (bi, i, 0)),
                  pl.BlockSpec((1, SC_HALO, c), lambda bi, i: (bi, jnp.maximum(i * hb - 1, 0), 0)),
                  pl.BlockSpec((1, tt, c), lambda bi, i: (bi, i, 0)),
                  pl.BlockSpec((SC_WIDTH, c), lambda bi, i: (0, 0))],
        out_specs=pl.BlockSpec((1, tt, c), lambda bi, i: (bi, i, 0)),
        scratch_shapes=[pltpu.VMEM((tt + SC_HALO, c), F32)],
        compiler_params=_params("parallel", "parallel"),
        name="short_conv",
    )(p, p, bg, w)


def _outproj_kernel(x_ref, a_ref, o_ref, wa_ref, wo_ref, y_ref):
    y_ref[...] = x_ref[...] + _dot(a_ref[...], wa_ref[...]) + _dot(o_ref[...], wo_ref[...])


def _outproj(x, a, o, w):
    n, d = x.shape
    gw = a.shape[1]
    tm = min(OUT_TM, n)
    w = w.astype(BF16)
    tok = lambda width: pl.BlockSpec((tm, width), lambda i: (i, 0))
    return pl.pallas_call(
        _outproj_kernel,
        out_shape=jax.ShapeDtypeStruct((n, d), F32),
        grid=(n // tm,),
        in_specs=[tok(d), tok(gw), tok(gw),
                  pl.BlockSpec((gw, d), lambda i: (0, 0)), pl.BlockSpec((gw, d), lambda i: (1, 0))],
        out_specs=tok(d),
        compiler_params=_params("parallel"),
        name="outproj",
    )(x, a, o, w, w)


def _gelu_tanh(x):
    return 0.5 * x * (1.0 + jnp.tanh(np.sqrt(2.0 / np.pi).astype(np.float32) * (x + 0.044715 * (x * x * x))))


def _compress_kernel(x_ref, pe_ref, w1_ref, w2_ref, o_ref, *, nchunk, keys):
    half = NSA_CMP_STRIDE * HEAD_DIM
    x = x_ref[0, 0]
    ha = _dot((x + pe_ref[:, :half]).astype(BF16), w1_ref[:half, :])
    hb = _dot((x + pe_ref[:, half:]).astype(BF16), w1_ref[half:, :])
    h = ha + pltpu.roll(hb, nchunk - 1, 0)
    y = _dot(_gelu_tanh(h).astype(BF16), w2_ref[...])
    row = lax.broadcasted_iota(jnp.int32, (nchunk, 1), 0)
    y = jnp.where(row < nchunk - 1, y, 0.0)
    if keys:
        y = y + _pos_lanes(lax.shift_right_logical(row, 1).astype(F32), (row & 1).astype(F32), POS_CMP)
    o_ref[0, 0] = y


def _compress(kx, pe, w1, w2, dup):
    b, t, _ = kx.shape
    g = NSA_KV_GROUPS
    nchunk = t // NSA_CMP_STRIDE
    feat = NSA_CMP_STRIDE * HEAD_DIM
    x = kx.reshape(b, nchunk, NSA_CMP_STRIDE, g, HEAD_DIM).transpose(0, 3, 1, 2, 4).reshape(b, g, nchunk, feat)
    w2p = jnp.concatenate([w2, w2 if dup else jnp.zeros_like(w2)], axis=-1).astype(BF16)
    y = pl.pallas_call(
        functools.partial(_compress_kernel, nchunk=nchunk, keys=not dup),
        out_shape=jax.ShapeDtypeStruct((b, g, nchunk, LANES), F32),
        grid=(b, g),
        in_specs=[pl.BlockSpec((1, 1, nchunk, feat), lambda bi, gi: (bi, gi, 0, 0)),
                  pl.BlockSpec((1, 2 * feat), lambda bi, gi: (0, 0)),
                  pl.BlockSpec((2 * feat, NSA_CMP_HIDDEN), lambda bi, gi: (0, 0)),
                  pl.BlockSpec((NSA_CMP_HIDDEN, LANES), lambda bi, gi: (0, 0))],
        out_specs=pl.BlockSpec((1, 1, nchunk, LANES), lambda bi, gi: (bi, gi, 0, 0)),
        compiler_params=_params("parallel", "parallel"),
        name="nsa_compress",
    )(x, pe.reshape(1, 2 * feat), w1.reshape(2 * feat, NSA_CMP_HIDDEN).astype(BF16), w2p)
    ratio = NSA_SEL_BLOCK // NSA_CMP_STRIDE
    return y.reshape(b, g, nchunk // ratio, ratio, LANES).transpose(0, 1, 3, 2, 4).astype(BF16)


def _nsa_kernel(q_ref, kc_ref, vc_ref, ks_ref, vs_ref, kw_ref, vw_ref, gt_ref, o_ref, *, t_len, ns):
    tq, tk, rep = NSA_TQ, NSA_TK, NSA_REP
    rows = rep * tq
    i = pl.program_id(2)
    q0 = i * tq

    q2 = q_ref[0]
    qg = jnp.concatenate([q2[:, r * LANES:(r + 1) * LANES] for r in range(rep)], axis=0)
    tq_i = q0 + lax.broadcasted_iota(jnp.int32, (tq, 1), 0)

    def per_head(x):
        return jnp.concatenate([x] * rep, axis=0)

    lane_i = lax.broadcasted_iota(jnp.int32, (1, ns), 1)
    ratio = NSA_SEL_BLOCK // NSA_CMP_STRIDE
    n_cmp = t_len // NSA_CMP_STRIDE - 1
    s_list = []
    for r in range(ratio):
        c_i = lane_i * ratio + r
        mask = ((c_i * NSA_CMP_STRIDE + (NSA_CMP_BLOCK - 1)) <= tq_i) & (c_i < n_cmp)
        s_list.append(_dot_nt(qg, kc_ref[0, 0, r]) + per_head(jnp.where(mask, 0.0, NEG_INF)))
    m = functools.reduce(jnp.maximum, [jnp.max(s, axis=-1, keepdims=True) for s in s_list])
    p_list = [jnp.exp2(s - m) for s in s_list]
    l = functools.reduce(lambda a, b: a + b, [jnp.sum(p, axis=-1, keepdims=True) for p in p_list])
    any_visible = per_head(tq_i >= NSA_CMP_BLOCK - 1)
    inv = jnp.where(any_visible, 1.0 / l, 0.0)
    p_list = [p * inv for p in p_list]
    o_c = functools.reduce(lambda a, b: a + b,
                           [_dot(p.astype(BF16), vc_ref[0, 0, r]) for r, p in enumerate(p_list)])

    def head_sum(p):
        return functools.reduce(lambda a, b: a + b, [p[r * tq:(r + 1) * tq] for r in range(rep)])

    ps = [head_sum(p) for p in p_list]
    lane_q = lax.broadcasted_iota(jnp.int32, (tq, ns), 1)
    prev_last = jnp.where(lane_q == 0, 0.0, pltpu.roll(ps[ratio - 1], 1, 1))
    imp = prev_last + ps[0] + ps[1] + ps[2] + ps[3]
    cur =lax.shift_right_logical(tq_i, int(np.log2(NSA_SEL_BLOCK)))
    visible = lane_q * NSA_SEL_BLOCK <= tq_i
    forced = (lane_q == 0) | (lane_q == cur) | (lane_q == cur - 1)
    score = jnp.where(visible, jnp.where(forced, SEL_FORCE, imp), -1.0)

    blk_f = lax.broadcasted_iota(jnp.int32, (ns, tq), 0).astype(F32)
    score_t = score.T
    taken = score_t == SEL_FORCE
    sel_t = jnp.where(taken, 1.0, 0.0)
    work = jnp.where(taken, -2.0, score_t)
    for _ in range(min(NSA_TOP_N, ns) - NSA_FORCED):
        top = jnp.max(work, axis=0, keepdims=True)
        idx = jnp.min(jnp.where(work == top, blk_f, float(ns)), axis=0, keepdims=True)
        hit = blk_f == idx
        sel_t = jnp.where(hit, 1.0, sel_t)
        work = jnp.where(hit, -2.0, work)
    sel = jnp.where(score >= 0.0, sel_t.T, 0.0)

    unsel = ((sel - 1.0) * MASK_BIG).astype(BF16)
    if ns < LANES:
        unsel = jnp.concatenate([unsel, jnp.zeros((tq, LANES - ns), BF16)], axis=1)
    q_aug = jnp.concatenate([qg, jnp.concatenate([unsel] * rep, axis=0)], axis=1)

    def sel_tile(k0, causal):
        s = _dot_nt(q_aug, ks_ref[0, pl.ds(k0, tk), :])
        if causal:
            tok = k0 + lax.broadcasted_iota(jnp.int32, (1, tk), 1)
            s = s + per_head(jnp.where(tok <= tq_i, 0.0, -MASK_BIG))
        return s

    blocks_per_tile = tk // NSA_SEL_BLOCK
    tile_of_blk = lax.shift_right_logical(lax.broadcasted_iota(jnp.int32, (ns, LANES), 0),
                                          int(np.log2(blocks_per_tile)))
    lane_t = lax.broadcasted_iota(jnp.int32, (ns, LANES), 1)
    per_tile = _dot(sel.astype(BF16), jnp.where(tile_of_blk == lane_t, 1.0, 0.0).astype(BF16))
    tile_any = jnp.max(per_tile, axis=0, keepdims=True) > 0.0
    pow2 = lax.shift_left(jnp.ones((1, LANES), jnp.int32), lane_t[:1] & 15).astype(F32)
    tile_bits = jnp.sum(jnp.where(tile_any, pow2, 0.0), axis=-1, keepdims=True).astype(jnp.int32)[0, 0]

    n_past = lax.shift_right_logical(q0, int(np.log2(tk)))
    kd = pl.multiple_of(n_past * tk, tk)
    s = sel_tile(kd, True)
    m_s = jnp.max(s, axis=-1, keepdims=True)
    p = jnp.exp2(s - m_s)
    l_s = jnp.sum(p, axis=-1, keepdims=True)
    acc_s = _dot(p.astype(BF16), vs_ref[0, pl.ds(kd, tk), :])

    def sel_update(kt, carry):
        m_run, l_run, acc = carry
        k0 = pl.multiple_of(kt * tk, tk)
        s = sel_tile(k0, False)
        m_new = jnp.maximum(m_run, jnp.max(s, axis=-1, keepdims=True))
        p = jnp.exp2(s - m_new)
        alpha = jnp.exp2(m_run - m_new)
        l_new = alpha * l_run + jnp.sum(p, axis=-1, keepdims=True)
        acc = alpha * acc + _dot(p.astype(BF16), vs_ref[0, pl.ds(k0, tk), :])
        return m_new, l_new, acc

    def next_active(state):
        return lax.while_loop(lambda st: (st[0] & 1) == 0,
                              lambda st: (lax.shift_right_logical(st[0], 1), st[1] + 1), state)

    def sel_body(state):
        pending, kt = next_active(state[:2])
        return (lax.shift_right_logical(pending, 1), kt + 1) + sel_update(kt, state[2:])

    past_bits = tile_bits & (lax.shift_left(jnp.int32(1), n_past) - 1)
    _, _, _, l_s, acc_s = lax.while_loop(lambda st: st[0] != 0, sel_body,
                                         (past_bits, jnp.int32(0), m_s, l_s, acc_s))
    o_s = acc_s / l_s

    wlen = NSA_WINDOW + tq
    w0 = pl.multiple_of(jnp.maximum(q0 - NSA_WINDOW, 0), tq)
    dist = tq_i - (w0 + lax.broadcasted_iota(jnp.int32, (1, wlen), 1))
    mask_w = (dist >= 0) & (dist < NSA_WINDOW)
    s_w = _dot_nt(qg, kw_ref[0, pl.ds(w0, wlen), :]) + per_head(jnp.where(mask_w, 0.0, NEG_INF))
    p_w = jnp.exp2(s_w - jnp.max(s_w, axis=-1, keepdims=True))
    l_w = jnp.sum(p_w, axis=-1, keepdims=True)
    o_w = _dot(p_w.astype(BF16), vw_ref[0, pl.ds(w0, wlen), :]) / l_w

    gt = gt_ref[0]

    def gate(branch):
        return jnp.concatenate(
            [gt[:, r * NSA_N_BRANCH + branch:r * NSA_N_BRANCH + branch + 1] for r in range(rep)], axis=0)

    o = gate(0) * o_c + gate(1) * o_s + gate(2) * o_w
    low_half = lax.broadcasted_iota(jnp.int32, (tq, LANES), 1) < HEAD_DIM
    for c in range(rep // 2):
        even = o[(2 * c) * tq:(2 * c + 1) * tq]
        odd = o[(2 * c + 1) * tq:(2 * c + 2) * tq]
        o_ref[0, :, c * LANES:(c + 1) * LANES] = jnp.where(low_half, even, odd).astype(o_ref.dtype)


def _nsa(q, kc, vc, ks, vs, kw, vw, gates):
    b, t, _ = q.shape
    g = NSA_KV_GROUPS
    ns = t // NSA_SEL_BLOCK
    assert ns <= LANES and t >= NSA_WINDOW + NSA_TQ and t % NSA_TK == 0 and NSA_TK % NSA_TQ == 0
    assert t // NSA_TK <= 16
    ratio = NSA_SEL_BLOCK // NSA_CMP_STRIDE
    qw = NSA_REP * LANES
    cmp_spec = pl.BlockSpec((1, 1, ratio, ns, LANES), lambda bi, gi, i: (bi, gi, 0, 0, 0))
    kv_spec = pl.BlockSpec((1, t, LANES), lambda bi, gi, i: (bi, 0, gi))
    ks_spec = pl.BlockSpec((1, t, 2 * LANES), lambda bi, gi, i: (bi, 0, gi))
    return pl.pallas_call(
        functools.partial(_nsa_kernel, t_len=t, ns=ns),
        out_shape=jax.ShapeDtypeStruct((b, t, GROUP_WIDTH), BF16),
        grid=(b, g, t // NSA_TQ),
        in_specs=[pl.BlockSpec((1, NSA_TQ, qw), lambda bi, gi, i: (bi, i, gi)),
                  cmp_spec, cmp_spec, ks_spec, kv_spec, kv_spec, kv_spec,
                  pl.BlockSpec((1, NSA_TQ, LANES), lambda bi, gi, i: (bi, i, gi))],
        out_specs=pl.BlockSpec((1, NSA_TQ, NSA_REP * HEAD_DIM), lambda bi, gi, i: (bi, i, gi)),
        compiler_params=_params("parallel", "parallel", "arbitrary"),
        name="nsa_attention",
    )(q, kc, vc, ks, vs, kw, vw, gates)


def _sb_tiles(qs, k, v, u2, carries, mask):
    stage1 = []
    for h, qh in enumerate(qs):
        p = h // 2
        z = _dot_nt(qh, k[:, p * LANES:(p + 1) * LANES])
        nk = jnp.maximum(z, 0.0) + jnp.log2(1.0 + jnp.exp2(-jnp.abs(z)))
        if mask is not None:
            nk = jnp.where(mask, nk, 0.0)
        hi = nk.astype(BF16)
        lo = (nk - hi.astype(F32)).astype(BF16)
        stage1.append((z, jnp.concatenate([hi, lo], axis=1)))
    laters = [_dot(hilo, u2) for (_, hilo) in stage1]
    out = []
    for h, ((z, _), later, (acc, c)) in enumerate(zip(stage1, laters, carries)):
        p = h // 2
        a = jnp.exp2(z + later)
        if mask is not None:
            a = jnp.where(mask, a, 0.0)
        acc = acc + jnp.exp2(c) * _dot(a.astype(BF16), v[:, p * LANES:(p + 1) * LANES])
        out.append((acc, c + later[:, 0:1]))
    return tuple(out)


def _sb_kernel(q_ref, k_ref, v_ref, u_ref, o_ref):
    tile = SB_T
    i = pl.program_id(2)
    q2 = q_ref[0]
    u = u_ref[...]
    qs = [q2[:, h * LANES:(h + 1) * LANES] for h in range(2 * SB_PAIRS)]
    q0 = pl.multiple_of(i * tile, tile)
    zero = (jnp.zeros((tile, LANES), F32), jnp.zeros((tile, 1), F32))
    mask = lax.broadcasted_iota(jnp.int32, (tile, tile), 1) < lax.broadcasted_iota(jnp.int32, (tile, tile), 0)
    carry = _sb_tiles(qs, k_ref[0, pl.ds(q0, tile), :], v_ref[0, pl.ds(q0, tile), :], u,
                      tuple(zero for _ in qs), mask)

    def body(jj, carry):
        k0 = pl.multiple_of((i - 1 - jj) * tile, tile)
        return _sb_tiles(qs, k_ref[0, pl.ds(k0, tile), :], v_ref[0, pl.ds(k0, tile), :], u, carry, None)

    def alive(carry):
        c_max = functools.reduce(jnp.maximum, [c for _, c in carry])
        return jnp.max(c_max) > SB_DEAD_LOG2

    def step(state):
        jj, _, carry = state
        carry = body(jj, carry)
        return jj + 1, alive(carry), carry

    _, _, carry = lax.while_loop(lambda st: (st[0] < i) & st[1], step, (jnp.int32(0), alive(carry), carry))
    low_half = lax.broadcasted_iota(jnp.int32, (tile, LANES), 1) < HEAD_DIM
    for p in range(SB_PAIRS):
        o_ref[0, :, p * LANES:(p + 1) * LANES] = jnp.where(
            low_half, carry[2 * p][0], carry[2 * p + 1][0]).astype(o_ref.dtype)


def _stick_breaking(q, k, v):
    b, t, _ = q.shape
    tile = SB_T
    assert t % tile == 0
    u = -(np.arange(tile)[:, None] >= np.arange(tile)[None, :]).astype(np.float32)
    u2 = np.concatenate([u, u], axis=0)
    kv_spec = pl.BlockSpec((1, t, SB_PAIRS * LANES), lambda bi, hp, i: (bi, 0, hp))
    return pl.pallas_call(
        _sb_kernel,
        out_shape=jax.ShapeDtypeStruct((b, t, GROUP_WIDTH), BF16),
        grid=(b, SB_HEADS // (2 * SB_PAIRS), t // tile),
        in_specs=[pl.BlockSpec((1, tile, 2 * SB_PAIRS * LANES), lambda bi, hp, i: (bi, i, hp)),
                  kv_spec, kv_spec,
                  pl.BlockSpec((2 * tile, tile), lambda bi, hp, i: (0, 0))],
        out_specs=pl.BlockSpec((1, tile, SB_PAIRS * LANES), lambda bi, hp, i: (bi, i, hp)),
        compiler_params=_params("parallel", "parallel", "arbitrary"),
        name="stick_breaking",
    )(q, k, v, jnp.asarray(u2, BF16))


def _mixer_conv_nsa(x, norm_g, w_in, dw_w, dw_b, ln_g, ln_b, pe_k, w1_k, w2_k, pe_v, w1_v, w2_v, w_out):
    b, t, d = x.shape
    a, q, kc, vc, ks, vs, kw, vw, gates = _proj_ab(x, norm_g, w_in)
    a = _conformer_conv(a, dw_w, dw_b, ln_g, ln_b)
    k_cmp = _compress(kc, pe_k, w1_k, w2_k, dup=False)
    v_cmp = _compress(vc, pe_v, w1_v, w2_v, dup=True)
    o = _nsa(q, k_cmp, v_cmp, ks, vs, kw, vw, gates)
    n = b * t
    return _outproj(x.reshape(n, d), a.reshape(n, -1), o.reshape(n, -1), w_out).reshape(b, t, d)


def _mixer_shortconv_sb(x, norm_g, w_in, sc_w, w_out):
    b, t, d = x.shape
    bg, p, q, k, v = _proj_cd(x, norm_g, w_in)
    c = _short_conv(p, bg, sc_w)
    o = _stick_breaking(q, k, v)
    n = b * t
    return _outproj(x.reshape(n, d), c.reshape(n, -1), o.reshape(n, -1), w_out).reshape(b, t, d)


def kernel(x, ffn1_norm, ffn1_w_in, ffn1_w_out, mix_norm, ffn2_norm, ffn2_w_in, ffn2_w_out, ab_w_in, conv_dw_w, conv_dw_b, conv_ln_g, conv_ln_b, nsa_pe_k, nsa_w1_k, nsa_w2_k, nsa_pe_v, nsa_w1_v, nsa_w2_v, ab_w_out, cd_w_in, sc_conv_w, cd_w_out, final_norm):
    b, t, d = x.shape
    depth = ffn1_norm.shape[0]
    n = b * t
    for layer in range(depth):
        x = _ffn(x.reshape(n, d), ffn1_norm[layer], ffn1_w_in[layer], ffn1_w_out[layer]).reshape(b, t, d)
        if layer % 2 == 0:
            e = layer // 2
            x = _mixer_conv_nsa(x, mix_norm[layer], ab_w_in[e], conv_dw_w[e], conv_dw_b[e], conv_ln_g[e],
                                conv_ln_b[e], nsa_pe_k[e], nsa_w1_k[e], nsa_w2_k[e],
                                nsa_pe_v[e], nsa_w1_v[e], nsa_w2_v[e], ab_w_out[e])
        else:
            o = layer // 2
            x = _mixer_shortconv_sb(x, mix_norm[layer], cd_w_in[o], sc_conv_w[o], cd_w_out[o])
        last = layer == depth - 1
        x = _ffn(x.reshape(n, d), ffn2_norm[layer], ffn2_w_in[layer], ffn2_w_out[layer],
                 final_g=final_norm if last else None).reshape(b, t, d)
    return x
```

```python
import functools

import numpy as np
import jax
import jax.numpy as jnp
from jax import lax
from jax.experimental import pallas as pl
from jax.experimental.pallas import tpu as pltpu

F32 = jnp.float32
BF16 = jnp.bfloat16

D_MODEL = 1024
HEAD_DIM = 64
GROUP_WIDTH = D_MODEL // 2
CONV_WIDTH = 31
NSA_HEADS = GROUP_WIDTH // HEAD_DIM
NSA_KV_GROUPS = 2
NSA_REP = NSA_HEADS // NSA_KV_GROUPS
NSA_CMP_BLOCK = 32
NSA_CMP_STRIDE = 16
NSA_CMP_HIDDEN = 128
NSA_SEL_BLOCK = 64
NSA_TOP_N = 16
NSA_FORCED = 3
NSA_WINDOW = 512
NSA_N_BRANCH = 3
SC_WIDTH = 3
SB_HEADS = GROUP_WIDTH // HEAD_DIM
D_FF = 2816
RMS_EPS = 1e-6
LN_EPS = 1e-5
NEG_INF = -1e30
SEL_FORCE = 1e4
QK_SCALE = HEAD_DIM ** -0.5
LOG2_E = float(np.log2(np.e))

LANES = 128
SUBLANES = 8
VMEM_LIMIT = 48 * 1024 * 1024

FFN_TM = 1024
FFN_TF = 256
PROJ_TM = 512
OUT_TM = 1024
CONV_TT = 256
CONV_HALO = 32
CONV_CHUNK = 64
SC_TT = 512
SC_HALO = 8
NSA_TQ = 256
NSA_TK = 512
SB_T = 256
SB_PAIRS = 2
SB_DEAD_LOG2 = -160.0


def _params(*sem):
    return pltpu.CompilerParams(dimension_semantics=sem, vmem_limit_bytes=VMEM_LIMIT)


def _dot(a, b):
    return jnp.dot(a, b, preferred_element_type=F32)


def _dot_nt(a, b):
    return lax.dot_general(a, b, (((1,), (1,)), ((), ())), preferred_element_type=F32)


def _sigmoid(x):
    return 1.0 / (1.0 + jnp.exp(-x))


def _rmsnorm_rows(x, g):
    return x * lax.rsqrt(jnp.mean(x * x, axis=-1, keepdims=True) + RMS_EPS) * g


def _ffn_kernel(x_ref, g_ref, wi_ref, wo_ref, *rest, n_ff, final_norm):
    if final_norm:
        fg_ref, o_ref = rest
    else:
        (o_ref,) = rest
    x = x_ref[...]
    xn = _rmsnorm_rows(x, g_ref[...]).astype(BF16)
    acc = None
    for j in range(n_ff):
        gate = _dot(xn, wi_ref[j])
        up = _dot(xn, wi_ref[j + n_ff])
        h = gate * _sigmoid(gate) * up
        part = _dot(h.astype(BF16), wo_ref[j * FFN_TF:(j + 1) * FFN_TF, :])
        acc = part if acc is None else acc + part
    y = x + 0.5 * acc
    if final_norm:
        y = _rmsnorm_rows(y, fg_ref[...])
    o_ref[...] = y


def _ffn_weights(w_in, w_out):
    depth, d, _ = w_in.shape
    n_ff = D_FF // FFN_TF
    return (w_in.astype(BF16).reshape(depth, d, 2 * n_ff, FFN_TF).transpose(0, 2, 1, 3), w_out.astype(BF16))


def _ffn(x, g, w_in, w_out, layer, final_g=None):
    n, d = x.shape
    n_ff = D_FF // FFN_TF
    tm = min(FFN_TM, n)
    once = pl.Buffered(1)
    in_specs = [
        pl.BlockSpec((tm, d), lambda i: (i, 0)),
        pl.BlockSpec((1, d), lambda i: (0, 0)),
        pl.BlockSpec((None, 2 * n_ff, d, FFN_TF), lambda i: (layer, 0, 0, 0), pipeline_mode=once),
        pl.BlockSpec((None, D_FF, d), lambda i: (layer, 0, 0), pipeline_mode=once),
    ]
    args = [x, g.reshape(1, d), w_in, w_out]
    if final_g is not None:
        in_specs.append(pl.BlockSpec((1, d), lambda i: (0, 0)))
        args.append(final_g.reshape(1, d))
    return pl.pallas_call(
        functools.partial(_ffn_kernel, n_ff=n_ff, final_norm=final_g is not None),
        out_shape=jax.ShapeDtypeStruct((n, d), F32),
        grid=(n // tm,),
        in_specs=in_specs,
        out_specs=pl.BlockSpec((tm, d), lambda i: (i, 0)),
        compiler_params=_params("parallel"),
        name="ffn",
    )(*args)


AB_A = 0
AB_Q = 1024
AB_KC = 2048
AB_VC = 2176
AB_KS = 2304
AB_VS = 2560
AB_KW = 2816
AB_VW = 3072
AB_G = 3328
AB_COLS = 3584

POS_SPLIT = 3
POS_TOK = HEAD_DIM
POS_CMP = POS_TOK + 2 * POS_SPLIT
MASK_BIG = 2.0 ** 100


def _pos_lanes(hi, lo, base):
    lane = lax.broadcasted_iota(jnp.int32, (1, LANES), 1)
    in_hi = (lane >= base) & (lane < base + POS_SPLIT)
    in_lo = (lane >= base + POS_SPLIT) & (lane < base + 2 * POS_SPLIT)
    return jnp.where(in_hi, hi, jnp.where(in_lo, lo, 0.0))


def _proj_ab_kernel(x_ref, g_ref, w_ref, qb_ref, a_ref, q_ref, kc_ref, vc_ref, ks_ref, vs_ref, kw_ref, vw_ref,
                    gt_ref, *, tm):
    xn = _rmsnorm_rows(x_ref[0], g_ref[...]).astype(BF16)

    def seg(lo, hi):
        return _dot(xn, w_ref[:, lo:hi])

    av = seg(AB_A, AB_A + 2 * GROUP_WIDTH)
    a_ref[0] = av[:, :GROUP_WIDTH] * _sigmoid(av[:, GROUP_WIDTH:])
    q_ref[0] = (seg(AB_Q, AB_KC) + qb_ref[...]).astype(BF16)
    kc_ref[0] = seg(AB_KC, AB_VC)
    vc_ref[0] = seg(AB_VC, AB_KS)
    vs_ref[0] = seg(AB_VS, AB_KW).astype(BF16)
    vw_ref[0] = seg(AB_VW, AB_G).astype(BF16)
    gt_ref[0] = _sigmoid(seg(AB_G, AB_COLS))

    tok = pl.program_id(1) * tm + lax.broadcasted_iota(jnp.int32, (tm, 1), 0)
    blk = lax.shift_right_logical(tok, int(np.log2(NSA_SEL_BLOCK)))
    pos = _pos_lanes(blk.astype(F32), (tok & (NSA_SEL_BLOCK - 1)).astype(F32), POS_TOK)
    onehot = jnp.where(lax.broadcasted_iota(jnp.int32, (1, LANES), 1) == blk, 1.0, 0.0).astype(BF16)
    ks = seg(AB_KS, AB_VS)
    kw = seg(AB_KW, AB_VW)
    for g in range(NSA_KV_GROUPS):
        ks_ref[0, :, 2 * g * LANES:(2 * g + 1) * LANES] = (ks[:, g * LANES:(g + 1) * LANES] + pos).astype(BF16)
        ks_ref[0, :, (2 * g + 1) * LANES:(2 * g + 2) * LANES] = onehot
        kw_ref[0, :, g * LANES:(g + 1) * LANES] = (kw[:, g * LANES:(g + 1) * LANES] + pos).astype(BF16)


def _arrange_ab_weight(w):
    d = w.shape[0]
    kvw = NSA_KV_GROUPS * HEAD_DIM
    o = 2 * GROUP_WIDTH
    a = w[:, :o]
    q = w[:, o:o + GROUP_WIDTH].reshape(d, NSA_HEADS, HEAD_DIM) * (QK_SCALE * LOG2_E)
    o += GROUP_WIDTH
    kc, vc, ks, vs, kw, vw = [w[:, o + i * kvw:o + (i + 1) * kvw] for i in range(6)]
    o += 6 * kvw
    g = w[:, o:].reshape(d, NSA_KV_GROUPS, NSA_REP * NSA_N_BRANCH)

    zeros_h = jnp.zeros((d, NSA_HEADS, HEAD_DIM), w.dtype)
    q_pad = jnp.concatenate([q, zeros_h], axis=-1).reshape(d, NSA_HEADS * LANES)

    def k_pad(k):
        k = k.reshape(d, NSA_KV_GROUPS, HEAD_DIM)
        return jnp.concatenate([k, jnp.zeros_like(k)], axis=-1).reshape(d, NSA_KV_GROUPS * LANES)

    def v_dup(v):
        v = v.reshape(d, NSA_KV_GROUPS, HEAD_DIM)
        return jnp.concatenate([v, v], axis=-1).reshape(d, NSA_KV_GROUPS * LANES)

    g_pad = jnp.pad(g, ((0, 0), (0, 0), (0, LANES - g.shape[-1]))).reshape(d, NSA_KV_GROUPS * LANES)
    out = jnp.concatenate([a, q_pad, kc, vc, k_pad(ks), v_dup(vs), k_pad(kw), v_dup(vw), g_pad], axis=-1)
    assert out.shape[1] == AB_COLS
    return out.astype(BF16)


def _bf16_terms(x, n):
    terms, rest = [], np.asarray(x, np.float64)
    for _ in range(n):
        term = rest.astype(BF16).astype(np.float64)
        terms.append(term)
        rest = rest - term
    return terms


def _alibi_query_lanes():
    row = np.zeros((NSA_HEADS, LANES), np.float64)
    for h in range(NSA_HEADS):
        slope = 2.0 ** (-8.0 * (h + 1) / NSA_HEADS)
        for i, term in enumerate(_bf16_terms(slope * np.log2(np.e), POS_SPLIT)):
            row[h, POS_TOK + i] = NSA_SEL_BLOCK * term
            row[h, POS_TOK + POS_SPLIT + i] = term
            row[h, POS_CMP + i] = 2 * NSA_CMP_STRIDE * term
            row[h, POS_CMP + POS_SPLIT + i] = NSA_CMP_STRIDE * term
    return jnp.asarray(row.reshape(1, NSA_HEADS * LANES), F32)


def _proj_ab(x, g, w):
    b, t, d = x.shape
    tm = min(PROJ_TM, t)
    wa = _arrange_ab_weight(w)

    def tok(width):
        return pl.BlockSpec((1, tm, width), lambda bi, i: (bi, i, 0))

    const = lambda width: pl.BlockSpec((1, width), lambda bi, i: (0, 0))
    kv = NSA_KV_GROUPS * LANES
    widths = [GROUP_WIDTH, NSA_HEADS * LANES, LANES, LANES, 2 * kv, kv, kv, kv, kv]
    dtypes = [F32, BF16, F32, F32, BF16, BF16, BF16, BF16, F32]
    return pl.pallas_call(
        functools.partial(_proj_ab_kernel, tm=tm),
        out_shape=[jax.ShapeDtypeStruct((b, t, wd), dt) for wd, dt in zip(widths, dtypes)],
        grid=(b, t // tm),
        in_specs=[tok(d), const(d), pl.BlockSpec((d, AB_COLS), lambda bi, i: (0, 0)), const(NSA_HEADS * LANES)],
        out_specs=[tok(wd) for wd in widths],
        compiler_params=_params("parallel", "parallel"),
        name="proj_ab",
    )(x, g.reshape(1, d), wa, _alibi_query_lanes())


CD_B = 0
CD_C = 512
CD_U = 1024
CD_Q = 1536
CD_K = 2560
CD_V = 3072
CD_COLS = 3584


def _proj_cd_kernel(x_ref, g_ref, w_ref, bg_ref, p_ref, q_ref, k_ref, v_ref):
    xn = _rmsnorm_rows(x_ref[0], g_ref[...]).astype(BF16)

    def seg(lo, hi):
        return _dot(xn, w_ref[:, lo:hi])

    bg_ref[0] = seg(CD_B, CD_C)
    p_ref[0] = seg(CD_C, CD_U) * seg(CD_U, CD_Q)
    q_ref[0] = seg(CD_Q, CD_K).astype(BF16)
    k_ref[0] = seg(CD_K, CD_V).astype(BF16)
    v_ref[0] = seg(CD_V, CD_COLS).astype(BF16)


def _arrange_cd_weight(w):
    d = w.shape[0]
    gw = GROUP_WIDTH
    q = w[:, 3 * gw:4 * gw].reshape(d, SB_HEADS // 2, 2, HEAD_DIM) * (QK_SCALE * LOG2_E)
    z = jnp.zeros((d, SB_HEADS // 2, HEAD_DIM), w.dtype)
    q_even = jnp.concatenate([q[:, :, 0], z], axis=-1)
    q_odd = jnp.concatenate([z, q[:, :, 1]], axis=-1)
    q_pad = jnp.stack([q_even, q_odd], axis=2).reshape(d, SB_HEADS * LANES)
    out = jnp.concatenate([w[:, :3 * gw], q_pad, w[:, 4 * gw:]], axis=-1)
    assert out.shape[1] == CD_COLS
    return out.astype(BF16)


def _proj_cd(x, g, w):
    b, t, d = x.shape
    tm = min(PROJ_TM, t)
    wa = _arrange_cd_weight(w)

    def tok(width):
        return pl.BlockSpec((1, tm, width), lambda bi, i: (bi, i, 0))

    widths = [GROUP_WIDTH, GROUP_WIDTH, SB_HEADS * LANES, GROUP_WIDTH, GROUP_WIDTH]
    dtypes = [F32, F32, BF16, BF16, BF16]
    return pl.pallas_call(
        _proj_cd_kernel,
        out_shape=[jax.ShapeDtypeStruct((b, t, wd), dt) for wd, dt in zip(widths, dtypes)],
        grid=(b, t // tm),
        in_specs=[tok(d), pl.BlockSpec((1, d), lambda bi, i: (0, 0)),
                  pl.BlockSpec((d, CD_COLS), lambda bi, i: (0, 0))],
        out_specs=[tok(wd) for wd in widths],
        compiler_params=_params("parallel", "parallel"),
        name="proj_cd",
    )(x, g.reshape(1, d), wa)


def _conv_kernel(cur_ref, halo_ref, w_ref, b_ref, lg_ref, lb_ref, o_ref, ext_ref, shift_ref, *, tt):
    i = pl.program_id(1)
    halo = halo_ref[0]
    ext_ref[0:CONV_HALO, :] = jnp.where(i == 0, jnp.zeros_like(halo), halo)
    ext_ref[CONV_HALO:, :] = cur_ref[0]
    off = CONV_HALO - (CONV_WIDTH - 1)
    last = off + CONV_WIDTH - 1
    steps = {phase: [m for m in range((last - phase) // SUBLANES + 1) if off <= SUBLANES * m + phase]
             for phase in range(SUBLANES)}
    for phase in range(SUBLANES):
        rows = SUBLANES * steps[phase][-1] + tt
        shift_ref[phase, 0:rows, :] = ext_ref[phase:phase + rows, :]
    for c in range(tt // CONV_CHUNK):
        r0 = c * CONV_CHUNK
        acc = jnp.zeros((CONV_CHUNK, GROUP_WIDTH), F32)
        for phase in range(SUBLANES):
            for m in steps[phase]:
                k = SUBLANES * m + phase - off
                lo = r0 + SUBLANES * m
                acc = acc + w_ref[k:k + 1, :] * shift_ref[phase, lo:lo + CONV_CHUNK, :]
        y = acc + b_ref[...]
        mu = jnp.mean(y, axis=-1, keepdims=True)
        yc = y - mu
        var = jnp.mean(yc * yc, axis=-1, keepdims=True)
        yn = yc * lax.rsqrt(var + LN_EPS) * lg_ref[...] + lb_ref[...]
        o_ref[0, r0:r0 + CONV_CHUNK, :] = (yn * _sigmoid(yn)).astype(o_ref.dtype)


def _conformer_conv(a, w, bias, ln_g, ln_b):
    b, t, c = a.shape
    tt = min(CONV_TT, t)
    hb = tt // CONV_HALO
    row = lambda v: v.reshape(1, c)
    const = lambda shape: pl.BlockSpec(shape, lambda bi, i: (0, 0))
    return pl.pallas_call(
        functools.partial(_conv_kernel, tt=tt),
        out_shape=jax.ShapeDtypeStruct((b, t, c), BF16),
        grid=(b, t // tt),
        in_specs=[pl.BlockSpec((1, tt, c), lambda bi, i: (bi, i, 0)),
                  pl.BlockSpec((1, CONV_HALO, c), lambda bi, i: (bi, jnp.maximum(i * hb - 1, 0), 0)),
                  const((CONV_WIDTH, c)), const((1, c)), const((1, c)), const((1, c))],
        out_specs=pl.BlockSpec((1, tt, c), lambda bi, i: (bi, i, 0)),
        scratch_shapes=[pltpu.VMEM((tt + CONV_HALO, c), F32), pltpu.VMEM((SUBLANES, tt + CONV_HALO, c), F32)],
        compiler_params=_params("parallel", "parallel"),
        name="conformer_conv",
    )(a, a, w, row(bias), row(ln_g), row(ln_b))


def _sc_kernel(p_ref, halo_ref, bg_ref, w_ref, o_ref, ext_ref, *, tt):
    i = pl.program_id(1)
    halo = halo_ref[0]
    ext_ref[0:SC_HALO, :] = jnp.where(i == 0, jnp.zeros_like(halo), halo)
    ext_ref[SC_HALO:, :] = p_ref[0]
    off = SC_HALO - (SC_WIDTH - 1)
    acc = jnp.zeros((tt, GROUP_WIDTH), F32)
    for k in range(SC_WIDTH):
        acc = acc + w_ref[k:k + 1, :] * ext_ref[off + k:off + k + tt, :]
    o_ref[0] = (bg_ref[0] * acc).astype(o_ref.dtype)


def _short_conv(p, bg, w):
    b, t, c = p.shape
    tt = min(SC_TT, t)
    hb = tt // SC_HALO
    return pl.pallas_call(
        functools.partial(_sc_kernel, tt=tt),
        out_shape=jax.ShapeDtypeStruct((b, t, c), BF16),
        grid=(b, t // tt),
        in_specs=[pl.BlockSpec((1, tt, c), lambda bi, i: (bi, i, 0)),
                  pl.BlockSpec((1, SC_HALO, c), lambda bi, i: (bi, jnp.maximum(i * hb - 1, 0), 0)),
                  pl.BlockSpec((1, tt, c), lambda bi, i: (bi, i, 0)),
                  pl.BlockSpec((SC_WIDTH, c), lambda bi, i: (0, 0))],
        out_specs=pl.BlockSpec((1, tt, c), lambda bi, i: (bi, i, 0)),
        scratch_shapes=[pltpu.VMEM((tt + SC_HALO, c), F32)],
        compiler_params=_params("parallel", "parallel"),
        name="short_conv",
    )(p, p, bg, w)


def _outproj_kernel(x_ref, a_ref, o_ref, wa_ref, wo_ref, y_ref):
    y_ref[...] = x_ref[...] + _dot(a_ref[...], wa_ref[...]) + _dot(o_ref[...], wo_ref[...])


def _outproj(x, a, o, w):
    n, d = x.shape
    gw = a.shape[1]
    tm = min(OUT_TM, n)
    w = w.astype(BF16)
    tok = lambda width: pl.BlockSpec((tm, width), lambda i: (i, 0))
    return pl.pallas_call(
        _outproj_kernel,
        out_shape=jax.ShapeDtypeStruct((n, d), F32),
        grid=(n // tm,),
        in_specs=[tok(d), tok(gw), tok(gw),
                  pl.BlockSpec((gw, d), lambda i: (0, 0)), pl.BlockSpec((gw, d), lambda i: (1, 0))],
        out_specs=tok(d),
        compiler_params=_params("parallel"),
        name="outproj",
    )(x, a, o, w, w)


def _gelu_tanh(x):
    return 0.5 * x * (1.0 + jnp.tanh(np.sqrt(2.0 / np.pi).astype(np.float32) * (x + 0.044715 * (x * x * x))))


def _compress_kernel(x_ref, pe_ref, w1_ref, w2_ref, o_ref, *, nchunk, keys):
    half = NSA_CMP_STRIDE * HEAD_DIM
    x = x_ref[0, 0]
    ha = _dot((x + pe_ref[:, :half]).astype(BF16), w1_ref[:half, :])
    hb = _dot((x + pe_ref[:, half:]).astype(BF16), w1_ref[half:, :])
    h = ha + pltpu.roll(hb, nchunk - 1, 0)
    y = _dot(_gelu_tanh(h).astype(BF16), w2_ref[...])
    row = lax.broadcasted_iota(jnp.int32, (nchunk, 1), 0)
    y = jnp.where(row < nchunk - 1, y, 0.0)
    if keys:
        y = y + _pos_lanes(lax.shift_right_logical(row, 1).astype(F32), (row & 1).astype(F32), POS_CMP)
    o_ref[0, 0] = y


def _compress(kx, pe, w1, w2, dup):
    b, t, _ = kx.shape
    g = NSA_KV_GROUPS
    nchunk = t // NSA_CMP_STRIDE
    feat = NSA_CMP_STRIDE * HEAD_DIM
    x = kx.reshape(b, nchunk, NSA_CMP_STRIDE, g, HEAD_DIM).transpose(0, 3, 1, 2, 4).reshape(b, g, nchunk, feat)
    w2p = jnp.concatenate([w2, w2 if dup else jnp.zeros_like(w2)], axis=-1).astype(BF16)
    y = pl.pallas_call(
        functools.partial(_compress_kernel, nchunk=nchunk, keys=not dup),
        out_shape=jax.ShapeDtypeStruct((b, g, nchunk, LANES), F32),
        grid=(b, g),
        in_specs=[pl.BlockSpec((1, 1, nchunk, feat), lambda bi, gi: (bi, gi, 0, 0)),
                  pl.BlockSpec((1, 2 * feat), lambda bi, gi: (0, 0)),
                  pl.BlockSpec((2 * feat, NSA_CMP_HIDDEN), lambda bi, gi: (0, 0)),
                  pl.BlockSpec((NSA_CMP_HIDDEN, LANES), lambda bi, gi: (0, 0))],
        out_specs=pl.BlockSpec((1, 1, nchunk, LANES), lambda bi, gi: (bi, gi, 0, 0)),
        compiler_params=_params("parallel", "parallel"),
        name="nsa_compress",
    )(x, pe.reshape(1, 2 * feat), w1.reshape(2 * feat, NSA_CMP_HIDDEN).astype(BF16), w2p)
    ratio = NSA_SEL_BLOCK // NSA_CMP_STRIDE
    return y.reshape(b, g, nchunk // ratio, ratio, LANES).transpose(0, 1, 3, 2, 4).astype(BF16)


def _nsa_kernel(q_ref, kc_ref, vc_ref, ks_ref, vs_ref, kw_ref, vw_ref, gt_ref, o_ref, *, t_len, ns):
    tq, tk, rep = NSA_TQ, NSA_TK, NSA_REP
    rows = rep * tq
    i = pl.program_id(2)
    q0 = i * tq

    q2 = q_ref[0]
    qg = jnp.concatenate([q2[:, r * LANES:(r + 1) * LANES] for r in range(rep)], axis=0)
    tq_i = q0 + lax.broadcasted_iota(jnp.int32, (tq, 1), 0)

    def per_head(x):
        return jnp.concatenate([x] * rep, axis=0)

    lane_i = lax.broadcasted_iota(jnp.int32, (1, ns), 1)
    ratio = NSA_SEL_BLOCK // NSA_CMP_STRIDE
    n_cmp = t_len // NSA_CMP_STRIDE - 1
    s_list = []
    for r in range(ratio):
        c_i = lane_i * ratio + r
        mask = ((c_i * NSA_CMP_STRIDE + (NSA_CMP_BLOCK - 1)) <= tq_i) & (c_i < n_cmp)
        s_list.append(_dot_nt(qg, kc_ref[0, 0, r]) + per_head(jnp.where(mask, 0.0, NEG_INF)))
    m = functools.reduce(jnp.maximum, [jnp.max(s, axis=-1, keepdims=True) for s in s_list])
    p_list = [jnp.exp2(s - m) for s in s_list]
    l = functools.reduce(lambda a, b: a + b, [jnp.sum(p, axis=-1, keepdims=True) for p in p_list])
    any_visible = per_head(tq_i >= NSA_CMP_BLOCK - 1)
    inv = jnp.where(any_visible, 1.0 / l, 0.0)
    p_list = [p * inv for p in p_list]
    o_c = functools.reduce(lambda a, b: a + b,
                           [_dot(p.astype(BF16), vc_ref[0, 0, r]) for r, p in enumerate(p_list)])

    def head_sum(p):
        return functools.reduce(lambda a, b: a + b, [p[r * tq:(r + 1) * tq] for r in range(rep)])

    ps = [head_sum(p) for p in p_list]
    lane_q = lax.broadcasted_iota(jnp.int32, (tq, ns), 1)
    prev_last = jnp.where(lane_q == 0, 0.0, pltpu.roll(ps[ratio - 1], 1, 1))
    imp = prev_last + ps[0] + ps[1] + ps[2] + ps[3]
    cur =lax.shift_right_logical(tq_i, int(np.log2(NSA_SEL_BLOCK)))
    visible = lane_q * NSA_SEL_BLOCK <= tq_i
    forced = (lane_q == 0) | (lane_q == cur) | (lane_q == cur - 1)
    score = jnp.where(visible, jnp.where(forced, SEL_FORCE, imp), -1.0)

    blk_f = lax.broadcasted_iota(jnp.int32, (ns, tq), 0).astype(F32)
    score_t = score.T
    taken = score_t == SEL_FORCE
    sel_t = jnp.where(taken, 1.0, 0.0)
    work = jnp.where(taken, -2.0, score_t)
    for _ in range(min(NSA_TOP_N, ns) - NSA_FORCED):
        top = jnp.max(work, axis=0, keepdims=True)
        idx = jnp.min(jnp.where(work == top, blk_f, float(ns)), axis=0, keepdims=True)
        hit = blk_f == idx
        sel_t = jnp.where(hit, 1.0, sel_t)
        work = jnp.where(hit, -2.0, work)
    sel = jnp.where(score >= 0.0, sel_t.T, 0.0)

    unsel = ((sel - 1.0) * MASK_BIG).astype(BF16)
    if ns < LANES:
        unsel = jnp.concatenate([unsel, jnp.zeros((tq, LANES - ns), BF16)], axis=1)
    q_aug = jnp.concatenate([qg, jnp.concatenate([unsel] * rep, axis=0)], axis=1)

    def sel_tile(k0, causal):
        s = _dot_nt(q_aug, ks_ref[0, pl.ds(k0, tk), :])
        if causal:
            tok = k0 + lax.broadcasted_iota(jnp.int32, (1, tk), 1)
            s = s + per_head(jnp.where(tok <= tq_i, 0.0, -MASK_BIG))
        return s

    blocks_per_tile = tk // NSA_SEL_BLOCK
    tile_of_blk = lax.shift_right_logical(lax.broadcasted_iota(jnp.int32, (ns, LANES), 0),
                                          int(np.log2(blocks_per_tile)))
    lane_t = lax.broadcasted_iota(jnp.int32, (ns, LANES), 1)
    per_tile = _dot(sel.astype(BF16), jnp.where(tile_of_blk == lane_t, 1.0, 0.0).astype(BF16))
    tile_any = jnp.max(per_tile, axis=0, keepdims=True) > 0.0
    pow2 = lax.shift_left(jnp.ones((1, LANES), jnp.int32), lane_t[:1] & 15).astype(F32)
    tile_bits = jnp.sum(jnp.where(tile_any, pow2, 0.0), axis=-1, keepdims=True).astype(jnp.int32)[0, 0]

    n_past = lax.shift_right_logical(q0, int(np.log2(tk)))
    kd = pl.multiple_of(n_past * tk, tk)
    s = sel_tile(kd, True)
    m_s = jnp.max(s, axis=-1, keepdims=True)
    p = jnp.exp2(s - m_s)
    l_s = jnp.sum(p, axis=-1, keepdims=True)
    acc_s = _dot(p.astype(BF16), vs_ref[0, pl.ds(kd, tk), :])

    def sel_update(kt, carry):
        m_run, l_run, acc = carry
        k0 = pl.multiple_of(kt * tk, tk)
        s = sel_tile(k0, False)
        m_new = jnp.maximum(m_run, jnp.max(s, axis=-1, keepdims=True))
        p = jnp.exp2(s - m_new)
        alpha = jnp.exp2(m_run - m_new)
        l_new = alpha * l_run + jnp.sum(p, axis=-1, keepdims=True)
        acc = alpha * acc + _dot(p.astype(BF16), vs_ref[0, pl.ds(k0, tk), :])
        return m_new, l_new, acc

    def next_active(state):
        return lax.while_loop(lambda st: (st[0] & 1) == 0,
                              lambda st: (lax.shift_right_logical(st[0], 1), st[1] + 1), state)

    def sel_body(state):
        pending, kt = next_active(state[:2])
        return (lax.shift_right_logical(pending, 1), kt + 1) + sel_update(kt, state[2:])

    past_bits = tile_bits & (lax.shift_left(jnp.int32(1), n_past) - 1)
    _, _, _, l_s, acc_s = lax.while_loop(lambda st: st[0] != 0, sel_body,
                                         (past_bits, jnp.int32(0), m_s, l_s, acc_s))
    o_s = acc_s / l_s

    wlen = NSA_WINDOW + tq
    w0 = pl.multiple_of(jnp.maximum(q0 - NSA_WINDOW, 0), tq)
    dist = tq_i - (w0 + lax.broadcasted_iota(jnp.int32, (1, wlen), 1))
    mask_w = (dist >= 0) & (dist < NSA_WINDOW)
    s_w = _dot_nt(qg, kw_ref[0, pl.ds(w0, wlen), :]) + per_head(jnp.where(mask_w, 0.0, NEG_INF))
    p_w = jnp.exp2(s_w - jnp.max(s_w, axis=-1, keepdims=True))
    l_w = jnp.sum(p_w, axis=-1, keepdims=True)
    o_w = _dot(p_w.astype(BF16), vw_ref[0, pl.ds(w0, wlen), :]) / l_w

    gt = gt_ref[0]

    def gate(branch):
        return jnp.concatenate(
            [gt[:, r * NSA_N_BRANCH + branch:r * NSA_N_BRANCH + branch + 1] for r in range(rep)], axis=0)

    o = gate(0) * o_c + gate(1) * o_s + gate(2) * o_w
    low_half = lax.broadcasted_iota(jnp.int32, (tq, LANES), 1) < HEAD_DIM
    for c in range(rep // 2):
        even = o[(2 * c) * tq:(2 * c + 1) * tq]
        odd = o[(2 * c + 1) * tq:(2 * c + 2) * tq]
        o_ref[0, :, c * LANES:(c + 1) * LANES] = jnp.where(low_half, even, odd).astype(o_ref.dtype)


def _nsa(q, kc, vc, ks, vs, kw, vw, gates):
    b, t, _ = q.shape
    g = NSA_KV_GROUPS
    ns = t // NSA_SEL_BLOCK
    assert ns <= LANES and t >= NSA_WINDOW + NSA_TQ and t % NSA_TK == 0 and NSA_TK % NSA_TQ == 0
    assert t // NSA_TK <= 16
    ratio = NSA_SEL_BLOCK // NSA_CMP_STRIDE
    qw = NSA_REP * LANES
    cmp_spec = pl.BlockSpec((1, 1, ratio, ns, LANES), lambda bi, gi, i: (bi, gi, 0, 0, 0))
    kv_spec = pl.BlockSpec((1, t, LANES), lambda bi, gi, i: (bi, 0, gi))
    ks_spec = pl.BlockSpec((1, t, 2 * LANES), lambda bi, gi, i: (bi, 0, gi))
    return pl.pallas_call(
        functools.partial(_nsa_kernel, t_len=t, ns=ns),
        out_shape=jax.ShapeDtypeStruct((b, t, GROUP_WIDTH), BF16),
        grid=(b, g, t // NSA_TQ),
        in_specs=[pl.BlockSpec((1, NSA_TQ, qw), lambda bi, gi, i: (bi, i, gi)),
                  cmp_spec, cmp_spec, ks_spec, kv_spec, kv_spec, kv_spec,
                  pl.BlockSpec((1, NSA_TQ, LANES), lambda bi, gi, i: (bi, i, gi))],
        out_specs=pl.BlockSpec((1, NSA_TQ, NSA_REP * HEAD_DIM), lambda bi, gi, i: (bi, i, gi)),
        compiler_params=_params("parallel", "parallel", "arbitrary"),
        name="nsa_attention",
    )(q, kc, vc, ks, vs, kw, vw, gates)


def _sb_tiles(qs, k, v, u2, carries, mask):
    stage1 = []
    for h, qh in enumerate(qs):
        p = h // 2
        z = _dot_nt(qh, k[:, p * LANES:(p + 1) * LANES])
        nk = jnp.maximum(z, 0.0) + jnp.log2(1.0 + jnp.exp2(-jnp.abs(z)))
        if mask is not None:
            nk = jnp.where(mask, nk, 0.0)
        hi = nk.astype(BF16)
        lo = (nk - hi.astype(F32)).astype(BF16)
        stage1.append((z, jnp.concatenate([hi, lo], axis=1)))
    laters = [_dot(hilo, u2) for (_, hilo) in stage1]
    out = []
    for h, ((z, _), later, (acc, c)) in enumerate(zip(stage1, laters, carries)):
        p = h // 2
        a = jnp.exp2(z + later)
        if mask is not None:
            a = jnp.where(mask, a, 0.0)
        acc = acc + jnp.exp2(c) * _dot(a.astype(BF16), v[:, p * LANES:(p + 1) * LANES])
        out.append((acc, c + later[:, 0:1]))
    return tuple(out)


def _sb_kernel(q_ref, k_ref, v_ref, u_ref, o_ref):
    tile = SB_T
    i = pl.program_id(2)
    q2 = q_ref[0]
    u = u_ref[...]
    qs = [q2[:, h * LANES:(h + 1) * LANES] for h in range(2 * SB_PAIRS)]
    q0 = pl.multiple_of(i * tile, tile)
    zero = (jnp.zeros((tile, LANES), F32), jnp.zeros((tile, 1), F32))
    mask = lax.broadcasted_iota(jnp.int32, (tile, tile), 1) < lax.broadcasted_iota(jnp.int32, (tile, tile), 0)
    carry = _sb_tiles(qs, k_ref[0, pl.ds(q0, tile), :], v_ref[0, pl.ds(q0, tile), :], u,
                      tuple(zero for _ in qs), mask)

    def body(jj, carry):
        k0 = pl.multiple_of((i - 1 - jj) * tile, tile)
        return _sb_tiles(qs, k_ref[0, pl.ds(k0, tile), :], v_ref[0, pl.ds(k0, tile), :], u, carry, None)

    def alive(carry):
        c_max = functools.reduce(jnp.maximum, [c for _, c in carry])
        return jnp.max(c_max) > SB_DEAD_LOG2

    def step(state):
        jj, _, carry = state
        carry = body(jj, carry)
        return jj + 1, alive(carry), carry

    _, _, carry = lax.while_loop(lambda st: (st[0] < i) & st[1], step, (jnp.int32(0), alive(carry), carry))
    low_half = lax.broadcasted_iota(jnp.int32, (tile, LANES), 1) < HEAD_DIM
    for p in range(SB_PAIRS):
        o_ref[0, :, p * LANES:(p + 1) * LANES] = jnp.where(
            low_half, carry[2 * p][0], carry[2 * p + 1][0]).astype(o_ref.dtype)


def _stick_breaking(q, k, v):
    b, t, _ = q.shape
    tile = SB_T
    assert t % tile == 0
    u = -(np.arange(tile)[:, None] >= np.arange(tile)[None, :]).astype(np.float32)
    u2 = np.concatenate([u, u], axis=0)
    kv_spec = pl.BlockSpec((1, t, SB_PAIRS * LANES), lambda bi, hp, i: (bi, 0, hp))
    return pl.pallas_call(
        _sb_kernel,
        out_shape=jax.ShapeDtypeStruct((b, t, GROUP_WIDTH), BF16),
        grid=(b, SB_HEADS // (2 * SB_PAIRS), t // tile),
        in_specs=[pl.BlockSpec((1, tile, 2 * SB_PAIRS * LANES), lambda bi, hp, i: (bi, i, hp)),
                  kv_spec, kv_spec,
                  pl.BlockSpec((2 * tile, tile), lambda bi, hp, i: (0, 0))],
        out_specs=pl.BlockSpec((1, tile, SB_PAIRS * LANES), lambda bi, hp, i: (bi, i, hp)),
        compiler_params=_params("parallel", "parallel", "arbitrary"),
        name="stick_breaking",
    )(q, k, v, jnp.asarray(u2, BF16))


def _mixer_conv_nsa(x, norm_g, w_in, dw_w, dw_b, ln_g, ln_b, pe_k, w1_k, w2_k, pe_v, w1_v, w2_v, w_out):
    b, t, d = x.shape
    a, q, kc, vc, ks, vs, kw, vw, gates = _proj_ab(x, norm_g, w_in)
    a = _conformer_conv(a, dw_w, dw_b, ln_g, ln_b)
    k_cmp = _compress(kc, pe_k, w1_k, w2_k, dup=False)
    v_cmp = _compress(vc, pe_v, w1_v, w2_v, dup=True)
    o = _nsa(q, k_cmp, v_cmp, ks, vs, kw, vw, gates)
    n = b * t
    return _outproj(x.reshape(n, d), a.reshape(n, -1), o.reshape(n, -1), w_out).reshape(b, t, d)


def _mixer_shortconv_sb(x, norm_g, w_in, sc_w, w_out):
    b, t, d = x.shape
    bg, p, q, k, v = _proj_cd(x, norm_g, w_in)
    c = _short_conv(p, bg, sc_w)
    o = _stick_breaking(q, k, v)
    n = b * t
    return _outproj(x.reshape(n, d), c.reshape(n, -1), o.reshape(n, -1), w_out).reshape(b, t, d)


def kernel(x, ffn1_norm, ffn1_w_in, ffn1_w_out, mix_norm, ffn2_norm, ffn2_w_in, ffn2_w_out, ab_w_in, conv_dw_w, conv_dw_b, conv_ln_g, conv_ln_b, nsa_pe_k, nsa_w1_k, nsa_w2_k, nsa_pe_v, nsa_w1_v, nsa_w2_v, ab_w_out, cd_w_in, sc_conv_w, cd_w_out, final_norm):
    b, t, d = x.shape
    depth = ffn1_norm.shape[0]
    n = b * t
    ffn1_w = _ffn_weights(ffn1_w_in, ffn1_w_out)
    ffn2_w = _ffn_weights(ffn2_w_in, ffn2_w_out)
    for layer in range(depth):
        x = _ffn(x.reshape(n, d), ffn1_norm[layer], *ffn1_w, layer).reshape(b, t, d)
        if layer % 2 == 0:
            e = layer // 2
            x = _mixer_conv_nsa(x, mix_norm[layer], ab_w_in[e], conv_dw_w[e], conv_dw_b[e], conv_ln_g[e],
                                conv_ln_b[e], nsa_pe_k[e], nsa_w1_k[e], nsa_w2_k[e],
                                nsa_pe_v[e], nsa_w1_v[e], nsa_w2_v[e], ab_w_out[e])
        else:
            o = layer // 2
            x = _mixer_shortconv_sb(x, mix_norm[layer], cd_w_in[o], sc_conv_w[o], cd_w_out[o])
        last = layer == depth - 1
        x = _ffn(x.reshape(n, d), ffn2_norm[layer], *ffn2_w, layer,
                 final_g=final_norm if last else None).reshape(b, t, d)
    return x
```

```python
import functools

import numpy as np
import jax
import jax.numpy as jnp
from jax import lax
from jax.experimental import pallas as pl
from jax.experimental.pallas import tpu as pltpu

F32 = jnp.float32
BF16 = jnp.bfloat16

D_MODEL = 1024
HEAD_DIM = 64
GROUP_WIDTH = D_MODEL // 2
CONV_WIDTH = 31
NSA_HEADS = GROUP_WIDTH // HEAD_DIM
NSA_KV_GROUPS = 2
NSA_REP = NSA_HEADS // NSA_KV_GROUPS
NSA_CMP_BLOCK = 32
NSA_CMP_STRIDE = 16
NSA_CMP_HIDDEN = 128
NSA_SEL_BLOCK = 64
NSA_TOP_N = 16
NSA_FORCED = 3
NSA_WINDOW = 512
NSA_N_BRANCH = 3
SC_WIDTH = 3
SB_HEADS = GROUP_WIDTH // HEAD_DIM
D_FF = 2816
RMS_EPS = 1e-6
LN_EPS = 1e-5
NEG_INF = -1e30
SEL_FORCE = 1e4
QK_SCALE = HEAD_DIM ** -0.5
LOG2_E = float(np.log2(np.e))

LANES = 128
SUBLANES = 8
VMEM_LIMIT = 48 * 1024 * 1024

FFN_TM = 1024
FFN_TF = 256
PROJ_TM = 512
CONV_HALO = 32
CONV_CHUNK = 64
SC_HALO = 8
NSA_TQ = 256
NSA_TK = 512
SB_T = 256
SB_PAIRS = 2
SB_DEAD_LOG2 = -160.0


def _params(*sem):
    return pltpu.CompilerParams(dimension_semantics=sem, vmem_limit_bytes=VMEM_LIMIT)


def _dot(a, b):
    return jnp.dot(a, b, preferred_element_type=F32)


def _dot_nt(a, b):
    return lax.dot_general(a, b, (((1,), (1,)), ((), ())), preferred_element_type=F32)


def _sigmoid(x):
    return 1.0 / (1.0 + jnp.exp(-x))


def _rmsnorm_rows(x, g):
    return x * lax.rsqrt(jnp.mean(x * x, axis=-1, keepdims=True) + RMS_EPS) * g


def _ffn_kernel(x_ref, g_ref, wi_ref, wo_ref, *rest, n_ff, mixer_out, final_norm):
    rest = list(rest)
    x = x_ref[...]
    if mixer_out:
        a_ref, m_ref, wm_ref = rest[:3]
        rest = rest[3:]
        half = a_ref.shape[1]
        x = x + _dot(a_ref[...], wm_ref[:half, :]) + _dot(m_ref[...], wm_ref[half:, :])
    if final_norm:
        fg_ref, o_ref = rest
    else:
        (o_ref,) = rest
    xn = _rmsnorm_rows(x, g_ref[...]).astype(BF16)
    acc = None
    for j in range(n_ff):
        gate = _dot(xn, wi_ref[:, j * FFN_TF:(j + 1) * FFN_TF])
        up = _dot(xn, wi_ref[:, D_FF + j * FFN_TF:D_FF + (j + 1) * FFN_TF])
        h = gate * _sigmoid(gate) * up
        part = _dot(h.astype(BF16), wo_ref[j * FFN_TF:(j + 1) * FFN_TF, :])
        acc = part if acc is None else acc + part
    y = x + 0.5 * acc
    if final_norm:
        y = _rmsnorm_rows(y, fg_ref[...])
    o_ref[...] = y


def _ffn(x, g, w_in, w_out, layer, mixer=None, final_g=None):
    n, d = x.shape
    n_ff = D_FF // FFN_TF
    tm = min(FFN_TM, n)
    once = pl.Buffered(1)
    in_specs = [
        pl.BlockSpec((tm, d), lambda i: (i, 0)),
        pl.BlockSpec((1, d), lambda i: (0, 0)),
        pl.BlockSpec((None, d, 2 * D_FF), lambda i: (layer, 0, 0), pipeline_mode=once),
        pl.BlockSpec((None, D_FF, d), lambda i: (layer, 0, 0), pipeline_mode=once),
    ]
    args = [x, g.reshape(1, d), w_in, w_out]
    if mixer is not None:
        a, o, w_mix = mixer
        in_specs += [pl.BlockSpec((tm, a.shape[1]), lambda i: (i, 0)), pl.BlockSpec((tm, o.shape[1]), lambda i: (i, 0)),
                     pl.BlockSpec((d, d), lambda i: (0, 0), pipeline_mode=once)]
        args += [a, o, w_mix]
    if final_g is not None:
        in_specs.append(pl.BlockSpec((1, d), lambda i: (0, 0)))
        args.append(final_g.reshape(1, d))
    return pl.pallas_call(
        functools.partial(_ffn_kernel, n_ff=n_ff, mixer_out=mixer is not None, final_norm=final_g is not None),
        out_shape=jax.ShapeDtypeStruct((n, d), F32),
        grid=(n // tm,),
        in_specs=in_specs,
        out_specs=pl.BlockSpec((tm, d), lambda i: (i, 0)),
        compiler_params=_params("parallel"),
        name="ffn",
    )(*args)


AB_A = 0
AB_Q = 1024
AB_KC = 2048
AB_VC = 2176
AB_KS = 2304
AB_VS = 2560
AB_KW = 2816
AB_VW = 3072
AB_G = 3328
AB_COLS = 3584

POS_SPLIT = 3
POS_TOK = HEAD_DIM
POS_CMP = POS_TOK + 2 * POS_SPLIT
MASK_BIG = 2.0 ** 100


def _pos_lanes(hi, lo, base):
    lane = lax.broadcasted_iota(jnp.int32, (1, LANES), 1)
    in_hi = (lane >= base) & (lane < base + POS_SPLIT)
    in_lo = (lane >= base + POS_SPLIT) & (lane < base + 2 * POS_SPLIT)
    return jnp.where(in_hi, hi, jnp.where(in_lo, lo, 0.0))


def _proj_ab_kernel(x_ref, g_ref, w_ref, qb_ref, dw_ref, db_ref, lg_ref, lb_ref,
                    a_ref, q_ref, kc_ref, vc_ref, ks_ref, vs_ref, kw_ref, vw_ref, gt_ref, ext_ref, shift_ref, *, tm):
    xn = _rmsnorm_rows(x_ref[0], g_ref[...]).astype(BF16)

    def seg(lo, hi):
        return _dot(xn, w_ref[:, lo:hi])

    @pl.when(pl.program_id(1) == 0)
    def _():
        ext_ref[0:CONV_HALO, :] = jnp.zeros((CONV_HALO, GROUP_WIDTH), F32)

    av = seg(AB_A, AB_A + 2 * GROUP_WIDTH)
    ext_ref[CONV_HALO:, :] = av[:, :GROUP_WIDTH] * _sigmoid(av[:, GROUP_WIDTH:])
    _conformer_conv_rows(ext_ref, shift_ref, dw_ref, db_ref, lg_ref, lb_ref, a_ref, tm)
    ext_ref[0:CONV_HALO, :] = ext_ref[tm:tm + CONV_HALO, :]
    q_ref[0] = (seg(AB_Q, AB_KC) + qb_ref[...]).astype(BF16)
    kc_ref[0] = seg(AB_KC, AB_VC)
    vc_ref[0] = seg(AB_VC, AB_KS)
    vs_ref[0] = seg(AB_VS, AB_KW).astype(BF16)
    vw_ref[0] = seg(AB_VW, AB_G).astype(BF16)
    gt_ref[0] = _sigmoid(seg(AB_G, AB_COLS))

    tok = pl.program_id(1) * tm + lax.broadcasted_iota(jnp.int32, (tm, 1), 0)
    blk = lax.shift_right_logical(tok, int(np.log2(NSA_SEL_BLOCK)))
    pos = _pos_lanes(blk.astype(F32), (tok & (NSA_SEL_BLOCK - 1)).astype(F32), POS_TOK)
    onehot = jnp.where(lax.broadcasted_iota(jnp.int32, (1, LANES), 1) == blk, 1.0, 0.0).astype(BF16)
    ks = seg(AB_KS, AB_VS)
    kw = seg(AB_KW, AB_VW)
    for g in range(NSA_KV_GROUPS):
        ks_ref[0, :, 2 * g * LANES:(2 * g + 1) * LANES] = (ks[:, g * LANES:(g + 1) * LANES] + pos).astype(BF16)
        ks_ref[0, :, (2 * g + 1) * LANES:(2 * g + 2) * LANES] = onehot
        kw_ref[0, :, g * LANES:(g + 1) * LANES] = (kw[:, g * LANES:(g + 1) * LANES] + pos).astype(BF16)


def _arrange_ab_weight(w):
    d = w.shape[0]
    kvw = NSA_KV_GROUPS * HEAD_DIM
    o = 2 * GROUP_WIDTH
    a = w[:, :o]
    q = w[:, o:o + GROUP_WIDTH].reshape(d, NSA_HEADS, HEAD_DIM) * (QK_SCALE * LOG2_E)
    o += GROUP_WIDTH
    kc, vc, ks, vs, kw, vw = [w[:, o + i * kvw:o + (i + 1) * kvw] for i in range(6)]
    o += 6 * kvw
    g = w[:, o:].reshape(d, NSA_KV_GROUPS, NSA_REP * NSA_N_BRANCH)

    zeros_h = jnp.zeros((d, NSA_HEADS, HEAD_DIM), w.dtype)
    q_pad = jnp.concatenate([q, zeros_h], axis=-1).reshape(d, NSA_HEADS * LANES)

    def k_pad(k):
        k = k.reshape(d, NSA_KV_GROUPS, HEAD_DIM)
        return jnp.concatenate([k, jnp.zeros_like(k)], axis=-1).reshape(d, NSA_KV_GROUPS * LANES)

    def v_dup(v):
        v = v.reshape(d, NSA_KV_GROUPS, HEAD_DIM)
        return jnp.concatenate([v, v], axis=-1).reshape(d, NSA_KV_GROUPS * LANES)

    g_pad = jnp.pad(g, ((0, 0), (0, 0), (0, LANES - g.shape[-1]))).reshape(d, NSA_KV_GROUPS * LANES)
    out = jnp.concatenate([a, q_pad, kc, vc, k_pad(ks), v_dup(vs), k_pad(kw), v_dup(vw), g_pad], axis=-1)
    assert out.shape[1] == AB_COLS
    return out.astype(BF16)


def _bf16_terms(x, n):
    terms, rest = [], np.asarray(x, np.float64)
    for _ in range(n):
        term = rest.astype(BF16).astype(np.float64)
        terms.append(term)
        rest = rest - term
    return terms


def _alibi_query_lanes():
    row = np.zeros((NSA_HEADS, LANES), np.float64)
    for h in range(NSA_HEADS):
        slope = 2.0 ** (-8.0 * (h + 1) / NSA_HEADS)
        for i, term in enumerate(_bf16_terms(slope * np.log2(np.e), POS_SPLIT)):
            row[h, POS_TOK + i] = NSA_SEL_BLOCK * term
            row[h, POS_TOK + POS_SPLIT + i] = term
            row[h, POS_CMP + i] = 2 * NSA_CMP_STRIDE * term
            row[h, POS_CMP + POS_SPLIT + i] = NSA_CMP_STRIDE * term
    return jnp.asarray(row.reshape(1, NSA_HEADS * LANES), F32)


def _proj_ab(x, g, w, dw_w, dw_b, ln_g, ln_b):
    b, t, d = x.shape
    tm = min(PROJ_TM, t)
    wa = _arrange_ab_weight(w)

    def tok(width):
        return pl.BlockSpec((1, tm, width), lambda bi, i: (bi, i, 0))

    const = lambda width: pl.BlockSpec((1, width), lambda bi, i: (0, 0))
    once = pl.Buffered(1)
    gw = GROUP_WIDTH
    kv = NSA_KV_GROUPS * LANES
    widths = [gw, NSA_HEADS * LANES, LANES, LANES, 2 * kv, kv, kv, kv, kv]
    dtypes = [BF16, BF16, F32, F32, BF16, BF16, BF16, BF16, F32]
    return pl.pallas_call(
        functools.partial(_proj_ab_kernel, tm=tm),
        out_shape=[jax.ShapeDtypeStruct((b, t, wd), dt) for wd, dt in zip(widths, dtypes)],
        grid=(b, t // tm),
        in_specs=[tok(d), const(d), pl.BlockSpec((d, AB_COLS), lambda bi, i: (0, 0), pipeline_mode=once),
                  const(NSA_HEADS * LANES), pl.BlockSpec((CONV_WIDTH, gw), lambda bi, i: (0, 0)),
                  const(gw), const(gw), const(gw)],
        out_specs=[tok(wd) for wd in widths],
        scratch_shapes=[pltpu.VMEM((tm + CONV_HALO, gw), F32), pltpu.VMEM((SUBLANES, tm + CONV_HALO, gw), F32)],
        compiler_params=_params("arbitrary", "arbitrary"),
        name="proj_ab",
    )(x, g.reshape(1, d), wa, _alibi_query_lanes(), dw_w, dw_b.reshape(1, gw), ln_g.reshape(1, gw),
      ln_b.reshape(1, gw))


CD_B = 0
CD_C = 512
CD_U = 1024
CD_Q = 1536
CD_K = 2560
CD_V = 3072
CD_COLS = 3584


def _proj_cd_kernel(x_ref, g_ref, w_ref, scw_ref, c_ref, q_ref, k_ref, v_ref, ext_ref, *, tm):
    xn = _rmsnorm_rows(x_ref[0], g_ref[...]).astype(BF16)

    def seg(lo, hi):
        return _dot(xn, w_ref[:, lo:hi])

    q_ref[0] = seg(CD_Q, CD_K).astype(BF16)
    k_ref[0] = seg(CD_K, CD_V).astype(BF16)
    v_ref[0] = seg(CD_V, CD_COLS).astype(BF16)

    @pl.when(pl.program_id(1) == 0)
    def _():
        ext_ref[0:SC_HALO, :] = jnp.zeros((SC_HALO, GROUP_WIDTH), F32)

    ext_ref[SC_HALO:, :] = seg(CD_C, CD_U) * seg(CD_U, CD_Q)
    off = SC_HALO - (SC_WIDTH - 1)
    acc = jnp.zeros((tm, GROUP_WIDTH), F32)
    for k in range(SC_WIDTH):
        acc = acc + scw_ref[k:k + 1, :] * ext_ref[off + k:off + k + tm, :]
    c_ref[0] = (seg(CD_B, CD_C) * acc).astype(c_ref.dtype)
    ext_ref[0:SC_HALO, :] = ext_ref[tm:tm + SC_HALO, :]


def _arrange_cd_weight(w):
    d = w.shape[0]
    gw = GROUP_WIDTH
    q = w[:, 3 * gw:4 * gw].reshape(d, SB_HEADS // 2, 2, HEAD_DIM) * (QK_SCALE * LOG2_E)
    z = jnp.zeros((d, SB_HEADS // 2, HEAD_DIM), w.dtype)
    q_even = jnp.concatenate([q[:, :, 0], z], axis=-1)
    q_odd = jnp.concatenate([z, q[:, :, 1]], axis=-1)
    q_pad = jnp.stack([q_even, q_odd], axis=2).reshape(d, SB_HEADS * LANES)
    out = jnp.concatenate([w[:, :3 * gw], q_pad, w[:, 4 * gw:]], axis=-1)
    assert out.shape[1] == CD_COLS
    return out.astype(BF16)


def _proj_cd(x, g, w, sc_w):
    b, t, d = x.shape
    tm = min(PROJ_TM, t)
    wa = _arrange_cd_weight(w)

    def tok(width):
        return pl.BlockSpec((1, tm, width), lambda bi, i: (bi, i, 0))

    widths = [GROUP_WIDTH, SB_HEADS * LANES, GROUP_WIDTH, GROUP_WIDTH]
    return pl.pallas_call(
        functools.partial(_proj_cd_kernel, tm=tm),
        out_shape=[jax.ShapeDtypeStruct((b, t, wd), BF16) for wd in widths],
        grid=(b, t // tm),
        in_specs=[tok(d), pl.BlockSpec((1, d), lambda bi, i: (0, 0)),
                  pl.BlockSpec((d, CD_COLS), lambda bi, i: (0, 0)),
                  pl.BlockSpec((SC_WIDTH, GROUP_WIDTH), lambda bi, i: (0, 0))],
        out_specs=[tok(wd) for wd in widths],
        scratch_shapes=[pltpu.VMEM((tm + SC_HALO, GROUP_WIDTH), F32)],
        compiler_params=_params("arbitrary", "arbitrary"),
        name="proj_cd",
    )(x, g.reshape(1, d), wa, sc_w)


def _conformer_conv_rows(ext_ref, shift_ref, w_ref, b_ref, lg_ref, lb_ref, o_ref, tt):
    off = CONV_HALO - (CONV_WIDTH - 1)
    last = off + CONV_WIDTH - 1
    steps = {phase: [m for m in range((last - phase) // SUBLANES + 1) if off <= SUBLANES * m + phase]
             for phase in range(SUBLANES)}
    for phase in range(SUBLANES):
        rows = SUBLANES * steps[phase][-1] + tt
        shift_ref[phase, 0:rows, :] = ext_ref[phase:phase + rows, :]
    for c in range(tt // CONV_CHUNK):
        r0 = c * CONV_CHUNK
        acc = jnp.zeros((CONV_CHUNK, GROUP_WIDTH), F32)
        for phase in range(SUBLANES):
            for m in steps[phase]:
                k = SUBLANES * m + phase - off
                lo = r0 + SUBLANES * m
                acc = acc + w_ref[k:k + 1, :] * shift_ref[phase, lo:lo + CONV_CHUNK, :]
        y = acc + b_ref[...]
        mu = jnp.mean(y, axis=-1, keepdims=True)
        yc = y - mu
        var = jnp.mean(yc * yc, axis=-1, keepdims=True)
        yn = yc * lax.rsqrt(var + LN_EPS) * lg_ref[...] + lb_ref[...]
        o_ref[0, r0:r0 + CONV_CHUNK, :] = (yn * _sigmoid(yn)).astype(o_ref.dtype)


def _gelu_tanh(x):
    return 0.5 * x * (1.0 + jnp.tanh(np.sqrt(2.0 / np.pi).astype(np.float32) * (x + 0.044715 * (x * x * x))))


def _compress_kernel(x_ref, pe_ref, w1_ref, w2_ref, o_ref, *, nchunk, keys):
    half = NSA_CMP_STRIDE * HEAD_DIM
    x = x_ref[0, 0]
    ha = _dot((x + pe_ref[:, :half]).astype(BF16), w1_ref[:half, :])
    hb = _dot((x + pe_ref[:, half:]).astype(BF16), w1_ref[half:, :])
    h = ha + pltpu.roll(hb, nchunk - 1, 0)
    y = _dot(_gelu_tanh(h).astype(BF16), w2_ref[...])
    row = lax.broadcasted_iota(jnp.int32, (nchunk, 1), 0)
    y = jnp.where(row < nchunk - 1, y, 0.0)
    if keys:
        y = y + _pos_lanes(lax.shift_right_logical(row, 1).astype(F32), (row & 1).astype(F32), POS_CMP)
    o_ref[0, 0] = y


def _compress(kx, pe, w1, w2, dup):
    b, t, _ = kx.shape
    g = NSA_KV_GROUPS
    nchunk = t // NSA_CMP_STRIDE
    feat = NSA_CMP_STRIDE * HEAD_DIM
    x = kx.reshape(b, nchunk, NSA_CMP_STRIDE, g, HEAD_DIM).transpose(0, 3, 1, 2, 4).reshape(b, g, nchunk, feat)
    w2p = jnp.concatenate([w2, w2 if dup else jnp.zeros_like(w2)], axis=-1).astype(BF16)
    y = pl.pallas_call(
        functools.partial(_compress_kernel, nchunk=nchunk, keys=not dup),
        out_shape=jax.ShapeDtypeStruct((b, g, nchunk, LANES), F32),
        grid=(b, g),
        in_specs=[pl.BlockSpec((1, 1, nchunk, feat), lambda bi, gi: (bi, gi, 0, 0)),
                  pl.BlockSpec((1, 2 * feat), lambda bi, gi: (0, 0)),
                  pl.BlockSpec((2 * feat, NSA_CMP_HIDDEN), lambda bi, gi: (0, 0)),
                  pl.BlockSpec((NSA_CMP_HIDDEN, LANES), lambda bi, gi: (0, 0))],
        out_specs=pl.BlockSpec((1, 1, nchunk, LANES), lambda bi, gi: (bi, gi, 0, 0)),
        compiler_params=_params("parallel", "parallel"),
        name="nsa_compress",
    )(x, pe.reshape(1, 2 * feat), w1.reshape(2 * feat, NSA_CMP_HIDDEN).astype(BF16), w2p)
    ratio = NSA_SEL_BLOCK // NSA_CMP_STRIDE
    return y.reshape(b, g, nchunk // ratio, ratio, LANES).transpose(0, 1, 3, 2, 4).astype(BF16)


def _nsa_kernel(q_ref, kc_ref, vc_ref, ks_ref, vs_ref, kw_ref, vw_ref, gt_ref, o_ref, *, t_len, ns):
    tq, tk, rep = NSA_TQ, NSA_TK, NSA_REP
    rows = rep * tq
    i = pl.program_id(2)
    q0 = i * tq

    q2 = q_ref[0]
    qg = jnp.concatenate([q2[:, r * LANES:(r + 1) * LANES] for r in range(rep)], axis=0)
    tq_i = q0 + lax.broadcasted_iota(jnp.int32, (tq, 1), 0)

    def per_head(x):
        return jnp.concatenate([x] * rep, axis=0)

    lane_i = lax.broadcasted_iota(jnp.int32, (1, ns), 1)
    ratio = NSA_SEL_BLOCK // NSA_CMP_STRIDE
    n_cmp = t_len // NSA_CMP_STRIDE - 1
    s_list = []
    for r in range(ratio):
        c_i = lane_i * ratio + r
        mask = ((c_i * NSA_CMP_STRIDE + (NSA_CMP_BLOCK - 1)) <= tq_i) & (c_i < n_cmp)
        s_list.append(_dot_nt(qg, kc_ref[0, 0, r]) + per_head(jnp.where(mask, 0.0, NEG_INF)))
    m = functools.reduce(jnp.maximum, [jnp.max(s, axis=-1, keepdims=True) for s in s_list])
    p_list = [jnp.exp2(s - m) for s in s_list]
    l = functools.reduce(lambda a, b: a + b, [jnp.sum(p, axis=-1, keepdims=True) for p in p_list])
    any_visible = per_head(tq_i >= NSA_CMP_BLOCK - 1)
    inv = jnp.where(any_visible, 1.0 / l, 0.0)
    p_list = [p * inv for p in p_list]
    o_c = functools.reduce(lambda a, b: a + b,
                           [_dot(p.astype(BF16), vc_ref[0, 0, r]) for r, p in enumerate(p_list)])

    def head_sum(p):
        return functools.reduce(lambda a, b: a + b, [p[r * tq:(r + 1) * tq] for r in range(rep)])

    ps = [head_sum(p) for p in p_list]
    lane_q = lax.broadcasted_iota(jnp.int32, (tq, ns), 1)
    prev_last = jnp.where(lane_q == 0, 0.0, pltpu.roll(ps[ratio - 1], 1, 1))
    imp = prev_last + ps[0] + ps[1] + ps[2] + ps[3]
    cur =lax.shift_right_logical(tq_i, int(np.log2(NSA_SEL_BLOCK)))
    visible = lane_q * NSA_SEL_BLOCK <= tq_i
    forced = (lane_q == 0) | (lane_q == cur) | (lane_q == cur - 1)
    score = jnp.where(visible, jnp.where(forced, SEL_FORCE, imp), -1.0)

    blk_f = lax.broadcasted_iota(jnp.int32, (ns, tq), 0).astype(F32)
    score_t = score.T
    taken = score_t == SEL_FORCE
    sel_t = jnp.where(taken, 1.0, 0.0)
    work = jnp.where(taken, -2.0, score_t)
    for _ in range(min(NSA_TOP_N, ns) - NSA_FORCED):
        top = jnp.max(work, axis=0, keepdims=True)
        idx = jnp.min(jnp.where(work == top, blk_f, float(ns)), axis=0, keepdims=True)
        hit = blk_f == idx
        sel_t = jnp.where(hit, 1.0, sel_t)
        work = jnp.where(hit, -2.0, work)
    sel = jnp.where(score >= 0.0, sel_t.T, 0.0)

    unsel = ((sel - 1.0) * MASK_BIG).astype(BF16)
    if ns < LANES:
        unsel = jnp.concatenate([unsel, jnp.zeros((tq, LANES - ns), BF16)], axis=1)
    q_aug = jnp.concatenate([qg, jnp.concatenate([unsel] * rep, axis=0)], axis=1)

    def sel_tile(k0, causal):
        s = _dot_nt(q_aug, ks_ref[0, pl.ds(k0, tk), :])
        if causal:
            tok = k0 + lax.broadcasted_iota(jnp.int32, (1, tk), 1)
            s = s + per_head(jnp.where(tok <= tq_i, 0.0, -MASK_BIG))
        return s

    blocks_per_tile = tk // NSA_SEL_BLOCK
    tile_of_blk = lax.shift_right_logical(lax.broadcasted_iota(jnp.int32, (ns, LANES), 0),
                                          int(np.log2(blocks_per_tile)))
    lane_t = lax.broadcasted_iota(jnp.int32, (ns, LANES), 1)
    per_tile = _dot(sel.astype(BF16), jnp.where(tile_of_blk == lane_t, 1.0, 0.0).astype(BF16))
    tile_any = jnp.max(per_tile, axis=0, keepdims=True) > 0.0
    pow2 = lax.shift_left(jnp.ones((1, LANES), jnp.int32), lane_t[:1] & 15).astype(F32)
    tile_bits = jnp.sum(jnp.where(tile_any, pow2, 0.0), axis=-1, keepdims=True).astype(jnp.int32)[0, 0]

    n_past = lax.shift_right_logical(q0, int(np.log2(tk)))
    kd = pl.multiple_of(n_past * tk, tk)
    s = sel_tile(kd, True)
    m_s = jnp.max(s, axis=-1, keepdims=True)
    p = jnp.exp2(s - m_s)
    l_s = jnp.sum(p, axis=-1, keepdims=True)
    acc_s = _dot(p.astype(BF16), vs_ref[0, pl.ds(kd, tk), :])

    def sel_update(kt, carry):
        m_run, l_run, acc = carry
        k0 = pl.multiple_of(kt * tk, tk)
        s = sel_tile(k0, False)
        m_new = jnp.maximum(m_run, jnp.max(s, axis=-1, keepdims=True))
        p = jnp.exp2(s - m_new)
        alpha = jnp.exp2(m_run - m_new)
        l_new = alpha * l_run + jnp.sum(p, axis=-1, keepdims=True)
        acc = alpha * acc + _dot(p.astype(BF16), vs_ref[0, pl.ds(k0, tk), :])
        return m_new, l_new, acc

    def next_active(state):
        return lax.while_loop(lambda st: (st[0] & 1) == 0,
                              lambda st: (lax.shift_right_logical(st[0], 1), st[1] + 1), state)

    def sel_body(state):
        pending, kt = next_active(state[:2])
        return (lax.shift_right_logical(pending, 1), kt + 1) + sel_update(kt, state[2:])

    past_bits = tile_bits & (lax.shift_left(jnp.int32(1), n_past) - 1)
    _, _, _, l_s, acc_s = lax.while_loop(lambda st: st[0] != 0, sel_body,
                                         (past_bits, jnp.int32(0), m_s, l_s, acc_s))
    o_s = acc_s / l_s

    wlen = NSA_WINDOW + tq
    w0 = pl.multiple_of(jnp.maximum(q0 - NSA_WINDOW, 0), tq)
    dist = tq_i - (w0 + lax.broadcasted_iota(jnp.int32, (1, wlen), 1))
    mask_w = (dist >= 0) & (dist < NSA_WINDOW)
    s_w = _dot_nt(qg, kw_ref[0, pl.ds(w0, wlen), :]) + per_head(jnp.where(mask_w, 0.0, NEG_INF))
    p_w = jnp.exp2(s_w - jnp.max(s_w, axis=-1, keepdims=True))
    l_w = jnp.sum(p_w, axis=-1, keepdims=True)
    o_w = _dot(p_w.astype(BF16), vw_ref[0, pl.ds(w0, wlen), :]) / l_w

    gt = gt_ref[0]

    def gate(branch):
        return jnp.concatenate(
            [gt[:, r * NSA_N_BRANCH + branch:r * NSA_N_BRANCH + branch + 1] for r in range(rep)], axis=0)

    o = gate(0) * o_c + gate(1) * o_s + gate(2) * o_w
    low_half = lax.broadcasted_iota(jnp.int32, (tq, LANES), 1) < HEAD_DIM
    for c in range(rep // 2):
        even = o[(2 * c) * tq:(2 * c + 1) * tq]
        odd = o[(2 * c + 1) * tq:(2 * c + 2) * tq]
        o_ref[0, :, c * LANES:(c + 1) * LANES] = jnp.where(low_half, even, odd).astype(o_ref.dtype)


def _nsa(q, kc, vc, ks, vs, kw, vw, gates):
    b, t, _ = q.shape
    g = NSA_KV_GROUPS
    ns = t // NSA_SEL_BLOCK
    assert ns <= LANES and t >= NSA_WINDOW + NSA_TQ and t % NSA_TK == 0 and NSA_TK % NSA_TQ == 0
    assert t // NSA_TK <= 16
    ratio = NSA_SEL_BLOCK // NSA_CMP_STRIDE
    qw = NSA_REP * LANES
    cmp_spec = pl.BlockSpec((1, 1, ratio, ns, LANES), lambda bi, gi, i: (bi, gi, 0, 0, 0))
    kv_spec = pl.BlockSpec((1, t, LANES), lambda bi, gi, i: (bi, 0, gi))
    ks_spec = pl.BlockSpec((1, t, 2 * LANES), lambda bi, gi, i: (bi, 0, gi))
    return pl.pallas_call(
        functools.partial(_nsa_kernel, t_len=t, ns=ns),
        out_shape=jax.ShapeDtypeStruct((b, t, GROUP_WIDTH), BF16),
        grid=(b, g, t // NSA_TQ),
        in_specs=[pl.BlockSpec((1, NSA_TQ, qw), lambda bi, gi, i: (bi, i, gi)),
                  cmp_spec, cmp_spec, ks_spec, kv_spec, kv_spec, kv_spec,
                  pl.BlockSpec((1, NSA_TQ, LANES), lambda bi, gi, i: (bi, i, gi))],
        out_specs=pl.BlockSpec((1, NSA_TQ, NSA_REP * HEAD_DIM), lambda bi, gi, i: (bi, i, gi)),
        compiler_params=_params("parallel", "parallel", "arbitrary"),
        name="nsa_attention",
    )(q, kc, vc, ks, vs, kw, vw, gates)


def _sb_tiles(qs, k, v, u2, carries, mask):
    stage1 = []
    for h, qh in enumerate(qs):
        p = h // 2
        z = _dot_nt(qh, k[:, p * LANES:(p + 1) * LANES])
        nk = jnp.maximum(z, 0.0) + jnp.log2(1.0 + jnp.exp2(-jnp.abs(z)))
        if mask is not None:
            nk = jnp.where(mask, nk, 0.0)
        hi = nk.astype(BF16)
        lo = (nk - hi.astype(F32)).astype(BF16)
        stage1.append((z, jnp.concatenate([hi, lo], axis=1)))
    laters = [_dot(hilo, u2) for (_, hilo) in stage1]
    out = []
    for h, ((z, _), later, (acc, c)) in enumerate(zip(stage1, laters, carries)):
        p = h // 2
        a = jnp.exp2(z + later)
        if mask is not None:
            a = jnp.where(mask, a, 0.0)
        acc = acc + jnp.exp2(c) * _dot(a.astype(BF16), v[:, p * LANES:(p + 1) * LANES])
        out.append((acc, c + later[:, 0:1]))
    return tuple(out)


def _sb_kernel(q_ref, k_ref, v_ref, u_ref, o_ref):
    tile = SB_T
    i = pl.program_id(2)
    q2 = q_ref[0]
    u = u_ref[...]
    qs = [q2[:, h * LANES:(h + 1) * LANES] for h in range(2 * SB_PAIRS)]
    q0 = pl.multiple_of(i * tile, tile)
    zero = (jnp.zeros((tile, LANES), F32), jnp.zeros((tile, 1), F32))
    mask = lax.broadcasted_iota(jnp.int32, (tile, tile), 1) < lax.broadcasted_iota(jnp.int32, (tile, tile), 0)
    carry = _sb_tiles(qs, k_ref[0, pl.ds(q0, tile), :], v_ref[0, pl.ds(q0, tile), :], u,
                      tuple(zero for _ in qs), mask)

    def body(jj, carry):
        k0 = pl.multiple_of((i - 1 - jj) * tile, tile)
        return _sb_tiles(qs, k_ref[0, pl.ds(k0, tile), :], v_ref[0, pl.ds(k0, tile), :], u, carry, None)

    def alive(carry):
        c_max = functools.reduce(jnp.maximum, [c for _, c in carry])
        return jnp.max(c_max) > SB_DEAD_LOG2

    def step(state):
        jj, _, carry = state
        carry = body(jj, carry)
        return jj + 1, alive(carry), carry

    _, _, carry = lax.while_loop(lambda st: (st[0] < i) & st[1], step, (jnp.int32(0), alive(carry), carry))
    low_half = lax.broadcasted_iota(jnp.int32, (tile, LANES), 1) < HEAD_DIM
    for p in range(SB_PAIRS):
        o_ref[0, :, p * LANES:(p + 1) * LANES] = jnp.where(
            low_half, carry[2 * p][0], carry[2 * p + 1][0]).astype(o_ref.dtype)


def _stick_breaking(q, k, v):
    b, t, _ = q.shape
    tile = SB_T
    assert t % tile == 0
    u = -(np.arange(tile)[:, None] >= np.arange(tile)[None, :]).astype(np.float32)
    u2 = np.concatenate([u, u], axis=0)
    kv_spec = pl.BlockSpec((1, t, SB_PAIRS * LANES), lambda bi, hp, i: (bi, 0, hp))
    return pl.pallas_call(
        _sb_kernel,
        out_shape=jax.ShapeDtypeStruct((b, t, GROUP_WIDTH), BF16),
        grid=(b, SB_HEADS // (2 * SB_PAIRS), t // tile),
        in_specs=[pl.BlockSpec((1, tile, 2 * SB_PAIRS * LANES), lambda bi, hp, i: (bi, i, hp)),
                  kv_spec, kv_spec,
                  pl.BlockSpec((2 * tile, tile), lambda bi, hp, i: (0, 0))],
        out_specs=pl.BlockSpec((1, tile, SB_PAIRS * LANES), lambda bi, hp, i: (bi, i, hp)),
        compiler_params=_params("parallel", "parallel", "arbitrary"),
        name="stick_breaking",
    )(q, k, v, jnp.asarray(u2, BF16))


def _mixer_conv_nsa(x, norm_g, w_in, dw_w, dw_b, ln_g, ln_b, pe_k, w1_k, w2_k, pe_v, w1_v, w2_v):
    a, q, kc, vc, ks, vs, kw, vw, gates = _proj_ab(x, norm_g, w_in, dw_w, dw_b, ln_g, ln_b)
    k_cmp = _compress(kc, pe_k, w1_k, w2_k, dup=False)
    v_cmp = _compress(vc, pe_v, w1_v, w2_v, dup=True)
    return a, _nsa(q, k_cmp, v_cmp, ks, vs, kw, vw, gates)


def _mixer_shortconv_sb(x, norm_g, w_in, sc_w):
    c, q, k, v = _proj_cd(x, norm_g, w_in, sc_w)
    return c, _stick_breaking(q, k, v)


def kernel(x, ffn1_norm, ffn1_w_in, ffn1_w_out, mix_norm, ffn2_norm, ffn2_w_in, ffn2_w_out, ab_w_in, conv_dw_w, conv_dw_b, conv_ln_g, conv_ln_b, nsa_pe_k, nsa_w1_k, nsa_w2_k, nsa_pe_v, nsa_w1_v, nsa_w2_v, ab_w_out, cd_w_in, sc_conv_w, cd_w_out, final_norm):
    b, t, d = x.shape
    depth = ffn1_norm.shape[0]
    n = b * t
    ffn1_w = (ffn1_w_in.astype(BF16), ffn1_w_out.astype(BF16))
    ffn2_w = (ffn2_w_in.astype(BF16), ffn2_w_out.astype(BF16))
    for layer in range(depth):
        x = _ffn(x.reshape(n, d), ffn1_norm[layer], *ffn1_w, layer).reshape(b, t, d)
        if layer % 2 == 0:
            e = layer // 2
            left, right = _mixer_conv_nsa(x, mix_norm[layer], ab_w_in[e], conv_dw_w[e], conv_dw_b[e], conv_ln_g[e],
                                          conv_ln_b[e], nsa_pe_k[e], nsa_w1_k[e], nsa_w2_k[e],
                                          nsa_pe_v[e], nsa_w1_v[e], nsa_w2_v[e])
            w_mix = ab_w_out[e]
        else:
            o = layer // 2
            left, right = _mixer_shortconv_sb(x, mix_norm[layer], cd_w_in[o], sc_conv_w[o])
            w_mix = cd_w_out[o]
        last = layer == depth - 1
        x = _ffn(x.reshape(n, d), ffn2_norm[layer], *ffn2_w, layer,
                 mixer=(left.reshape(n, -1), right.reshape(n, -1), w_mix.astype(BF16)),
                 final_g=final_norm if last else None).reshape(b, t, d)
    return x
```

```python
import functools

import numpy as np
import jax
import jax.numpy as jnp
from jax import lax
from jax.experimental import pallas as pl
from jax.experimental.pallas import tpu as pltpu

F32 = jnp.float32
BF16 = jnp.bfloat16

D_MODEL = 1024
HEAD_DIM = 64
GROUP_WIDTH = D_MODEL // 2
CONV_WIDTH = 31
NSA_HEADS = GROUP_WIDTH // HEAD_DIM
NSA_KV_GROUPS = 2
NSA_REP = NSA_HEADS // NSA_KV_GROUPS
NSA_CMP_BLOCK = 32
NSA_CMP_STRIDE = 16
NSA_CMP_HIDDEN = 128
NSA_SEL_BLOCK = 64
NSA_TOP_N = 16
NSA_FORCED = 3
NSA_WINDOW = 512
NSA_N_BRANCH = 3
SC_WIDTH = 3
SB_HEADS = GROUP_WIDTH // HEAD_DIM
D_FF = 2816
RMS_EPS = 1e-6
LN_EPS = 1e-5
NEG_INF = -1e30
SEL_FORCE = 1e4
QK_SCALE = HEAD_DIM ** -0.5
LOG2_E = float(np.log2(np.e))

LANES = 128
SUBLANES = 8
VMEM_LIMIT = 48 * 1024 * 1024

FFN_TM = 1024
FFN_TF = 256
PROJ_TM = 512
CONV_HALO = 32
CONV_CHUNK = 64
SC_HALO = 8
NSA_TQ = 256
NSA_TK = 512
SB_T = 256
SB_PAIRS = 2
SB_DEAD_LOG2 = -160.0


def _params(*sem):
    return pltpu.CompilerParams(dimension_semantics=sem, vmem_limit_bytes=VMEM_LIMIT)


def _dot(a, b):
    return jnp.dot(a, b, preferred_element_type=F32)


def _dot_nt(a, b):
    return lax.dot_general(a, b, (((1,), (1,)), ((), ())), preferred_element_type=F32)


def _sigmoid(x):
    return 1.0 / (1.0 + jnp.exp(-x))


def _rmsnorm_rows(x, g):
    return x * lax.rsqrt(jnp.mean(x * x, axis=-1, keepdims=True) + RMS_EPS) * g


def _ffn_kernel(x_ref, g_ref, wi_ref, wo_ref, *rest, n_ff, mixer_out, final_norm):
    rest = list(rest)
    x = x_ref[...]
    if mixer_out:
        a_ref, m_ref, wm_ref = rest[:3]
        rest = rest[3:]
        half = a_ref.shape[1]
        x = x + _dot(a_ref[...], wm_ref[:half, :]) + _dot(m_ref[...], wm_ref[half:, :])
    if final_norm:
        fg_ref, o_ref = rest
    else:
        (o_ref,) = rest
    xn = _rmsnorm_rows(x, g_ref[...]).astype(BF16)
    acc = None
    for j in range(n_ff):
        gate = _dot(xn, wi_ref[:, j * FFN_TF:(j + 1) * FFN_TF])
        up = _dot(xn, wi_ref[:, D_FF + j * FFN_TF:D_FF + (j + 1) * FFN_TF])
        h = gate * _sigmoid(gate) * up
        part = _dot(h.astype(BF16), wo_ref[j * FFN_TF:(j + 1) * FFN_TF, :])
        acc = part if acc is None else acc + part
    y = x + 0.5 * acc
    if final_norm:
        y = _rmsnorm_rows(y, fg_ref[...])
    o_ref[...] = y


def _ffn(x, g, w_in, w_out, layer, mixer=None, final_g=None):
    n, d = x.shape
    n_ff = D_FF // FFN_TF
    tm = min(FFN_TM, n)
    once = pl.Buffered(1)
    in_specs = [
        pl.BlockSpec((tm, d), lambda i: (i, 0)),
        pl.BlockSpec((1, d), lambda i: (0, 0)),
        pl.BlockSpec((None, d, 2 * D_FF), lambda i: (layer, 0, 0), pipeline_mode=once),
        pl.BlockSpec((None, D_FF, d), lambda i: (layer, 0, 0), pipeline_mode=once),
    ]
    args = [x, g.reshape(1, d), w_in, w_out]
    if mixer is not None:
        a, o, w_mix = mixer
        in_specs += [pl.BlockSpec((tm, a.shape[1]), lambda i: (i, 0)), pl.BlockSpec((tm, o.shape[1]), lambda i: (i, 0)),
                     pl.BlockSpec((d, d), lambda i: (0, 0), pipeline_mode=once)]
        args += [a, o, w_mix]
    if final_g is not None:
        in_specs.append(pl.BlockSpec((1, d), lambda i: (0, 0)))
        args.append(final_g.reshape(1, d))
    return pl.pallas_call(
        functools.partial(_ffn_kernel, n_ff=n_ff, mixer_out=mixer is not None, final_norm=final_g is not None),
        out_shape=jax.ShapeDtypeStruct((n, d), F32),
        grid=(n // tm,),
        in_specs=in_specs,
        out_specs=pl.BlockSpec((tm, d), lambda i: (i, 0)),
        compiler_params=_params("parallel"),
        name="ffn",
    )(*args)


AB_A = 0
AB_Q = 1024
AB_KC = 2048
AB_VC = 2176
AB_KS = 2304
AB_VS = 2560
AB_KW = 2816
AB_VW = 3072
AB_G = 3328
AB_COLS = 3584

POS_SPLIT = 3
POS_TOK = HEAD_DIM
POS_CMP = POS_TOK + 2 * POS_SPLIT
MASK_BIG = 2.0 ** 100


def _pos_lanes(hi, lo, base):
    lane = lax.broadcasted_iota(jnp.int32, (1, LANES), 1)
    in_hi = (lane >= base) & (lane < base + POS_SPLIT)
    in_lo = (lane >= base + POS_SPLIT) & (lane < base + 2 * POS_SPLIT)
    return jnp.where(in_hi, hi, jnp.where(in_lo, lo, 0.0))


def _proj_ab_kernel(x_ref, g_ref, w_ref, qb_ref, dw_ref, db_ref, lg_ref, lb_ref,
                    a_ref, q_ref, kc_ref, vc_ref, ks_ref, vs_ref, kw_ref, vw_ref, gt_ref, ext_ref, shift_ref, *, tm):
    xn = _rmsnorm_rows(x_ref[0], g_ref[...]).astype(BF16)

    def seg(lo, hi):
        return _dot(xn, w_ref[:, lo:hi])

    @pl.when(pl.program_id(1) == 0)
    def _():
        ext_ref[0:CONV_HALO, :] = jnp.zeros((CONV_HALO, GROUP_WIDTH), F32)

    av = seg(AB_A, AB_A + 2 * GROUP_WIDTH)
    ext_ref[CONV_HALO:, :] = av[:, :GROUP_WIDTH] * _sigmoid(av[:, GROUP_WIDTH:])
    _conformer_conv_rows(ext_ref, shift_ref, dw_ref, db_ref, lg_ref, lb_ref, a_ref, tm)
    ext_ref[0:CONV_HALO, :] = ext_ref[tm:tm + CONV_HALO, :]
    q_ref[0] = (seg(AB_Q, AB_KC) + qb_ref[...]).astype(BF16)
    kc_ref[0] = seg(AB_KC, AB_VC)
    vc_ref[0] = seg(AB_VC, AB_KS)
    vs_ref[0] = seg(AB_VS, AB_KW).astype(BF16)
    vw_ref[0] = seg(AB_VW, AB_G).astype(BF16)
    gt_ref[0] = _sigmoid(seg(AB_G, AB_COLS))

    tok = pl.program_id(1) * tm + lax.broadcasted_iota(jnp.int32, (tm, 1), 0)
    blk = lax.shift_right_logical(tok, int(np.log2(NSA_SEL_BLOCK)))
    pos = _pos_lanes(blk.astype(F32), (tok & (NSA_SEL_BLOCK - 1)).astype(F32), POS_TOK)
    onehot = jnp.where(lax.broadcasted_iota(jnp.int32, (1, LANES), 1) == blk, 1.0, 0.0).astype(BF16)
    ks = seg(AB_KS, AB_VS)
    kw = seg(AB_KW, AB_VW)
    for g in range(NSA_KV_GROUPS):
        ks_ref[0, :, 2 * g * LANES:(2 * g + 1) * LANES] = (ks[:, g * LANES:(g + 1) * LANES] + pos).astype(BF16)
        ks_ref[0, :, (2 * g + 1) * LANES:(2 * g + 2) * LANES] = onehot
        kw_ref[0, :, g * LANES:(g + 1) * LANES] = (kw[:, g * LANES:(g + 1) * LANES] + pos).astype(BF16)


def _arrange_ab_weight(w):
    d = w.shape[0]
    kvw = NSA_KV_GROUPS * HEAD_DIM
    o = 2 * GROUP_WIDTH
    a = w[:, :o]
    q = w[:, o:o + GROUP_WIDTH].reshape(d, NSA_HEADS, HEAD_DIM) * (QK_SCALE * LOG2_E)
    o += GROUP_WIDTH
    kc, vc, ks, vs, kw, vw = [w[:, o + i * kvw:o + (i + 1) * kvw] for i in range(6)]
    o += 6 * kvw
    g = w[:, o:].reshape(d, NSA_KV_GROUPS, NSA_REP * NSA_N_BRANCH)

    zeros_h = jnp.zeros((d, NSA_HEADS, HEAD_DIM), w.dtype)
    q_pad = jnp.concatenate([q, zeros_h], axis=-1).reshape(d, NSA_HEADS * LANES)

    def k_pad(k):
        k = k.reshape(d, NSA_KV_GROUPS, HEAD_DIM)
        return jnp.concatenate([k, jnp.zeros_like(k)], axis=-1).reshape(d, NSA_KV_GROUPS * LANES)

    def v_dup(v):
        v = v.reshape(d, NSA_KV_GROUPS, HEAD_DIM)
        return jnp.concatenate([v, v], axis=-1).reshape(d, NSA_KV_GROUPS * LANES)

    g_pad = jnp.pad(g, ((0, 0), (0, 0), (0, LANES - g.shape[-1]))).reshape(d, NSA_KV_GROUPS * LANES)
    out = jnp.concatenate([a, q_pad, kc, vc, k_pad(ks), v_dup(vs), k_pad(kw), v_dup(vw), g_pad], axis=-1)
    assert out.shape[1] == AB_COLS
    return out.astype(BF16)


def _bf16_terms(x, n):
    terms, rest = [], np.asarray(x, np.float64)
    for _ in range(n):
        term = rest.astype(BF16).astype(np.float64)
        terms.append(term)
        rest = rest - term
    return terms


def _alibi_query_lanes():
    row = np.zeros((NSA_HEADS, LANES), np.float64)
    for h in range(NSA_HEADS):
        slope = 2.0 ** (-8.0 * (h + 1) / NSA_HEADS)
        for i, term in enumerate(_bf16_terms(slope * np.log2(np.e), POS_SPLIT)):
            row[h, POS_TOK + i] = NSA_SEL_BLOCK * term
            row[h, POS_TOK + POS_SPLIT + i] = term
            row[h, POS_CMP + i] = 2 * NSA_CMP_STRIDE * term
            row[h, POS_CMP + POS_SPLIT + i] = NSA_CMP_STRIDE * term
    return jnp.asarray(row.reshape(1, NSA_HEADS * LANES), F32)


def _proj_ab(x, g, w, dw_w, dw_b, ln_g, ln_b):
    b, t, d = x.shape
    tm = min(PROJ_TM, t)
    wa = _arrange_ab_weight(w)

    def tok(width):
        return pl.BlockSpec((1, tm, width), lambda bi, i: (bi, i, 0))

    const = lambda width: pl.BlockSpec((1, width), lambda bi, i: (0, 0))
    once = pl.Buffered(1)
    gw = GROUP_WIDTH
    kv = NSA_KV_GROUPS * LANES
    widths = [gw, NSA_HEADS * LANES, LANES, LANES, 2 * kv, kv, kv, kv, kv]
    dtypes = [BF16, BF16, F32, F32, BF16, BF16, BF16, BF16, F32]
    return pl.pallas_call(
        functools.partial(_proj_ab_kernel, tm=tm),
        out_shape=[jax.ShapeDtypeStruct((b, t, wd), dt) for wd, dt in zip(widths, dtypes)],
        grid=(b, t // tm),
        in_specs=[tok(d), const(d), pl.BlockSpec((d, AB_COLS), lambda bi, i: (0, 0), pipeline_mode=once),
                  const(NSA_HEADS * LANES), pl.BlockSpec((CONV_WIDTH, gw), lambda bi, i: (0, 0)),
                  const(gw), const(gw), const(gw)],
        out_specs=[tok(wd) for wd in widths],
        scratch_shapes=[pltpu.VMEM((tm + CONV_HALO, gw), F32), pltpu.VMEM((SUBLANES, tm + CONV_HALO, gw), F32)],
        compiler_params=_params("arbitrary", "arbitrary"),
        name="proj_ab",
    )(x, g.reshape(1, d), wa, _alibi_query_lanes(), dw_w, dw_b.reshape(1, gw), ln_g.reshape(1, gw),
      ln_b.reshape(1, gw))


CD_B = 0
CD_C = 512
CD_U = 1024
CD_Q = 1536
CD_K = 2560
CD_V = 3072
CD_COLS = 3584


def _proj_cd_kernel(x_ref, g_ref, w_ref, scw_ref, c_ref, q_ref, k_ref, v_ref, ext_ref, *, tm):
    xn = _rmsnorm_rows(x_ref[0], g_ref[...]).astype(BF16)

    def seg(lo, hi):
        return _dot(xn, w_ref[:, lo:hi])

    q_ref[0] = seg(CD_Q, CD_K).astype(BF16)
    k_ref[0] = seg(CD_K, CD_V).astype(BF16)
    v_ref[0] = seg(CD_V, CD_COLS).astype(BF16)

    @pl.when(pl.program_id(1) == 0)
    def _():
        ext_ref[0:SC_HALO, :] = jnp.zeros((SC_HALO, GROUP_WIDTH), F32)

    ext_ref[SC_HALO:, :] = seg(CD_C, CD_U) * seg(CD_U, CD_Q)
    off = SC_HALO - (SC_WIDTH - 1)
    acc = jnp.zeros((tm, GROUP_WIDTH), F32)
    for k in range(SC_WIDTH):
        acc = acc + scw_ref[k:k + 1, :] * ext_ref[off + k:off + k + tm, :]
    c_ref[0] = (seg(CD_B, CD_C) * acc).astype(c_ref.dtype)
    ext_ref[0:SC_HALO, :] = ext_ref[tm:tm + SC_HALO, :]


def _arrange_cd_weight(w):
    d = w.shape[0]
    gw = GROUP_WIDTH
    q = w[:, 3 * gw:4 * gw].reshape(d, SB_HEADS // 2, 2, HEAD_DIM) * (QK_SCALE * LOG2_E)
    z = jnp.zeros((d, SB_HEADS // 2, HEAD_DIM), w.dtype)
    q_even = jnp.concatenate([q[:, :, 0], z], axis=-1)
    q_odd = jnp.concatenate([z, q[:, :, 1]], axis=-1)
    q_pad = jnp.stack([q_even, q_odd], axis=2).reshape(d, SB_HEADS * LANES)
    out = jnp.concatenate([w[:, :3 * gw], q_pad, w[:, 4 * gw:]], axis=-1)
    assert out.shape[1] == CD_COLS
    return out.astype(BF16)


def _proj_cd(x, g, w, sc_w):
    b, t, d = x.shape
    tm = min(PROJ_TM, t)
    wa = _arrange_cd_weight(w)

    def tok(width):
        return pl.BlockSpec((1, tm, width), lambda bi, i: (bi, i, 0))

    widths = [GROUP_WIDTH, SB_HEADS * LANES, GROUP_WIDTH, GROUP_WIDTH]
    return pl.pallas_call(
        functools.partial(_proj_cd_kernel, tm=tm),
        out_shape=[jax.ShapeDtypeStruct((b, t, wd), BF16) for wd in widths],
        grid=(b, t // tm),
        in_specs=[tok(d), pl.BlockSpec((1, d), lambda bi, i: (0, 0)),
                  pl.BlockSpec((d, CD_COLS), lambda bi, i: (0, 0)),
                  pl.BlockSpec((SC_WIDTH, GROUP_WIDTH), lambda bi, i: (0, 0))],
        out_specs=[tok(wd) for wd in widths],
        scratch_shapes=[pltpu.VMEM((tm + SC_HALO, GROUP_WIDTH), F32)],
        compiler_params=_params("arbitrary", "arbitrary"),
        name="proj_cd",
    )(x, g.reshape(1, d), wa, sc_w)


def _conformer_conv_rows(ext_ref, shift_ref, w_ref, b_ref, lg_ref, lb_ref, o_ref, tt):
    off = CONV_HALO - (CONV_WIDTH - 1)
    last = off + CONV_WIDTH - 1
    steps = {phase: [m for m in range((last - phase) // SUBLANES + 1) if off <= SUBLANES * m + phase]
             for phase in range(SUBLANES)}
    for phase in range(SUBLANES):
        rows = SUBLANES * steps[phase][-1] + tt
        shift_ref[phase, 0:rows, :] = ext_ref[phase:phase + rows, :]
    for c in range(tt // CONV_CHUNK):
        r0 = c * CONV_CHUNK
        acc = jnp.zeros((CONV_CHUNK, GROUP_WIDTH), F32)
        for phase in range(SUBLANES):
            for m in steps[phase]:
                k = SUBLANES * m + phase - off
                lo = r0 + SUBLANES * m
                acc = acc + w_ref[k:k + 1, :] * shift_ref[phase, lo:lo + CONV_CHUNK, :]
        y = acc + b_ref[...]
        mu = jnp.mean(y, axis=-1, keepdims=True)
        yc = y - mu
        var = jnp.mean(yc * yc, axis=-1, keepdims=True)
        yn = yc * lax.rsqrt(var + LN_EPS) * lg_ref[...] + lb_ref[...]
        o_ref[0, r0:r0 + CONV_CHUNK, :] = (yn * _sigmoid(yn)).astype(o_ref.dtype)


def _gelu_tanh(x):
    return 0.5 * x * (1.0 + jnp.tanh(np.sqrt(2.0 / np.pi).astype(np.float32) * (x + 0.044715 * (x * x * x))))


def _compress_kernel(x_ref, pe_ref, w1_ref, w2_ref, o_ref, *, nchunk, keys):
    half = NSA_CMP_STRIDE * HEAD_DIM
    x = x_ref[0, 0]
    ha = _dot((x + pe_ref[:, :half]).astype(BF16), w1_ref[:half, :])
    hb = _dot((x + pe_ref[:, half:]).astype(BF16), w1_ref[half:, :])
    h = ha + pltpu.roll(hb, nchunk - 1, 0)
    y = _dot(_gelu_tanh(h).astype(BF16), w2_ref[...])
    row = lax.broadcasted_iota(jnp.int32, (nchunk, 1), 0)
    y = jnp.where(row < nchunk - 1, y, 0.0)
    if keys:
        y = y + _pos_lanes(lax.shift_right_logical(row, 1).astype(F32), (row & 1).astype(F32), POS_CMP)
    o_ref[0, 0] = y


def _compress(kx, pe, w1, w2, dup):
    b, t, _ = kx.shape
    g = NSA_KV_GROUPS
    nchunk = t // NSA_CMP_STRIDE
    feat = NSA_CMP_STRIDE * HEAD_DIM
    x = kx.reshape(b, nchunk, NSA_CMP_STRIDE, g, HEAD_DIM).transpose(0, 3, 1, 2, 4).reshape(b, g, nchunk, feat)
    w2p = jnp.concatenate([w2, w2 if dup else jnp.zeros_like(w2)], axis=-1).astype(BF16)
    y = pl.pallas_call(
        functools.partial(_compress_kernel, nchunk=nchunk, keys=not dup),
        out_shape=jax.ShapeDtypeStruct((b, g, nchunk, LANES), F32),
        grid=(b, g),
        in_specs=[pl.BlockSpec((1, 1, nchunk, feat), lambda bi, gi: (bi, gi, 0, 0)),
                  pl.BlockSpec((1, 2 * feat), lambda bi, gi: (0, 0)),
                  pl.BlockSpec((2 * feat, NSA_CMP_HIDDEN), lambda bi, gi: (0, 0)),
                  pl.BlockSpec((NSA_CMP_HIDDEN, LANES), lambda bi, gi: (0, 0))],
        out_specs=pl.BlockSpec((1, 1, nchunk, LANES), lambda bi, gi: (bi, gi, 0, 0)),
        compiler_params=_params("parallel", "parallel"),
        name="nsa_compress",
    )(x, pe.reshape(1, 2 * feat), w1.reshape(2 * feat, NSA_CMP_HIDDEN).astype(BF16), w2p)
    ratio = NSA_SEL_BLOCK // NSA_CMP_STRIDE
    return y.reshape(b, g, nchunk // ratio, ratio, LANES).transpose(0, 1, 3, 2, 4).astype(BF16)


def _nsa_kernel(q_ref, kc_ref, vc_ref, ks_ref, vs_ref, kw_ref, vw_ref, gt_ref, o_ref, *, t_len, ns):
    tq, tk, rep = NSA_TQ, NSA_TK, NSA_REP
    rows = rep * tq
    i = pl.program_id(2)
    q0 = i * tq

    q2 = q_ref[0]
    qg = jnp.concatenate([q2[:, r * LANES:(r + 1) * LANES] for r in range(rep)], axis=0)
    tq_i = q0 + lax.broadcasted_iota(jnp.int32, (tq, 1), 0)

    def per_head(x):
        return jnp.concatenate([x] * rep, axis=0)

    lane_i = lax.broadcasted_iota(jnp.int32, (1, ns), 1)
    ratio = NSA_SEL_BLOCK // NSA_CMP_STRIDE
    n_cmp = t_len // NSA_CMP_STRIDE - 1
    s_list = []
    for r in range(ratio):
        c_i = lane_i * ratio + r
        mask = ((c_i * NSA_CMP_STRIDE + (NSA_CMP_BLOCK - 1)) <= tq_i) & (c_i < n_cmp)
        s_list.append(_dot_nt(qg, kc_ref[0, 0, r]) + per_head(jnp.where(mask, 0.0, NEG_INF)))
    m = jnp.max(functools.reduce(jnp.maximum, s_list), axis=-1, keepdims=True)
    p_list = [jnp.exp2(s - m) for s in s_list]
    l = jnp.sum(functools.reduce(lambda a, b: a + b, p_list), axis=-1, keepdims=True)
    any_visible = per_head(tq_i >= NSA_CMP_BLOCK - 1)
    inv = jnp.where(any_visible, 1.0 / l, 0.0)
    p_list = [p * inv for p in p_list]
    o_c = functools.reduce(lambda a, b: a + b,
                           [_dot(p.astype(BF16), vc_ref[0, 0, r]) for r, p in enumerate(p_list)])

    def head_sum(p):
        return functools.reduce(lambda a, b: a + b, [p[r * tq:(r + 1) * tq] for r in range(rep)])

    ps = [head_sum(p) for p in p_list]
    lane_q = lax.broadcasted_iota(jnp.int32, (tq, ns), 1)
    prev_last = jnp.where(lane_q == 0, 0.0, pltpu.roll(ps[ratio - 1], 1, 1))
    imp = prev_last + ps[0] + ps[1] + ps[2] + ps[3]
    cur =lax.shift_right_logical(tq_i, int(np.log2(NSA_SEL_BLOCK)))
    visible = lane_q * NSA_SEL_BLOCK <= tq_i
    forced = (lane_q == 0) | (lane_q == cur) | (lane_q == cur - 1)
    score = jnp.where(visible, jnp.where(forced, SEL_FORCE, imp), -1.0)

    blk_f = lax.broadcasted_iota(jnp.int32, (ns, tq), 0).astype(F32)
    score_t = score.T
    work = jnp.where(score_t == SEL_FORCE, -2.0, score_t)
    for _ in range(min(NSA_TOP_N, ns) - NSA_FORCED):
        top = jnp.max(work, axis=0, keepdims=True)
        idx = jnp.min(jnp.where(work == top, blk_f, float(ns)), axis=0, keepdims=True)
        work = jnp.where(blk_f == idx, -2.0, work)
    sel = jnp.where((score >= 0.0) & (work.T == -2.0), 1.0, 0.0)

    unsel = ((sel - 1.0) * MASK_BIG).astype(BF16)
    if ns < LANES:
        unsel = jnp.concatenate([unsel, jnp.zeros((tq, LANES - ns), BF16)], axis=1)
    q_aug = jnp.concatenate([qg, jnp.concatenate([unsel] * rep, axis=0)], axis=1)

    def sel_tile(k0, causal):
        s = _dot_nt(q_aug, ks_ref[0, pl.ds(k0, tk), :])
        if causal:
            tok = k0 + lax.broadcasted_iota(jnp.int32, (1, tk), 1)
            s = s + per_head(jnp.where(tok <= tq_i, 0.0, -MASK_BIG))
        return s

    blocks_per_tile = tk // NSA_SEL_BLOCK
    tile_of_blk = lax.shift_right_logical(lax.broadcasted_iota(jnp.int32, (ns, LANES), 0),
                                          int(np.log2(blocks_per_tile)))
    lane_t = lax.broadcasted_iota(jnp.int32, (ns, LANES), 1)
    per_tile = _dot(sel.astype(BF16), jnp.where(tile_of_blk == lane_t, 1.0, 0.0).astype(BF16))
    tile_any = jnp.max(per_tile, axis=0, keepdims=True) > 0.0
    pow2 = lax.shift_left(jnp.ones((1, LANES), jnp.int32), lane_t[:1] & 15).astype(F32)
    tile_bits = jnp.sum(jnp.where(tile_any, pow2, 0.0), axis=-1, keepdims=True).astype(jnp.int32)[0, 0]

    n_past = lax.shift_right_logical(q0, int(np.log2(tk)))
    kd = pl.multiple_of(n_past * tk, tk)
    s = sel_tile(kd, True)
    m_s = jnp.max(s, axis=-1, keepdims=True)
    p = jnp.exp2(s - m_s)
    l_s = jnp.sum(p, axis=-1, keepdims=True)
    acc_s = _dot(p.astype(BF16), vs_ref[0, pl.ds(kd, tk), :])

    def sel_update(kt, carry):
        m_run, l_run, acc = carry
        k0 = pl.multiple_of(kt * tk, tk)
        s = sel_tile(k0, False)
        m_new = jnp.maximum(m_run, jnp.max(s, axis=-1, keepdims=True))
        p = jnp.exp2(s - m_new)
        alpha = jnp.exp2(m_run - m_new)
        l_new = alpha * l_run + jnp.sum(p, axis=-1, keepdims=True)
        acc = alpha * acc + _dot(p.astype(BF16), vs_ref[0, pl.ds(k0, tk), :])
        return m_new, l_new, acc

    def next_active(state):
        return lax.while_loop(lambda st: (st[0] & 1) == 0,
                              lambda st: (lax.shift_right_logical(st[0], 1), st[1] + 1), state)

    def sel_body(state):
        pending, kt = next_active(state[:2])
        return (lax.shift_right_logical(pending, 1), kt + 1) + sel_update(kt, state[2:])

    past_bits = tile_bits & (lax.shift_left(jnp.int32(1), n_past) - 1)
    _, _, _, l_s, acc_s = lax.while_loop(lambda st: st[0] != 0, sel_body,
                                         (past_bits, jnp.int32(0), m_s, l_s, acc_s))
    o_s = acc_s / l_s

    wlen = NSA_WINDOW + tq
    w0 = pl.multiple_of(jnp.maximum(q0 - NSA_WINDOW, 0), tq)
    dist = tq_i - (w0 + lax.broadcasted_iota(jnp.int32, (1, wlen), 1))
    mask_w = (dist >= 0) & (dist < NSA_WINDOW)
    s_w = _dot_nt(qg, kw_ref[0, pl.ds(w0, wlen), :]) + per_head(jnp.where(mask_w, 0.0, NEG_INF))
    p_w = jnp.exp2(s_w - jnp.max(s_w, axis=-1, keepdims=True))
    l_w = jnp.sum(p_w, axis=-1, keepdims=True)
    o_w = _dot(p_w.astype(BF16), vw_ref[0, pl.ds(w0, wlen), :]) / l_w

    gt = gt_ref[0]

    def gate(branch):
        return jnp.concatenate(
            [gt[:, r * NSA_N_BRANCH + branch:r * NSA_N_BRANCH + branch + 1] for r in range(rep)], axis=0)

    o = gate(0) * o_c + gate(1) * o_s + gate(2) * o_w
    low_half = lax.broadcasted_iota(jnp.int32, (tq, LANES), 1) < HEAD_DIM
    for c in range(rep // 2):
        even = o[(2 * c) * tq:(2 * c + 1) * tq]
        odd = o[(2 * c + 1) * tq:(2 * c + 2) * tq]
        o_ref[0, :, c * LANES:(c + 1) * LANES] = jnp.where(low_half, even, odd).astype(o_ref.dtype)


def _nsa(q, kc, vc, ks, vs, kw, vw, gates):
    b, t, _ = q.shape
    g = NSA_KV_GROUPS
    ns = t // NSA_SEL_BLOCK
    assert ns <= LANES and t >= NSA_WINDOW + NSA_TQ and t % NSA_TK == 0 and NSA_TK % NSA_TQ == 0
    assert t // NSA_TK <= 16
    ratio = NSA_SEL_BLOCK // NSA_CMP_STRIDE
    qw = NSA_REP * LANES
    cmp_spec = pl.BlockSpec((1, 1, ratio, ns, LANES), lambda bi, gi, i: (bi, gi, 0, 0, 0))
    kv_spec = pl.BlockSpec((1, t, LANES), lambda bi, gi, i: (bi, 0, gi))
    ks_spec = pl.BlockSpec((1, t, 2 * LANES), lambda bi, gi, i: (bi, 0, gi))
    return pl.pallas_call(
        functools.partial(_nsa_kernel, t_len=t, ns=ns),
        out_shape=jax.ShapeDtypeStruct((b, t, GROUP_WIDTH), BF16),
        grid=(b, g, t // NSA_TQ),
        in_specs=[pl.BlockSpec((1, NSA_TQ, qw), lambda bi, gi, i: (bi, i, gi)),
                  cmp_spec, cmp_spec, ks_spec, kv_spec, kv_spec, kv_spec,
                  pl.BlockSpec((1, NSA_TQ, LANES), lambda bi, gi, i: (bi, i, gi))],
        out_specs=pl.BlockSpec((1, NSA_TQ, NSA_REP * HEAD_DIM), lambda bi, gi, i: (bi, i, gi)),
        compiler_params=_params("parallel", "parallel", "arbitrary"),
        name="nsa_attention",
    )(q, kc, vc, ks, vs, kw, vw, gates)


def _sb_tiles(qs, k, v, u2, carries, mask):
    stage1 = []
    for h, qh in enumerate(qs):
        p = h // 2
        z = _dot_nt(qh, k[:, p * LANES:(p + 1) * LANES])
        nk = jnp.maximum(z, 0.0) + jnp.log2(1.0 + jnp.exp2(-jnp.abs(z)))
        if mask is not None:
            nk = jnp.where(mask, nk, 0.0)
        hi = nk.astype(BF16)
        lo = (nk - hi.astype(F32)).astype(BF16)
        stage1.append((z, jnp.concatenate([hi, lo], axis=1)))
    laters = [_dot(hilo, u2) for (_, hilo) in stage1]
    out = []
    for h, ((z, _), later, (acc, c)) in enumerate(zip(stage1, laters, carries)):
        p = h // 2
        a = jnp.exp2(z + later)
        if mask is not None:
            a = jnp.where(mask, a, 0.0)
        acc = acc + jnp.exp2(c) * _dot(a.astype(BF16), v[:, p * LANES:(p + 1) * LANES])
        out.append((acc, c + later[:, 0:1]))
    return tuple(out)


def _sb_kernel(q_ref, k_ref, v_ref, u_ref, o_ref):
    tile = SB_T
    i = pl.program_id(2)
    q2 = q_ref[0]
    u = u_ref[...]
    qs = [q2[:, h * LANES:(h + 1) * LANES] for h in range(2 * SB_PAIRS)]
    q0 = pl.multiple_of(i * tile, tile)
    zero = (jnp.zeros((tile, LANES), F32), jnp.zeros((tile, 1), F32))
    mask = lax.broadcasted_iota(jnp.int32, (tile, tile), 1) < lax.broadcasted_iota(jnp.int32, (tile, tile), 0)
    carry = _sb_tiles(qs, k_ref[0, pl.ds(q0, tile), :], v_ref[0, pl.ds(q0, tile), :], u,
                      tuple(zero for _ in qs), mask)

    def body(jj, carry):
        k0 = pl.multiple_of((i - 1 - jj) * tile, tile)
        return _sb_tiles(qs, k_ref[0, pl.ds(k0, tile), :], v_ref[0, pl.ds(k0, tile), :], u, carry, None)

    def alive(carry):
        c_max = functools.reduce(jnp.maximum, [c for _, c in carry])
        return jnp.max(c_max) > SB_DEAD_LOG2

    def step(state):
        jj, _, carry = state
        carry = body(jj, carry)
        return jj + 1, alive(carry), carry

    _, _, carry = lax.while_loop(lambda st: (st[0] < i) & st[1], step, (jnp.int32(0), alive(carry), carry))
    low_half = lax.broadcasted_iota(jnp.int32, (tile, LANES), 1) < HEAD_DIM
    for p in range(SB_PAIRS):
        o_ref[0, :, p * LANES:(p + 1) * LANES] = jnp.where(
            low_half, carry[2 * p][0], carry[2 * p + 1][0]).astype(o_ref.dtype)


def _stick_breaking(q, k, v):
    b, t, _ = q.shape
    tile = SB_T
    assert t % tile == 0
    u = -(np.arange(tile)[:, None] >= np.arange(tile)[None, :]).astype(np.float32)
    u2 = np.concatenate([u, u], axis=0)
    kv_spec = pl.BlockSpec((1, t, SB_PAIRS * LANES), lambda bi, hp, i: (bi, 0, hp))
    return pl.pallas_call(
        _sb_kernel,
        out_shape=jax.ShapeDtypeStruct((b, t, GROUP_WIDTH), BF16),
        grid=(b, SB_HEADS // (2 * SB_PAIRS), t // tile),
        in_specs=[pl.BlockSpec((1, tile, 2 * SB_PAIRS * LANES), lambda bi, hp, i: (bi, i, hp)),
                  kv_spec, kv_spec,
                  pl.BlockSpec((2 * tile, tile), lambda bi, hp, i: (0, 0))],
        out_specs=pl.BlockSpec((1, tile, SB_PAIRS * LANES), lambda bi, hp, i: (bi, i, hp)),
        compiler_params=_params("parallel", "parallel", "arbitrary"),
        name="stick_breaking",
    )(q, k, v, jnp.asarray(u2, BF16))


def _mixer_conv_nsa(x, norm_g, w_in, dw_w, dw_b, ln_g, ln_b, pe_k, w1_k, w2_k, pe_v, w1_v, w2_v):
    a, q, kc, vc, ks, vs, kw, vw, gates = _proj_ab(x, norm_g, w_in, dw_w, dw_b, ln_g, ln_b)
    k_cmp = _compress(kc, pe_k, w1_k, w2_k, dup=False)
    v_cmp = _compress(vc, pe_v, w1_v, w2_v, dup=True)
    return a, _nsa(q, k_cmp, v_cmp, ks, vs, kw, vw, gates)


def _mixer_shortconv_sb(x, norm_g, w_in, sc_w):
    c, q, k, v = _proj_cd(x, norm_g, w_in, sc_w)
    return c, _stick_breaking(q, k, v)


def kernel(x, ffn1_norm, ffn1_w_in, ffn1_w_out, mix_norm, ffn2_norm, ffn2_w_in, ffn2_w_out, ab_w_in, conv_dw_w, conv_dw_b, conv_ln_g, conv_ln_b, nsa_pe_k, nsa_w1_k, nsa_w2_k, nsa_pe_v, nsa_w1_v, nsa_w2_v, ab_w_out, cd_w_in, sc_conv_w, cd_w_out, final_norm):
    b, t, d = x.shape
    depth = ffn1_norm.shape[0]
    n = b * t
    ffn1_w = (ffn1_w_in.astype(BF16), ffn1_w_out.astype(BF16))
    ffn2_w = (ffn2_w_in.astype(BF16), ffn2_w_out.astype(BF16))
    for layer in range(depth):
        x = _ffn(x.reshape(n, d), ffn1_norm[layer], *ffn1_w, layer).reshape(b, t, d)
        if layer % 2 == 0:
            e = layer // 2
            left, right = _mixer_conv_nsa(x, mix_norm[layer], ab_w_in[e], conv_dw_w[e], conv_dw_b[e], conv_ln_g[e],
                                          conv_ln_b[e], nsa_pe_k[e], nsa_w1_k[e], nsa_w2_k[e],
                                          nsa_pe_v[e], nsa_w1_v[e], nsa_w2_v[e])
            w_mix = ab_w_out[e]
        else:
            o = layer // 2
            left, right = _mixer_shortconv_sb(x, mix_norm[layer], cd_w_in[o], sc_conv_w[o])
            w_mix = cd_w_out[o]
        last = layer == depth - 1
        x = _ffn(x.reshape(n, d), ffn2_norm[layer], *ffn2_w, layer,
                 mixer=(left.reshape(n, -1), right.reshape(n, -1), w_mix.astype(BF16)),
                 final_g=final_norm if last else None).reshape(b, t, d)
    return x
```

```python
import functools

import numpy as np
import jax
import jax.numpy as jnp
from jax import lax
from jax.experimental import pallas as pl
from jax.experimental.pallas import tpu as pltpu

F32 = jnp.float32
BF16 = jnp.bfloat16

D_MODEL = 1024
HEAD_DIM = 64
GROUP_WIDTH = D_MODEL // 2
CONV_WIDTH = 31
NSA_HEADS = GROUP_WIDTH // HEAD_DIM
NSA_KV_GROUPS = 2
NSA_REP = NSA_HEADS // NSA_KV_GROUPS
NSA_CMP_BLOCK = 32
NSA_CMP_STRIDE = 16
NSA_CMP_HIDDEN = 128
NSA_SEL_BLOCK = 64
NSA_TOP_N = 16
NSA_FORCED = 3
NSA_WINDOW = 512
NSA_N_BRANCH = 3
SC_WIDTH = 3
SB_HEADS = GROUP_WIDTH // HEAD_DIM
D_FF = 2816
RMS_EPS = 1e-6
LN_EPS = 1e-5
NEG_INF = -1e30
SEL_FORCE = 1e4
QK_SCALE = HEAD_DIM ** -0.5
LOG2_E = float(np.log2(np.e))

LANES = 128
SUBLANES = 8
VMEM_LIMIT = 48 * 1024 * 1024

FFN_TM = 1024
FFN_TF = 256
PROJ_TM = 512
CONV_HALO = 32
CONV_CHUNK = 64
SC_HALO = 8
NSA_TQ = 256
NSA_TK = 512
SB_T = 256
SB_PAIRS = 2
SB_DEAD_LOG2 = -160.0


def _params(*sem):
    return pltpu.CompilerParams(dimension_semantics=sem, vmem_limit_bytes=VMEM_LIMIT)


def _dot(a, b):
    return jnp.dot(a, b, preferred_element_type=F32)


def _dot_nt(a, b):
    return lax.dot_general(a, b, (((1,), (1,)), ((), ())), preferred_element_type=F32)


def _sigmoid(x):
    return 1.0 / (1.0 + jnp.exp(-x))


def _rmsnorm_rows(x, g):
    return x * lax.rsqrt(jnp.mean(x * x, axis=-1, keepdims=True) + RMS_EPS) * g


def _ffn_kernel(x_ref, g_ref, wi_ref, wo_ref, *rest, n_ff, mixer_out, final_norm):
    rest = list(rest)
    x = x_ref[...]
    if mixer_out:
        a_ref, m_ref, wm_ref = rest[:3]
        rest = rest[3:]
        half = a_ref.shape[1]
        x = x + _dot(a_ref[...], wm_ref[:half, :]) + _dot(m_ref[...], wm_ref[half:, :])
    if final_norm:
        fg_ref, o_ref = rest
    else:
        (o_ref,) = rest
    xn = _rmsnorm_rows(x, g_ref[...]).astype(BF16)
    acc = None
    for j in range(n_ff):
        gate = _dot(xn, wi_ref[:, j * FFN_TF:(j + 1) * FFN_TF])
        up = _dot(xn, wi_ref[:, D_FF + j * FFN_TF:D_FF + (j + 1) * FFN_TF])
        h = gate * _sigmoid(gate) * up
        part = _dot(h.astype(BF16), wo_ref[j * FFN_TF:(j + 1) * FFN_TF, :])
        acc = part if acc is None else acc + part
    y = x + 0.5 * acc
    if final_norm:
        y = _rmsnorm_rows(y, fg_ref[...])
    o_ref[...] = y


def _ffn(x, g, w_in, w_out, layer, mixer=None, final_g=None):
    n, d = x.shape
    n_ff = D_FF // FFN_TF
    tm = min(FFN_TM, n)
    once = pl.Buffered(1)
    in_specs = [
        pl.BlockSpec((tm, d), lambda i: (i, 0)),
        pl.BlockSpec((1, d), lambda i: (0, 0)),
        pl.BlockSpec((None, d, 2 * D_FF), lambda i: (layer, 0, 0), pipeline_mode=once),
        pl.BlockSpec((None, D_FF, d), lambda i: (layer, 0, 0), pipeline_mode=once),
    ]
    args = [x, g.reshape(1, d), w_in, w_out]
    if mixer is not None:
        a, o, w_mix = mixer
        in_specs += [pl.BlockSpec((tm, a.shape[1]), lambda i: (i, 0)), pl.BlockSpec((tm, o.shape[1]), lambda i: (i, 0)),
                     pl.BlockSpec((d, d), lambda i: (0, 0), pipeline_mode=once)]
        args += [a, o, w_mix]
    if final_g is not None:
        in_specs.append(pl.BlockSpec((1, d), lambda i: (0, 0)))
        args.append(final_g.reshape(1, d))
    return pl.pallas_call(
        functools.partial(_ffn_kernel, n_ff=n_ff, mixer_out=mixer is not None, final_norm=final_g is not None),
        out_shape=jax.ShapeDtypeStruct((n, d), F32),
        grid=(n // tm,),
        in_specs=in_specs,
        out_specs=pl.BlockSpec((tm, d), lambda i: (i, 0)),
        compiler_params=_params("parallel"),
        name="ffn",
    )(*args)


AB_A = 0
AB_Q = 1024
AB_KC = 2048
AB_VC = 2176
AB_KS = 2304
AB_VS = 2560
AB_KW = 2816
AB_VW = 3072
AB_G = 3328
AB_COLS = 3584

POS_SPLIT = 3
POS_TOK = HEAD_DIM
POS_CMP = POS_TOK + 2 * POS_SPLIT
MASK_BIG = 2.0 ** 100


def _pos_lanes(hi, lo, base):
    lane = lax.broadcasted_iota(jnp.int32, (1, LANES), 1)
    in_hi = (lane >= base) & (lane < base + POS_SPLIT)
    in_lo = (lane >= base + POS_SPLIT) & (lane < base + 2 * POS_SPLIT)
    return jnp.where(in_hi, hi, jnp.where(in_lo, lo, 0.0))


def _proj_ab_kernel(x_ref, g_ref, w_ref, qb_ref, dw_ref, db_ref, lg_ref, lb_ref,
                    a_ref, q_ref, kc_ref, vc_ref, ks_ref, vs_ref, kw_ref, vw_ref, gt_ref, ext_ref, shift_ref, *, tm):
    xn = _rmsnorm_rows(x_ref[0], g_ref[...]).astype(BF16)

    def seg(lo, hi):
        return _dot(xn, w_ref[:, lo:hi])

    @pl.when(pl.program_id(1) == 0)
    def _():
        ext_ref[0:CONV_HALO, :] = jnp.zeros((CONV_HALO, GROUP_WIDTH), F32)

    av = seg(AB_A, AB_A + 2 * GROUP_WIDTH)
    ext_ref[CONV_HALO:, :] = av[:, :GROUP_WIDTH] * _sigmoid(av[:, GROUP_WIDTH:])
    _conformer_conv_rows(ext_ref, shift_ref, dw_ref, db_ref, lg_ref, lb_ref, a_ref, tm)
    ext_ref[0:CONV_HALO, :] = ext_ref[tm:tm + CONV_HALO, :]
    q_ref[0] = (seg(AB_Q, AB_KC) + qb_ref[...]).astype(BF16)
    kc_ref[0] = seg(AB_KC, AB_VC)
    vc_ref[0] = seg(AB_VC, AB_KS)
    vs_ref[0] = seg(AB_VS, AB_KW).astype(BF16)
    vw_ref[0] = seg(AB_VW, AB_G).astype(BF16)
    gt_ref[0] = _sigmoid(seg(AB_G, AB_COLS))

    tok = pl.program_id(1) * tm + lax.broadcasted_iota(jnp.int32, (tm, 1), 0)
    blk = lax.shift_right_logical(tok, int(np.log2(NSA_SEL_BLOCK)))
    pos = _pos_lanes(blk.astype(F32), (tok & (NSA_SEL_BLOCK - 1)).astype(F32), POS_TOK)
    onehot = jnp.where(lax.broadcasted_iota(jnp.int32, (1, LANES), 1) == blk, 1.0, 0.0).astype(BF16)
    ks = seg(AB_KS, AB_VS)
    kw = seg(AB_KW, AB_VW)
    for g in range(NSA_KV_GROUPS):
        ks_ref[0, :, 2 * g * LANES:(2 * g + 1) * LANES] = (ks[:, g * LANES:(g + 1) * LANES] + pos).astype(BF16)
        ks_ref[0, :, (2 * g + 1) * LANES:(2 * g + 2) * LANES] = onehot
        kw_ref[0, :, g * LANES:(g + 1) * LANES] = (kw[:, g * LANES:(g + 1) * LANES] + pos).astype(BF16)


def _arrange_ab_weight(w):
    d = w.shape[0]
    kvw = NSA_KV_GROUPS * HEAD_DIM
    o = 2 * GROUP_WIDTH
    a = w[:, :o]
    q = w[:, o:o + GROUP_WIDTH].reshape(d, NSA_HEADS, HEAD_DIM) * (QK_SCALE * LOG2_E)
    o += GROUP_WIDTH
    kc, vc, ks, vs, kw, vw = [w[:, o + i * kvw:o + (i + 1) * kvw] for i in range(6)]
    o += 6 * kvw
    g = w[:, o:].reshape(d, NSA_KV_GROUPS, NSA_REP * NSA_N_BRANCH)

    zeros_h = jnp.zeros((d, NSA_HEADS, HEAD_DIM), w.dtype)
    q_pad = jnp.concatenate([q, zeros_h], axis=-1).reshape(d, NSA_HEADS * LANES)

    def k_pad(k):
        k = k.reshape(d, NSA_KV_GROUPS, HEAD_DIM)
        return jnp.concatenate([k, jnp.zeros_like(k)], axis=-1).reshape(d, NSA_KV_GROUPS * LANES)

    def v_dup(v):
        v = v.reshape(d, NSA_KV_GROUPS, HEAD_DIM)
        return jnp.concatenate([v, v], axis=-1).reshape(d, NSA_KV_GROUPS * LANES)

    g_pad = jnp.pad(g, ((0, 0), (0, 0), (0, LANES - g.shape[-1]))).reshape(d, NSA_KV_GROUPS * LANES)
    out = jnp.concatenate([a, q_pad, kc, vc, k_pad(ks), v_dup(vs), k_pad(kw), v_dup(vw), g_pad], axis=-1)
    assert out.shape[1] == AB_COLS
    return out.astype(BF16)


def _bf16_terms(x, n):
    terms, rest = [], np.asarray(x, np.float64)
    for _ in range(n):
        term = rest.astype(BF16).astype(np.float64)
        terms.append(term)
        rest = rest - term
    return terms


def _alibi_query_lanes():
    row = np.zeros((NSA_HEADS, LANES), np.float64)
    for h in range(NSA_HEADS):
        slope = 2.0 ** (-8.0 * (h + 1) / NSA_HEADS)
        for i, term in enumerate(_bf16_terms(slope * np.log2(np.e), POS_SPLIT)):
            row[h, POS_TOK + i] = NSA_SEL_BLOCK * term
            row[h, POS_TOK + POS_SPLIT + i] = term
            row[h, POS_CMP + i] = 2 * NSA_CMP_STRIDE * term
            row[h, POS_CMP + POS_SPLIT + i] = NSA_CMP_STRIDE * term
    return jnp.asarray(row.reshape(1, NSA_HEADS * LANES), F32)


def _proj_ab(x, g, w, dw_w, dw_b, ln_g, ln_b):
    b, t, d = x.shape
    tm = min(PROJ_TM, t)
    wa = _arrange_ab_weight(w)

    def tok(width):
        return pl.BlockSpec((1, tm, width), lambda bi, i: (bi, i, 0))

    const = lambda width: pl.BlockSpec((1, width), lambda bi, i: (0, 0))
    once = pl.Buffered(1)
    gw = GROUP_WIDTH
    kv = NSA_KV_GROUPS * LANES
    widths = [gw, NSA_HEADS * LANES, LANES, LANES, 2 * kv, kv, kv, kv, kv]
    dtypes = [BF16, BF16, F32, F32, BF16, BF16, BF16, BF16, F32]
    return pl.pallas_call(
        functools.partial(_proj_ab_kernel, tm=tm),
        out_shape=[jax.ShapeDtypeStruct((b, t, wd), dt) for wd, dt in zip(widths, dtypes)],
        grid=(b, t // tm),
        in_specs=[tok(d), const(d), pl.BlockSpec((d, AB_COLS), lambda bi, i: (0, 0), pipeline_mode=once),
                  const(NSA_HEADS * LANES), pl.BlockSpec((CONV_WIDTH, gw), lambda bi, i: (0, 0)),
                  const(gw), const(gw), const(gw)],
        out_specs=[tok(wd) for wd in widths],
        scratch_shapes=[pltpu.VMEM((tm + CONV_HALO, gw), F32), pltpu.VMEM((SUBLANES, tm + CONV_HALO, gw), F32)],
        compiler_params=_params("arbitrary", "arbitrary"),
        name="proj_ab",
    )(x, g.reshape(1, d), wa, _alibi_query_lanes(), dw_w, dw_b.reshape(1, gw), ln_g.reshape(1, gw),
      ln_b.reshape(1, gw))


CD_B = 0
CD_C = 512
CD_U = 1024
CD_Q = 1536
CD_K = 2560
CD_V = 3072
CD_COLS = 3584


def _proj_cd_kernel(x_ref, g_ref, w_ref, scw_ref, c_ref, q_ref, k_ref, v_ref, ext_ref, *, tm):
    xn = _rmsnorm_rows(x_ref[0], g_ref[...]).astype(BF16)

    def seg(lo, hi):
        return _dot(xn, w_ref[:, lo:hi])

    q_ref[0] = seg(CD_Q, CD_K).astype(BF16)
    k_ref[0] = seg(CD_K, CD_V).astype(BF16)
    v_ref[0] = seg(CD_V, CD_COLS).astype(BF16)

    @pl.when(pl.program_id(1) == 0)
    def _():
        ext_ref[0:SC_HALO, :] = jnp.zeros((SC_HALO, GROUP_WIDTH), F32)

    ext_ref[SC_HALO:, :] = seg(CD_C, CD_U) * seg(CD_U, CD_Q)
    off = SC_HALO - (SC_WIDTH - 1)
    acc = jnp.zeros((tm, GROUP_WIDTH), F32)
    for k in range(SC_WIDTH):
        acc = acc + scw_ref[k:k + 1, :] * ext_ref[off + k:off + k + tm, :]
    c_ref[0] = (seg(CD_B, CD_C) * acc).astype(c_ref.dtype)
    ext_ref[0:SC_HALO, :] = ext_ref[tm:tm + SC_HALO, :]


def _arrange_cd_weight(w):
    d = w.shape[0]
    gw = GROUP_WIDTH
    q = w[:, 3 * gw:4 * gw].reshape(d, SB_HEADS // 2, 2, HEAD_DIM) * (QK_SCALE * LOG2_E)
    z = jnp.zeros((d, SB_HEADS // 2, HEAD_DIM), w.dtype)
    q_even = jnp.concatenate([q[:, :, 0], z], axis=-1)
    q_odd = jnp.concatenate([z, q[:, :, 1]], axis=-1)
    q_pad = jnp.stack([q_even, q_odd], axis=2).reshape(d, SB_HEADS * LANES)
    out = jnp.concatenate([w[:, :3 * gw], q_pad, w[:, 4 * gw:]], axis=-1)
    assert out.shape[1] == CD_COLS
    return out.astype(BF16)


def _proj_cd(x, g, w, sc_w):
    b, t, d = x.shape
    tm = min(PROJ_TM, t)
    wa = _arrange_cd_weight(w)

    def tok(width):
        return pl.BlockSpec((1, tm, width), lambda bi, i: (bi, i, 0))

    widths = [GROUP_WIDTH, SB_HEADS * LANES, GROUP_WIDTH, GROUP_WIDTH]
    return pl.pallas_call(
        functools.partial(_proj_cd_kernel, tm=tm),
        out_shape=[jax.ShapeDtypeStruct((b, t, wd), BF16) for wd in widths],
        grid=(b, t // tm),
        in_specs=[tok(d), pl.BlockSpec((1, d), lambda bi, i: (0, 0)),
                  pl.BlockSpec((d, CD_COLS), lambda bi, i: (0, 0)),
                  pl.BlockSpec((SC_WIDTH, GROUP_WIDTH), lambda bi, i: (0, 0))],
        out_specs=[tok(wd) for wd in widths],
        scratch_shapes=[pltpu.VMEM((tm + SC_HALO, GROUP_WIDTH), F32)],
        compiler_params=_params("arbitrary", "arbitrary"),
        name="proj_cd",
    )(x, g.reshape(1, d), wa, sc_w)


def _conformer_conv_rows(ext_ref, shift_ref, w_ref, b_ref, lg_ref, lb_ref, o_ref, tt):
    off = CONV_HALO - (CONV_WIDTH - 1)
    last = off + CONV_WIDTH - 1
    steps = {phase: [m for m in range((last - phase) // SUBLANES + 1) if off <= SUBLANES * m + phase]
             for phase in range(SUBLANES)}
    for phase in range(SUBLANES):
        rows = SUBLANES * steps[phase][-1] + tt
        shift_ref[phase, 0:rows, :] = ext_ref[phase:phase + rows, :]
    for c in range(tt // CONV_CHUNK):
        r0 = c * CONV_CHUNK
        acc = jnp.zeros((CONV_CHUNK, GROUP_WIDTH), F32)
        for phase in range(SUBLANES):
            for m in steps[phase]:
                k = SUBLANES * m + phase - off
                lo = r0 + SUBLANES * m
                acc = acc + w_ref[k:k + 1, :] * shift_ref[phase, lo:lo + CONV_CHUNK, :]
        y = acc + b_ref[...]
        mu = jnp.mean(y, axis=-1, keepdims=True)
        yc = y - mu
        var = jnp.mean(yc * yc, axis=-1, keepdims=True)
        yn = yc * lax.rsqrt(var + LN_EPS) * lg_ref[...] + lb_ref[...]
        o_ref[0, r0:r0 + CONV_CHUNK, :] = (yn * _sigmoid(yn)).astype(o_ref.dtype)


def _gelu_tanh(x):
    return 0.5 * x * (1.0 + jnp.tanh(np.sqrt(2.0 / np.pi).astype(np.float32) * (x + 0.044715 * (x * x * x))))


def _compress_kernel(x_ref, pe_ref, w1_ref, w2_ref, o_ref, y_ref, *, nblk, keys):
    stride = NSA_CMP_STRIDE

    def partial_sum(l0):
        acc = None
        for l in range(l0, l0 + stride):
            xl = x_ref[0, pl.ds(l - l0, nblk, stride=stride), :] + pe_ref[l:l + 1, :]
            part = _dot(xl.astype(BF16), w1_ref[l])
            acc = part if acc is None else acc + part
        return acc

    h = partial_sum(0) + pltpu.roll(partial_sum(stride), nblk - 1, 0)
    row = lax.broadcasted_iota(jnp.int32, (nblk, 1), 0)
    ratio = NSA_SEL_BLOCK // NSA_CMP_STRIDE
    for g in range(NSA_KV_GROUPS):
        y = _dot(_gelu_tanh(h[:, g * NSA_CMP_HIDDEN:(g + 1) * NSA_CMP_HIDDEN]).astype(BF16), w2_ref[...])
        y = jnp.where(row < nblk - 1, y, 0.0)
        if keys:
            y = y + _pos_lanes(lax.shift_right_logical(row, 1).astype(F32), (row & 1).astype(F32), POS_CMP)
        y_ref[...] = y
        for r in range(ratio):
            o_ref[0, g, r] = y_ref[pl.ds(r, nblk // ratio, stride=ratio), :].astype(o_ref.dtype)


def _compress(kx, pe, w1, w2, dup):
    b, t, width = kx.shape
    g = NSA_KV_GROUPS
    nblk = t // NSA_CMP_STRIDE
    ratio = NSA_SEL_BLOCK // NSA_CMP_STRIDE
    hid = NSA_CMP_HIDDEN
    zeros = jnp.zeros_like(w1)
    w1_bd = jnp.concatenate([jnp.concatenate([w1, zeros], axis=-1), jnp.concatenate([zeros, w1], axis=-1)],
                            axis=1).astype(BF16)
    w2p = jnp.concatenate([w2, w2 if dup else jnp.zeros_like(w2)], axis=-1).astype(BF16)
    return pl.pallas_call(
        functools.partial(_compress_kernel, nblk=nblk, keys=not dup),
        out_shape=jax.ShapeDtypeStruct((b, g, ratio, nblk // ratio, LANES), BF16),
        grid=(b,),
        in_specs=[pl.BlockSpec((1, t, width), lambda bi: (bi, 0, 0)),
                  pl.BlockSpec((NSA_CMP_BLOCK, width), lambda bi: (0, 0)),
                  pl.BlockSpec((NSA_CMP_BLOCK, width, g * hid), lambda bi: (0, 0, 0)),
                  pl.BlockSpec((hid, LANES), lambda bi: (0, 0))],
        out_specs=pl.BlockSpec((1, g, ratio, nblk // ratio, LANES), lambda bi: (bi, 0, 0, 0, 0)),
        scratch_shapes=[pltpu.VMEM((nblk, LANES), F32)],
        compiler_params=_params("parallel"),
        name="nsa_compress",
    )(kx, jnp.tile(pe, (1, g)), w1_bd, w2p)


def _nsa_kernel(q_ref, kc_ref, vc_ref, ks_ref, vs_ref, kw_ref, vw_ref, gt_ref, o_ref, *, t_len, ns):
    tq, tk, rep = NSA_TQ, NSA_TK, NSA_REP
    rows = rep * tq
    i = pl.program_id(2)
    q0 = i * tq

    q2 = q_ref[0]
    qg = jnp.concatenate([q2[:, r * LANES:(r + 1) * LANES] for r in range(rep)], axis=0)
    tq_i = q0 + lax.broadcasted_iota(jnp.int32, (tq, 1), 0)

    def per_head(x):
        return jnp.concatenate([x] * rep, axis=0)

    lane_i = lax.broadcasted_iota(jnp.int32, (1, ns), 1)
    ratio = NSA_SEL_BLOCK // NSA_CMP_STRIDE
    n_cmp = t_len // NSA_CMP_STRIDE - 1
    s_list = []
    for r in range(ratio):
        c_i = lane_i * ratio + r
        mask = ((c_i * NSA_CMP_STRIDE + (NSA_CMP_BLOCK - 1)) <= tq_i) & (c_i < n_cmp)
        s_list.append(_dot_nt(qg, kc_ref[0, 0, r]) + per_head(jnp.where(mask, 0.0, NEG_INF)))
    m = jnp.max(functools.reduce(jnp.maximum, s_list), axis=-1, keepdims=True)
    p_list = [jnp.exp2(s - m) for s in s_list]
    l = jnp.sum(functools.reduce(lambda a, b: a + b, p_list), axis=-1, keepdims=True)
    any_visible = per_head(tq_i >= NSA_CMP_BLOCK - 1)
    inv = jnp.where(any_visible, 1.0 / l, 0.0)
    p_list = [p * inv for p in p_list]
    o_c = functools.reduce(lambda a, b: a + b,
                           [_dot(p.astype(BF16), vc_ref[0, 0, r]) for r, p in enumerate(p_list)])

    def head_sum(p):
        return functools.reduce(lambda a, b: a + b, [p[r * tq:(r + 1) * tq] for r in range(rep)])

    ps = [head_sum(p) for p in p_list]
    lane_q = lax.broadcasted_iota(jnp.int32, (tq, ns), 1)
    prev_last = jnp.where(lane_q == 0, 0.0, pltpu.roll(ps[ratio - 1], 1, 1))
    imp = prev_last + ps[0] + ps[1] + ps[2] + ps[3]
    cur =lax.shift_right_logical(tq_i, int(np.log2(NSA_SEL_BLOCK)))
    visible = lane_q * NSA_SEL_BLOCK <= tq_i
    forced = (lane_q == 0) | (lane_q == cur) | (lane_q == cur - 1)
    score = jnp.where(visible, jnp.where(forced, SEL_FORCE, imp), -1.0)

    blk_f = lax.broadcasted_iota(jnp.int32, (ns, tq), 0).astype(F32)
    score_t = score.T
    work = jnp.where(score_t == SEL_FORCE, -2.0, score_t)
    for _ in range(min(NSA_TOP_N, ns) - NSA_FORCED):
        top = jnp.max(work, axis=0, keepdims=True)
        idx = jnp.min(jnp.where(work == top, blk_f, float(ns)), axis=0, keepdims=True)
        work = jnp.where(blk_f == idx, -2.0, work)
    sel = jnp.where((score >= 0.0) & (work.T == -2.0), 1.0, 0.0)

    unsel = ((sel - 1.0) * MASK_BIG).astype(BF16)
    if ns < LANES:
        unsel = jnp.concatenate([unsel, jnp.zeros((tq, LANES - ns), BF16)], axis=1)
    q_aug = jnp.concatenate([qg, jnp.concatenate([unsel] * rep, axis=0)], axis=1)

    def sel_tile(k0, causal):
        s = _dot_nt(q_aug, ks_ref[0, pl.ds(k0, tk), :])
        if causal:
            tok = k0 + lax.broadcasted_iota(jnp.int32, (1, tk), 1)
            s = s + per_head(jnp.where(tok <= tq_i, 0.0, -MASK_BIG))
        return s

    blocks_per_tile = tk // NSA_SEL_BLOCK
    tile_of_blk = lax.shift_right_logical(lax.broadcasted_iota(jnp.int32, (ns, LANES), 0),
                                          int(np.log2(blocks_per_tile)))
    lane_t = lax.broadcasted_iota(jnp.int32, (ns, LANES), 1)
    per_tile = _dot(sel.astype(BF16), jnp.where(tile_of_blk == lane_t, 1.0, 0.0).astype(BF16))
    tile_any = jnp.max(per_tile, axis=0, keepdims=True) > 0.0
    pow2 = lax.shift_left(jnp.ones((1, LANES), jnp.int32), lane_t[:1] & 15).astype(F32)
    tile_bits = jnp.sum(jnp.where(tile_any, pow2, 0.0), axis=-1, keepdims=True).astype(jnp.int32)[0, 0]

    n_past = lax.shift_right_logical(q0, int(np.log2(tk)))
    kd = pl.multiple_of(n_past * tk, tk)
    s = sel_tile(kd, True)
    m_s = jnp.max(s, axis=-1, keepdims=True)
    p = jnp.exp2(s - m_s)
    l_s = jnp.sum(p, axis=-1, keepdims=True)
    acc_s = _dot(p.astype(BF16), vs_ref[0, pl.ds(kd, tk), :])

    def sel_update(kt, carry):
        m_run, l_run, acc = carry
        k0 = pl.multiple_of(kt * tk, tk)
        s = sel_tile(k0, False)
        m_new = jnp.maximum(m_run, jnp.max(s, axis=-1, keepdims=True))
        p = jnp.exp2(s - m_new)
        alpha = jnp.exp2(m_run - m_new)
        l_new = alpha * l_run + jnp.sum(p, axis=-1, keepdims=True)
        acc = alpha * acc + _dot(p.astype(BF16), vs_ref[0, pl.ds(k0, tk), :])
        return m_new, l_new, acc

    def next_active(state):
        return lax.while_loop(lambda st: (st[0] & 1) == 0,
                              lambda st: (lax.shift_right_logical(st[0], 1), st[1] + 1), state)

    def sel_body(state):
        pending, kt = next_active(state[:2])
        return (lax.shift_right_logical(pending, 1), kt + 1) + sel_update(kt, state[2:])

    past_bits = tile_bits & (lax.shift_left(jnp.int32(1), n_past) - 1)
    _, _, _, l_s, acc_s = lax.while_loop(lambda st: st[0] != 0, sel_body,
                                         (past_bits, jnp.int32(0), m_s, l_s, acc_s))
    o_s = acc_s / l_s

    wlen = NSA_WINDOW + tq
    w0 = pl.multiple_of(jnp.maximum(q0 - NSA_WINDOW, 0), tq)
    dist = tq_i - (w0 + lax.broadcasted_iota(jnp.int32, (1, wlen), 1))
    mask_w = (dist >= 0) & (dist < NSA_WINDOW)
    s_w = _dot_nt(qg, kw_ref[0, pl.ds(w0, wlen), :]) + per_head(jnp.where(mask_w, 0.0, NEG_INF))
    p_w = jnp.exp2(s_w - jnp.max(s_w, axis=-1, keepdims=True))
    l_w = jnp.sum(p_w, axis=-1, keepdims=True)
    o_w = _dot(p_w.astype(BF16), vw_ref[0, pl.ds(w0, wlen), :]) / l_w

    gt = gt_ref[0]

    def gate(branch):
        return jnp.concatenate(
            [gt[:, r * NSA_N_BRANCH + branch:r * NSA_N_BRANCH + branch + 1] for r in range(rep)], axis=0)

    o = gate(0) * o_c + gate(1) * o_s + gate(2) * o_w
    low_half = lax.broadcasted_iota(jnp.int32, (tq, LANES), 1) < HEAD_DIM
    for c in range(rep // 2):
        even = o[(2 * c) * tq:(2 * c + 1) * tq]
        odd = o[(2 * c + 1) * tq:(2 * c + 2) * tq]
        o_ref[0, :, c * LANES:(c + 1) * LANES] = jnp.where(low_half, even, odd).astype(o_ref.dtype)


def _nsa(q, kc, vc, ks, vs, kw, vw, gates):
    b, t, _ = q.shape
    g = NSA_KV_GROUPS
    ns = t // NSA_SEL_BLOCK
    assert ns <= LANES and t >= NSA_WINDOW + NSA_TQ and t % NSA_TK == 0 and NSA_TK % NSA_TQ == 0
    assert t // NSA_TK <= 16
    ratio = NSA_SEL_BLOCK // NSA_CMP_STRIDE
    qw = NSA_REP * LANES
    cmp_spec = pl.BlockSpec((1, 1, ratio, ns, LANES), lambda bi, gi, i: (bi, gi, 0, 0, 0))
    kv_spec = pl.BlockSpec((1, t, LANES), lambda bi, gi, i: (bi, 0, gi))
    ks_spec = pl.BlockSpec((1, t, 2 * LANES), lambda bi, gi, i: (bi, 0, gi))
    return pl.pallas_call(
        functools.partial(_nsa_kernel, t_len=t, ns=ns),
        out_shape=jax.ShapeDtypeStruct((b, t, GROUP_WIDTH), BF16),
        grid=(b, g, t // NSA_TQ),
        in_specs=[pl.BlockSpec((1, NSA_TQ, qw), lambda bi, gi, i: (bi, i, gi)),
                  cmp_spec, cmp_spec, ks_spec, kv_spec, kv_spec, kv_spec,
                  pl.BlockSpec((1, NSA_TQ, LANES), lambda bi, gi, i: (bi, i, gi))],
        out_specs=pl.BlockSpec((1, NSA_TQ, NSA_REP * HEAD_DIM), lambda bi, gi, i: (bi, i, gi)),
        compiler_params=_params("parallel", "parallel", "arbitrary"),
        name="nsa_attention",
    )(q, kc, vc, ks, vs, kw, vw, gates)


def _sb_tiles(qs, k, v, u2, carries, mask):
    stage1 = []
    for h, qh in enumerate(qs):
        p = h // 2
        z = _dot_nt(qh, k[:, p * LANES:(p + 1) * LANES])
        nk = jnp.maximum(z, 0.0) + jnp.log2(1.0 + jnp.exp2(-jnp.abs(z)))
        if mask is not None:
            nk = jnp.where(mask, nk, 0.0)
        hi = nk.astype(BF16)
        lo = (nk - hi.astype(F32)).astype(BF16)
        stage1.append((z, jnp.concatenate([hi, lo], axis=1)))
    laters = [_dot(hilo, u2) for (_, hilo) in stage1]
    out = []
    for h, ((z, _), later, (acc, c)) in enumerate(zip(stage1, laters, carries)):
        p = h // 2
        a = jnp.exp2(z + later)
        if mask is not None:
            a = jnp.where(mask, a, 0.0)
        acc = acc + jnp.exp2(c) * _dot(a.astype(BF16), v[:, p * LANES:(p + 1) * LANES])
        out.append((acc, c + later[:, 0:1]))
    return tuple(out)


def _sb_kernel(q_ref, k_ref, v_ref, u_ref, o_ref):
    tile = SB_T
    i = pl.program_id(2)
    q2 = q_ref[0]
    u = u_ref[...]
    qs = [q2[:, h * LANES:(h + 1) * LANES] for h in range(2 * SB_PAIRS)]
    q0 = pl.multiple_of(i * tile, tile)
    zero = (jnp.zeros((tile, LANES), F32), jnp.zeros((tile, 1), F32))
    mask = lax.broadcasted_iota(jnp.int32, (tile, tile), 1) < lax.broadcasted_iota(jnp.int32, (tile, tile), 0)
    carry = _sb_tiles(qs, k_ref[0, pl.ds(q0, tile), :], v_ref[0, pl.ds(q0, tile), :], u,
                      tuple(zero for _ in qs), mask)

    def body(jj, carry):
        k0 = pl.multiple_of((i - 1 - jj) * tile, tile)
        return _sb_tiles(qs, k_ref[0, pl.ds(k0, tile), :], v_ref[0, pl.ds(k0, tile), :], u, carry, None)

    def alive(carry):
        c_max = functools.reduce(jnp.maximum, [c for _, c in carry])
        return jnp.max(c_max) > SB_DEAD_LOG2

    def step(state):
        jj, _, carry = state
        carry = body(jj, carry)
        return jj + 1, alive(carry), carry

    _, _, carry = lax.while_loop(lambda st: (st[0] < i) & st[1], step, (jnp.int32(0), alive(carry), carry))
    low_half = lax.broadcasted_iota(jnp.int32, (tile, LANES), 1) < HEAD_DIM
    for p in range(SB_PAIRS):
        o_ref[0, :, p * LANES:(p + 1) * LANES] = jnp.where(
            low_half, carry[2 * p][0], carry[2 * p + 1][0]).astype(o_ref.dtype)


def _stick_breaking(q, k, v):
    b, t, _ = q.shape
    tile = SB_T
    assert t % tile == 0
    u = -(np.arange(tile)[:, None] >= np.arange(tile)[None, :]).astype(np.float32)
    u2 = np.concatenate([u, u], axis=0)
    kv_spec = pl.BlockSpec((1, t, SB_PAIRS * LANES), lambda bi, hp, i: (bi, 0, hp))
    return pl.pallas_call(
        _sb_kernel,
        out_shape=jax.ShapeDtypeStruct((b, t, GROUP_WIDTH), BF16),
        grid=(b, SB_HEADS // (2 * SB_PAIRS), t // tile),
        in_specs=[pl.BlockSpec((1, tile, 2 * SB_PAIRS * LANES), lambda bi, hp, i: (bi, i, hp)),
                  kv_spec, kv_spec,
                  pl.BlockSpec((2 * tile, tile), lambda bi, hp, i: (0, 0))],
        out_specs=pl.BlockSpec((1, tile, SB_PAIRS * LANES), lambda bi, hp, i: (bi, i, hp)),
        compiler_params=_params("parallel", "parallel", "arbitrary"),
        name="stick_breaking",
    )(q, k, v, jnp.asarray(u2, BF16))


def _mixer_conv_nsa(x, norm_g, w_in, dw_w, dw_b, ln_g, ln_b, pe_k, w1_k, w2_k, pe_v, w1_v, w2_v):
    a, q, kc, vc, ks, vs, kw, vw, gates = _proj_ab(x, norm_g, w_in, dw_w, dw_b, ln_g, ln_b)
    k_cmp = _compress(kc, pe_k, w1_k, w2_k, dup=False)
    v_cmp = _compress(vc, pe_v, w1_v, w2_v, dup=True)
    return a, _nsa(q, k_cmp, v_cmp, ks, vs, kw, vw, gates)


def _mixer_shortconv_sb(x, norm_g, w_in, sc_w):
    c, q, k, v = _proj_cd(x, norm_g, w_in, sc_w)
    return c, _stick_breaking(q, k, v)


def kernel(x, ffn1_norm, ffn1_w_in, ffn1_w_out, mix_norm, ffn2_norm, ffn2_w_in, ffn2_w_out, ab_w_in, conv_dw_w, conv_dw_b, conv_ln_g, conv_ln_b, nsa_pe_k, nsa_w1_k, nsa_w2_k, nsa_pe_v, nsa_w1_v, nsa_w2_v, ab_w_out, cd_w_in, sc_conv_w, cd_w_out, final_norm):
    b, t, d = x.shape
    depth = ffn1_norm.shape[0]
    n = b * t
    ffn1_w = (ffn1_w_in.astype(BF16), ffn1_w_out.astype(BF16))
    ffn2_w = (ffn2_w_in.astype(BF16), ffn2_w_out.astype(BF16))
    for layer in range(depth):
        x = _ffn(x.reshape(n, d), ffn1_norm[layer], *ffn1_w, layer).reshape(b, t, d)
        if layer % 2 == 0:
            e = layer // 2
            left, right = _mixer_conv_nsa(x, mix_norm[layer], ab_w_in[e], conv_dw_w[e], conv_dw_b[e], conv_ln_g[e],
                                          conv_ln_b[e], nsa_pe_k[e], nsa_w1_k[e], nsa_w2_k[e],
                                          nsa_pe_v[e], nsa_w1_v[e], nsa_w2_v[e])
            w_mix = ab_w_out[e]
        else:
            o = layer // 2
            left, right = _mixer_shortconv_sb(x, mix_norm[layer], cd_w_in[o], sc_conv_w[o])
            w_mix = cd_w_out[o]
        last = layer == depth - 1
        x = _ffn(x.reshape(n, d), ffn2_norm[layer], *ffn2_w, layer,
                 mixer=(left.reshape(n, -1), right.reshape(n, -1), w_mix.astype(BF16)),
                 final_g=final_norm if last else None).reshape(b, t, d)
    return x
```

```python
import functools

import numpy as np
import jax
import jax.numpy as jnp
from jax import lax
from jax.experimental import pallas as pl
from jax.experimental.pallas import tpu as pltpu

F32 = jnp.float32
BF16 = jnp.bfloat16

D_MODEL = 1024
HEAD_DIM = 64
GROUP_WIDTH = D_MODEL // 2
CONV_WIDTH = 31
NSA_HEADS = GROUP_WIDTH // HEAD_DIM
NSA_KV_GROUPS = 2
NSA_REP = NSA_HEADS // NSA_KV_GROUPS
NSA_CMP_BLOCK = 32
NSA_CMP_STRIDE = 16
NSA_CMP_HIDDEN = 128
NSA_SEL_BLOCK = 64
NSA_TOP_N = 16
NSA_FORCED = 3
NSA_WINDOW = 512
NSA_N_BRANCH = 3
SC_WIDTH = 3
SB_HEADS = GROUP_WIDTH // HEAD_DIM
D_FF = 2816
RMS_EPS = 1e-6
LN_EPS = 1e-5
NEG_INF = -1e30
SEL_FORCE = 1e4
QK_SCALE = HEAD_DIM ** -0.5
LOG2_E = float(np.log2(np.e))

LANES = 128
SUBLANES = 8
VMEM_LIMIT = 48 * 1024 * 1024

FFN_TM = 1024
FFN_TF = 256
PROJ_TM = 512
CONV_HALO = 32
CONV_CHUNK = 64
SC_HALO = 8
NSA_TQ = 256
NSA_TK = 512
SB_T = 256
SB_PAIRS = 2
SB_DEAD_LOG2 = -160.0


def _params(*sem):
    return pltpu.CompilerParams(dimension_semantics=sem, vmem_limit_bytes=VMEM_LIMIT)


def _dot(a, b):
    return jnp.dot(a, b, preferred_element_type=F32)


def _dot_nt(a, b):
    return lax.dot_general(a, b, (((1,), (1,)), ((), ())), preferred_element_type=F32)


def _sigmoid(x):
    return 1.0 / (1.0 + jnp.exp(-x))


def _rmsnorm_rows(x, g):
    return x * lax.rsqrt(jnp.mean(x * x, axis=-1, keepdims=True) + RMS_EPS) * g


def _ffn_kernel(x_ref, g_ref, wi_ref, wo_ref, *rest, n_ff, mixer_out, final_norm):
    rest = list(rest)
    x = x_ref[...]
    if mixer_out:
        a_ref, m_ref, wm_ref = rest[:3]
        rest = rest[3:]
        half = a_ref.shape[1]
        x = x + _dot(a_ref[...], wm_ref[:half, :]) + _dot(m_ref[...], wm_ref[half:, :])
    if final_norm:
        fg_ref, o_ref = rest
    else:
        (o_ref,) = rest
    xn = _rmsnorm_rows(x, g_ref[...]).astype(BF16)
    acc = None
    for j in range(n_ff):
        gate = _dot(xn, wi_ref[:, j * FFN_TF:(j + 1) * FFN_TF])
        up = _dot(xn, wi_ref[:, D_FF + j * FFN_TF:D_FF + (j + 1) * FFN_TF])
        h = gate * _sigmoid(gate) * up
        part = _dot(h.astype(BF16), wo_ref[j * FFN_TF:(j + 1) * FFN_TF, :])
        acc = part if acc is None else acc + part
    y = x + 0.5 * acc
    if final_norm:
        y = _rmsnorm_rows(y, fg_ref[...])
    o_ref[...] = y


def _ffn(x, g, w_in, w_out, layer, mixer=None, final_g=None):
    n, d = x.shape
    n_ff = D_FF // FFN_TF
    tm = min(FFN_TM, n)
    once = pl.Buffered(1)
    in_specs = [
        pl.BlockSpec((tm, d), lambda i: (i, 0)),
        pl.BlockSpec((1, d), lambda i: (0, 0)),
        pl.BlockSpec((None, d, 2 * D_FF), lambda i: (layer, 0, 0), pipeline_mode=once),
        pl.BlockSpec((None, D_FF, d), lambda i: (layer, 0, 0), pipeline_mode=once),
    ]
    args = [x, g.reshape(1, d), w_in, w_out]
    if mixer is not None:
        a, o, w_mix = mixer
        in_specs += [pl.BlockSpec((tm, a.shape[1]), lambda i: (i, 0)), pl.BlockSpec((tm, o.shape[1]), lambda i: (i, 0)),
                     pl.BlockSpec((d, d), lambda i: (0, 0), pipeline_mode=once)]
        args += [a, o, w_mix]
    if final_g is not None:
        in_specs.append(pl.BlockSpec((1, d), lambda i: (0, 0)))
        args.append(final_g.reshape(1, d))
    return pl.pallas_call(
        functools.partial(_ffn_kernel, n_ff=n_ff, mixer_out=mixer is not None, final_norm=final_g is not None),
        out_shape=jax.ShapeDtypeStruct((n, d), F32),
        grid=(n // tm,),
        in_specs=in_specs,
        out_specs=pl.BlockSpec((tm, d), lambda i: (i, 0)),
        compiler_params=_params("parallel"),
        name="ffn",
    )(*args)


AB_A = 0
AB_Q = 1024
AB_KC = 2048
AB_VC = 2176
AB_KS = 2304
AB_VS = 2560
AB_KW = 2816
AB_VW = 3072
AB_G = 3328
AB_COLS = 3584

POS_SPLIT = 3
POS_TOK = HEAD_DIM
POS_CMP = POS_TOK + 2 * POS_SPLIT
MASK_BIG = 2.0 ** 100


def _pos_lanes(hi, lo, base):
    lane = lax.broadcasted_iota(jnp.int32, (1, LANES), 1)
    in_hi = (lane >= base) & (lane < base + POS_SPLIT)
    in_lo = (lane >= base + POS_SPLIT) & (lane < base + 2 * POS_SPLIT)
    return jnp.where(in_hi, hi, jnp.where(in_lo, lo, 0.0))


def _proj_ab_kernel(x_ref, g_ref, w_ref, qb_ref, dw_ref, db_ref, lg_ref, lb_ref,
                    a_ref, q_ref, kc_ref, vc_ref, ks_ref, vs_ref, kw_ref, vw_ref, gt_ref, ext_ref, shift_ref, *, tm):
    xn = _rmsnorm_rows(x_ref[0], g_ref[...]).astype(BF16)

    def seg(lo, hi):
        return _dot(xn, w_ref[:, lo:hi])

    @pl.when(pl.program_id(1) == 0)
    def _():
        ext_ref[0:CONV_HALO, :] = jnp.zeros((CONV_HALO, GROUP_WIDTH), F32)

    av = seg(AB_A, AB_A + 2 * GROUP_WIDTH)
    ext_ref[CONV_HALO:, :] = av[:, :GROUP_WIDTH] * _sigmoid(av[:, GROUP_WIDTH:])
    _conformer_conv_rows(ext_ref, shift_ref, dw_ref, db_ref, lg_ref, lb_ref, a_ref, tm)
    ext_ref[0:CONV_HALO, :] = ext_ref[tm:tm + CONV_HALO, :]
    q_ref[0] = (seg(AB_Q, AB_KC) + qb_ref[...]).astype(BF16)
    kc_ref[0] = seg(AB_KC, AB_VC)
    vc_ref[0] = seg(AB_VC, AB_KS)
    one_lane = jnp.where(lax.broadcasted_iota(jnp.int32, (1, NSA_KV_GROUPS * LANES), 1) % LANES == HEAD_DIM, 1.0, 0.0)
    vs_ref[0] = (seg(AB_VS, AB_KW) + one_lane).astype(BF16)
    vw_ref[0] = (seg(AB_VW, AB_G) + one_lane).astype(BF16)
    gt_ref[0] = _sigmoid(seg(AB_G, AB_COLS))

    tok = pl.program_id(1) * tm + lax.broadcasted_iota(jnp.int32, (tm, 1), 0)
    blk = lax.shift_right_logical(tok, int(np.log2(NSA_SEL_BLOCK)))
    pos = _pos_lanes(blk.astype(F32), (tok & (NSA_SEL_BLOCK - 1)).astype(F32), POS_TOK)
    onehot = jnp.where(lax.broadcasted_iota(jnp.int32, (1, LANES), 1) == blk, 1.0, 0.0).astype(BF16)
    ks = seg(AB_KS, AB_VS)
    kw = seg(AB_KW, AB_VW)
    for g in range(NSA_KV_GROUPS):
        ks_ref[0, :, 2 * g * LANES:(2 * g + 1) * LANES] = (ks[:, g * LANES:(g + 1) * LANES] + pos).astype(BF16)
        ks_ref[0, :, (2 * g + 1) * LANES:(2 * g + 2) * LANES] = onehot
        kw_ref[0, :, g * LANES:(g + 1) * LANES] = (kw[:, g * LANES:(g + 1) * LANES] + pos).astype(BF16)


def _arrange_ab_weight(w):
    d = w.shape[0]
    kvw = NSA_KV_GROUPS * HEAD_DIM
    o = 2 * GROUP_WIDTH
    a = w[:, :o]
    q = w[:, o:o + GROUP_WIDTH].reshape(d, NSA_HEADS, HEAD_DIM) * (QK_SCALE * LOG2_E)
    o += GROUP_WIDTH
    kc, vc, ks, vs, kw, vw = [w[:, o + i * kvw:o + (i + 1) * kvw] for i in range(6)]
    o += 6 * kvw
    g = w[:, o:].reshape(d, NSA_KV_GROUPS, NSA_REP * NSA_N_BRANCH)

    zeros_h = jnp.zeros((d, NSA_HEADS, HEAD_DIM), w.dtype)
    q_pad = jnp.concatenate([q, zeros_h], axis=-1).reshape(d, NSA_HEADS * LANES)

    def k_pad(k):
        k = k.reshape(d, NSA_KV_GROUPS, HEAD_DIM)
        return jnp.concatenate([k, jnp.zeros_like(k)], axis=-1).reshape(d, NSA_KV_GROUPS * LANES)

    g_pad = jnp.pad(g, ((0, 0), (0, 0), (0, LANES - g.shape[-1]))).reshape(d, NSA_KV_GROUPS * LANES)
    out = jnp.concatenate([a, q_pad, kc, vc, k_pad(ks), k_pad(vs), k_pad(kw), k_pad(vw), g_pad], axis=-1)
    assert out.shape[1] == AB_COLS
    return out.astype(BF16)


def _bf16_terms(x, n):
    terms, rest = [], np.asarray(x, np.float64)
    for _ in range(n):
        term = rest.astype(BF16).astype(np.float64)
        terms.append(term)
        rest = rest - term
    return terms


def _alibi_query_lanes():
    row = np.zeros((NSA_HEADS, LANES), np.float64)
    for h in range(NSA_HEADS):
        slope = 2.0 ** (-8.0 * (h + 1) / NSA_HEADS)
        for i, term in enumerate(_bf16_terms(slope * np.log2(np.e), POS_SPLIT)):
            row[h, POS_TOK + i] = NSA_SEL_BLOCK * term
            row[h, POS_TOK + POS_SPLIT + i] = term
            row[h, POS_CMP + i] = 2 * NSA_CMP_STRIDE * term
            row[h, POS_CMP + POS_SPLIT + i] = NSA_CMP_STRIDE * term
    return jnp.asarray(row.reshape(1, NSA_HEADS * LANES), F32)


def _proj_ab(x, g, w, dw_w, dw_b, ln_g, ln_b):
    b, t, d = x.shape
    tm = min(PROJ_TM, t)
    wa = _arrange_ab_weight(w)

    def tok(width):
        return pl.BlockSpec((1, tm, width), lambda bi, i: (bi, i, 0))

    const = lambda width: pl.BlockSpec((1, width), lambda bi, i: (0, 0))
    once = pl.Buffered(1)
    gw = GROUP_WIDTH
    kv = NSA_KV_GROUPS * LANES
    widths = [gw, NSA_HEADS * LANES, LANES, LANES, 2 * kv, kv, kv, kv, kv]
    dtypes = [BF16, BF16, F32, F32, BF16, BF16, BF16, BF16, F32]
    return pl.pallas_call(
        functools.partial(_proj_ab_kernel, tm=tm),
        out_shape=[jax.ShapeDtypeStruct((b, t, wd), dt) for wd, dt in zip(widths, dtypes)],
        grid=(b, t // tm),
        in_specs=[tok(d), const(d), pl.BlockSpec((d, AB_COLS), lambda bi, i: (0, 0), pipeline_mode=once),
                  const(NSA_HEADS * LANES), pl.BlockSpec((CONV_WIDTH, gw), lambda bi, i: (0, 0)),
                  const(gw), const(gw), const(gw)],
        out_specs=[tok(wd) for wd in widths],
        scratch_shapes=[pltpu.VMEM((tm + CONV_HALO, gw), F32), pltpu.VMEM((SUBLANES, tm + CONV_HALO, gw), F32)],
        compiler_params=_params("arbitrary", "arbitrary"),
        name="proj_ab",
    )(x, g.reshape(1, d), wa, _alibi_query_lanes(), dw_w, dw_b.reshape(1, gw), ln_g.reshape(1, gw),
      ln_b.reshape(1, gw))


CD_B = 0
CD_C = 512
CD_U = 1024
CD_Q = 1536
CD_K = 2560
CD_V = 3072
CD_COLS = 3584


def _proj_cd_kernel(x_ref, g_ref, w_ref, scw_ref, c_ref, q_ref, k_ref, v_ref, ext_ref, *, tm):
    xn = _rmsnorm_rows(x_ref[0], g_ref[...]).astype(BF16)

    def seg(lo, hi):
        return _dot(xn, w_ref[:, lo:hi])

    q_ref[0] = seg(CD_Q, CD_K).astype(BF16)
    k_ref[0] = seg(CD_K, CD_V).astype(BF16)
    v_ref[0] = seg(CD_V, CD_COLS).astype(BF16)

    @pl.when(pl.program_id(1) == 0)
    def _():
        ext_ref[0:SC_HALO, :] = jnp.zeros((SC_HALO, GROUP_WIDTH), F32)

    ext_ref[SC_HALO:, :] = seg(CD_C, CD_U) * seg(CD_U, CD_Q)
    off = SC_HALO - (SC_WIDTH - 1)
    acc = jnp.zeros((tm, GROUP_WIDTH), F32)
    for k in range(SC_WIDTH):
        acc = acc + scw_ref[k:k + 1, :] * ext_ref[off + k:off + k + tm, :]
    c_ref[0] = (seg(CD_B, CD_C) * acc).astype(c_ref.dtype)
    ext_ref[0:SC_HALO, :] = ext_ref[tm:tm + SC_HALO, :]


def _arrange_cd_weight(w):
    d = w.shape[0]
    gw = GROUP_WIDTH
    q = w[:, 3 * gw:4 * gw].reshape(d, SB_HEADS // 2, 2, HEAD_DIM) * (QK_SCALE * LOG2_E)
    z = jnp.zeros((d, SB_HEADS // 2, HEAD_DIM), w.dtype)
    q_even = jnp.concatenate([q[:, :, 0], z], axis=-1)
    q_odd = jnp.concatenate([z, q[:, :, 1]], axis=-1)
    q_pad = jnp.stack([q_even, q_odd], axis=2).reshape(d, SB_HEADS * LANES)
    out = jnp.concatenate([w[:, :3 * gw], q_pad, w[:, 4 * gw:]], axis=-1)
    assert out.shape[1] == CD_COLS
    return out.astype(BF16)


def _proj_cd(x, g, w, sc_w):
    b, t, d = x.shape
    tm = min(PROJ_TM, t)
    wa = _arrange_cd_weight(w)

    def tok(width):
        return pl.BlockSpec((1, tm, width), lambda bi, i: (bi, i, 0))

    widths = [GROUP_WIDTH, SB_HEADS * LANES, GROUP_WIDTH, GROUP_WIDTH]
    return pl.pallas_call(
        functools.partial(_proj_cd_kernel, tm=tm),
        out_shape=[jax.ShapeDtypeStruct((b, t, wd), BF16) for wd in widths],
        grid=(b, t // tm),
        in_specs=[tok(d), pl.BlockSpec((1, d), lambda bi, i: (0, 0)),
                  pl.BlockSpec((d, CD_COLS), lambda bi, i: (0, 0)),
                  pl.BlockSpec((SC_WIDTH, GROUP_WIDTH), lambda bi, i: (0, 0))],
        out_specs=[tok(wd) for wd in widths],
        scratch_shapes=[pltpu.VMEM((tm + SC_HALO, GROUP_WIDTH), F32)],
        compiler_params=_params("arbitrary", "arbitrary"),
        name="proj_cd",
    )(x, g.reshape(1, d), wa, sc_w)


def _conformer_conv_rows(ext_ref, shift_ref, w_ref, b_ref, lg_ref, lb_ref, o_ref, tt):
    off = CONV_HALO - (CONV_WIDTH - 1)
    last = off + CONV_WIDTH - 1
    steps = {phase: [m for m in range((last - phase) // SUBLANES + 1) if off <= SUBLANES * m + phase]
             for phase in range(SUBLANES)}
    for phase in range(SUBLANES):
        rows = SUBLANES * steps[phase][-1] + tt
        shift_ref[phase, 0:rows, :] = ext_ref[phase:phase + rows, :]
    for c in range(tt // CONV_CHUNK):
        r0 = c * CONV_CHUNK
        acc = jnp.zeros((CONV_CHUNK, GROUP_WIDTH), F32)
        for phase in range(SUBLANES):
            for m in steps[phase]:
                k = SUBLANES * m + phase - off
                lo = r0 + SUBLANES * m
                acc = acc + w_ref[k:k + 1, :] * shift_ref[phase, lo:lo + CONV_CHUNK, :]
        y = acc + b_ref[...]
        mu = jnp.mean(y, axis=-1, keepdims=True)
        yc = y - mu
        var = jnp.mean(yc * yc, axis=-1, keepdims=True)
        yn = yc * lax.rsqrt(var + LN_EPS) * lg_ref[...] + lb_ref[...]
        o_ref[0, r0:r0 + CONV_CHUNK, :] = (yn * _sigmoid(yn)).astype(o_ref.dtype)


def _gelu_tanh(x):
    return 0.5 * x * (1.0 + jnp.tanh(np.sqrt(2.0 / np.pi).astype(np.float32) * (x + 0.044715 * (x * x * x))))


def _compress_kernel(x_ref, pe_ref, w1_ref, w2_ref, o_ref, y_ref, *, nblk, keys):
    stride = NSA_CMP_STRIDE

    def partial_sum(l0):
        acc = None
        for l in range(l0, l0 + stride):
            xl = x_ref[0, pl.ds(l - l0, nblk, stride=stride), :] + pe_ref[l:l + 1, :]
            part = _dot(xl.astype(BF16), w1_ref[l])
            acc = part if acc is None else acc + part
        return acc

    h = partial_sum(0) + pltpu.roll(partial_sum(stride), nblk - 1, 0)
    row = lax.broadcasted_iota(jnp.int32, (nblk, 1), 0)
    ratio = NSA_SEL_BLOCK // NSA_CMP_STRIDE
    for g in range(NSA_KV_GROUPS):
        y = _dot(_gelu_tanh(h[:, g * NSA_CMP_HIDDEN:(g + 1) * NSA_CMP_HIDDEN]).astype(BF16), w2_ref[...])
        y = jnp.where(row < nblk - 1, y, 0.0)
        if keys:
            y = y + _pos_lanes(lax.shift_right_logical(row, 1).astype(F32), (row & 1).astype(F32), POS_CMP)
        y_ref[...] = y
        for r in range(ratio):
            o_ref[0, g, r] = y_ref[pl.ds(r, nblk // ratio, stride=ratio), :].astype(o_ref.dtype)


def _compress(kx, pe, w1, w2, dup):
    b, t, width = kx.shape
    g = NSA_KV_GROUPS
    nblk = t // NSA_CMP_STRIDE
    ratio = NSA_SEL_BLOCK // NSA_CMP_STRIDE
    hid = NSA_CMP_HIDDEN
    zeros = jnp.zeros_like(w1)
    w1_bd = jnp.concatenate([jnp.concatenate([w1, zeros], axis=-1), jnp.concatenate([zeros, w1], axis=-1)],
                            axis=1).astype(BF16)
    w2p = jnp.concatenate([w2, w2 if dup else jnp.zeros_like(w2)], axis=-1).astype(BF16)
    return pl.pallas_call(
        functools.partial(_compress_kernel, nblk=nblk, keys=not dup),
        out_shape=jax.ShapeDtypeStruct((b, g, ratio, nblk // ratio, LANES), BF16),
        grid=(b,),
        in_specs=[pl.BlockSpec((1, t, width), lambda bi: (bi, 0, 0)),
                  pl.BlockSpec((NSA_CMP_BLOCK, width), lambda bi: (0, 0)),
                  pl.BlockSpec((NSA_CMP_BLOCK, width, g * hid), lambda bi: (0, 0, 0)),
                  pl.BlockSpec((hid, LANES), lambda bi: (0, 0))],
        out_specs=pl.BlockSpec((1, g, ratio, nblk // ratio, LANES), lambda bi: (bi, 0, 0, 0, 0)),
        scratch_shapes=[pltpu.VMEM((nblk, LANES), F32)],
        compiler_params=_params("parallel"),
        name="nsa_compress",
    )(kx, jnp.tile(pe, (1, g)), w1_bd, w2p)


def _nsa_kernel(q_ref, kc_ref, vc_ref, ks_ref, vs_ref, kw_ref, vw_ref, gt_ref, o_ref, *, t_len, ns):
    tq, tk, rep = NSA_TQ, NSA_TK, NSA_REP
    rows = rep * tq
    i = pl.program_id(2)
    q0 = i * tq

    q2 = q_ref[0]
    qg = jnp.concatenate([q2[:, r * LANES:(r + 1) * LANES] for r in range(rep)], axis=0)
    tq_i = q0 + lax.broadcasted_iota(jnp.int32, (tq, 1), 0)

    def per_head(x):
        return jnp.concatenate([x] * rep, axis=0)

    lane_i = lax.broadcasted_iota(jnp.int32, (1, ns), 1)
    ratio = NSA_SEL_BLOCK // NSA_CMP_STRIDE
    n_cmp = t_len // NSA_CMP_STRIDE - 1
    s_list = []
    for r in range(ratio):
        c_i = lane_i * ratio + r
        mask = ((c_i * NSA_CMP_STRIDE + (NSA_CMP_BLOCK - 1)) <= tq_i) & (c_i < n_cmp)
        s_list.append(_dot_nt(qg, kc_ref[0, 0, r]) + per_head(jnp.where(mask, 0.0, NEG_INF)))
    m = jnp.max(functools.reduce(jnp.maximum, s_list), axis=-1, keepdims=True)
    p_list = [jnp.exp2(s - m) for s in s_list]
    l = jnp.sum(functools.reduce(lambda a, b: a + b, p_list), axis=-1, keepdims=True)
    any_visible = per_head(tq_i >= NSA_CMP_BLOCK - 1)
    inv = jnp.where(any_visible, 1.0 / l, 0.0)
    p_list = [p * inv for p in p_list]
    o_c = functools.reduce(lambda a, b: a + b,
                           [_dot(p.astype(BF16), vc_ref[0, 0, r]) for r, p in enumerate(p_list)])

    def head_sum(p):
        return functools.reduce(lambda a, b: a + b, [p[r * tq:(r + 1) * tq] for r in range(rep)])

    ps = [head_sum(p) for p in p_list]
    lane_q = lax.broadcasted_iota(jnp.int32, (tq, ns), 1)
    prev_last = jnp.where(lane_q == 0, 0.0, pltpu.roll(ps[ratio - 1], 1, 1))
    imp = prev_last + ps[0] + ps[1] + ps[2] + ps[3]
    cur =lax.shift_right_logical(tq_i, int(np.log2(NSA_SEL_BLOCK)))
    visible = lane_q * NSA_SEL_BLOCK <= tq_i
    forced = (lane_q == 0) | (lane_q == cur) | (lane_q == cur - 1)
    score = jnp.where(visible, jnp.where(forced, SEL_FORCE, imp), -1.0)

    blk_f = lax.broadcasted_iota(jnp.int32, (ns, tq), 0).astype(F32)
    score_t = score.T
    work = jnp.where(score_t == SEL_FORCE, -2.0, score_t)
    for _ in range(min(NSA_TOP_N, ns) - NSA_FORCED):
        top = jnp.max(work, axis=0, keepdims=True)
        idx = jnp.min(jnp.where(work == top, blk_f, float(ns)), axis=0, keepdims=True)
        work = jnp.where(blk_f == idx, -2.0, work)
    sel = jnp.where((score >= 0.0) & (work.T == -2.0), 1.0, 0.0)

    unsel = ((sel - 1.0) * MASK_BIG).astype(BF16)
    if ns < LANES:
        unsel = jnp.concatenate([unsel, jnp.zeros((tq, LANES - ns), BF16)], axis=1)
    q_aug = jnp.concatenate([qg, jnp.concatenate([unsel] * rep, axis=0)], axis=1)

    def sel_tile(k0, causal):
        s = _dot_nt(q_aug, ks_ref[0, pl.ds(k0, tk), :])
        if causal:
            tok = k0 + lax.broadcasted_iota(jnp.int32, (1, tk), 1)
            s = s + per_head(jnp.where(tok <= tq_i, 0.0, -MASK_BIG))
        return s

    blocks_per_tile = tk // NSA_SEL_BLOCK
    tile_of_blk = lax.shift_right_logical(lax.broadcasted_iota(jnp.int32, (ns, LANES), 0),
                                          int(np.log2(blocks_per_tile)))
    lane_t = lax.broadcasted_iota(jnp.int32, (ns, LANES), 1)
    per_tile = _dot(sel.astype(BF16), jnp.where(tile_of_blk == lane_t, 1.0, 0.0).astype(BF16))
    tile_any = jnp.max(per_tile, axis=0, keepdims=True) > 0.0
    pow2 = lax.shift_left(jnp.ones((1, LANES), jnp.int32), lane_t[:1] & 15).astype(F32)
    tile_bits = jnp.sum(jnp.where(tile_any, pow2, 0.0), axis=-1, keepdims=True).astype(jnp.int32)[0, 0]

    n_past = lax.shift_right_logical(q0, int(np.log2(tk)))
    kd = pl.multiple_of(n_past * tk, tk)
    s = sel_tile(kd, True)
    m_s = jnp.max(s, axis=-1, keepdims=True)
    acc_s = _dot(jnp.exp2(s - m_s).astype(BF16), vs_ref[0, pl.ds(kd, tk), :])

    def sel_update(kt, carry):
        m_run, acc = carry
        k0 = pl.multiple_of(kt * tk, tk)
        s = sel_tile(k0, False)
        m_new = jnp.maximum(m_run, jnp.max(s, axis=-1, keepdims=True))
        acc = jnp.exp2(m_run - m_new) * acc + _dot(jnp.exp2(s - m_new).astype(BF16), vs_ref[0, pl.ds(k0, tk), :])
        return m_new, acc

    def next_active(state):
        return lax.while_loop(lambda st: (st[0] & 1) == 0,
                              lambda st: (lax.shift_right_logical(st[0], 1), st[1] + 1), state)

    def sel_body(state):
        pending, kt = next_active(state[:2])
        return (lax.shift_right_logical(pending, 1), kt + 1) + sel_update(kt, state[2:])

    past_bits = tile_bits & (lax.shift_left(jnp.int32(1), n_past) - 1)
    _, _, _, acc_s = lax.while_loop(lambda st: st[0] != 0, sel_body, (past_bits, jnp.int32(0), m_s, acc_s))
    o_s = acc_s / acc_s[:, HEAD_DIM:HEAD_DIM + 1]

    wlen = NSA_WINDOW + tq
    w0 = pl.multiple_of(jnp.maximum(q0 - NSA_WINDOW, 0), tq)
    dist = tq_i - (w0 + lax.broadcasted_iota(jnp.int32, (1, wlen), 1))
    mask_w = (dist >= 0) & (dist < NSA_WINDOW)
    s_w = _dot_nt(qg, kw_ref[0, pl.ds(w0, wlen), :]) + per_head(jnp.where(mask_w, 0.0, NEG_INF))
    p_w = jnp.exp2(s_w - jnp.max(s_w, axis=-1, keepdims=True))
    o_w = _dot(p_w.astype(BF16), vw_ref[0, pl.ds(w0, wlen), :])
    o_w = o_w / o_w[:, HEAD_DIM:HEAD_DIM + 1]

    gt = gt_ref[0]

    def gate(branch):
        return jnp.concatenate(
            [gt[:, r * NSA_N_BRANCH + branch:r * NSA_N_BRANCH + branch + 1] for r in range(rep)], axis=0)

    o = gate(0) * o_c + gate(1) * o_s + gate(2) * o_w
    low_half = lax.broadcasted_iota(jnp.int32, (tq, LANES), 1) < HEAD_DIM
    for c in range(rep // 2):
        even = o[(2 * c) * tq:(2 * c + 1) * tq]
        odd = pltpu.roll(o[(2 * c + 1) * tq:(2 * c + 2) * tq], HEAD_DIM, 1)
        o_ref[0, :, c * LANES:(c + 1) * LANES] = jnp.where(low_half, even, odd).astype(o_ref.dtype)


def _nsa(q, kc, vc, ks, vs, kw, vw, gates):
    b, t, _ = q.shape
    g = NSA_KV_GROUPS
    ns = t // NSA_SEL_BLOCK
    assert ns <= LANES and t >= NSA_WINDOW + NSA_TQ and t % NSA_TK == 0 and NSA_TK % NSA_TQ == 0
    assert t // NSA_TK <= 16
    ratio = NSA_SEL_BLOCK // NSA_CMP_STRIDE
    qw = NSA_REP * LANES
    cmp_spec = pl.BlockSpec((1, 1, ratio, ns, LANES), lambda bi, gi, i: (bi, gi, 0, 0, 0))
    kv_spec = pl.BlockSpec((1, t, LANES), lambda bi, gi, i: (bi, 0, gi))
    ks_spec = pl.BlockSpec((1, t, 2 * LANES), lambda bi, gi, i: (bi, 0, gi))
    return pl.pallas_call(
        functools.partial(_nsa_kernel, t_len=t, ns=ns),
        out_shape=jax.ShapeDtypeStruct((b, t, GROUP_WIDTH), BF16),
        grid=(b, g, t // NSA_TQ),
        in_specs=[pl.BlockSpec((1, NSA_TQ, qw), lambda bi, gi, i: (bi, i, gi)),
                  cmp_spec, cmp_spec, ks_spec, kv_spec, kv_spec, kv_spec,
                  pl.BlockSpec((1, NSA_TQ, LANES), lambda bi, gi, i: (bi, i, gi))],
        out_specs=pl.BlockSpec((1, NSA_TQ, NSA_REP * HEAD_DIM), lambda bi, gi, i: (bi, i, gi)),
        compiler_params=_params("parallel", "parallel", "arbitrary"),
        name="nsa_attention",
    )(q, kc, vc, ks, vs, kw, vw, gates)


def _sb_tiles(qs, k, v, u2, carries, mask):
    stage1 = []
    for h, qh in enumerate(qs):
        p = h // 2
        z = _dot_nt(qh, k[:, p * LANES:(p + 1) * LANES])
        nk = jnp.maximum(z, 0.0) + jnp.log2(1.0 + jnp.exp2(-jnp.abs(z)))
        if mask is not None:
            nk = jnp.where(mask, nk, 0.0)
        hi = nk.astype(BF16)
        lo = (nk - hi.astype(F32)).astype(BF16)
        stage1.append((z, jnp.concatenate([hi, lo], axis=1)))
    laters = [_dot(hilo, u2) for (_, hilo) in stage1]
    out = []
    for h, ((z, _), later, (acc, c)) in enumerate(zip(stage1, laters, carries)):
        p = h // 2
        a = jnp.exp2(z + later)
        if mask is not None:
            a = jnp.where(mask, a, 0.0)
        acc = acc + jnp.exp2(c) * _dot(a.astype(BF16), v[:, p * LANES:(p + 1) * LANES])
        out.append((acc, c + later[:, 0:1]))
    return tuple(out)


def _sb_kernel(q_ref, k_ref, v_ref, u_ref, o_ref):
    tile = SB_T
    i = pl.program_id(2)
    q2 = q_ref[0]
    u = u_ref[...]
    qs = [q2[:, h * LANES:(h + 1) * LANES] for h in range(2 * SB_PAIRS)]
    q0 = pl.multiple_of(i * tile, tile)
    zero = (jnp.zeros((tile, LANES), F32), jnp.zeros((tile, 1), F32))
    mask = lax.broadcasted_iota(jnp.int32, (tile, tile), 1) < lax.broadcasted_iota(jnp.int32, (tile, tile), 0)
    carry = _sb_tiles(qs, k_ref[0, pl.ds(q0, tile), :], v_ref[0, pl.ds(q0, tile), :], u,
                      tuple(zero for _ in qs), mask)

    def body(jj, carry):
        k0 = pl.multiple_of((i - 1 - jj) * tile, tile)
        return _sb_tiles(qs, k_ref[0, pl.ds(k0, tile), :], v_ref[0, pl.ds(k0, tile), :], u, carry, None)

    def alive(carry):
        c_max = functools.reduce(jnp.maximum, [c for _, c in carry])
        return jnp.max(c_max) > SB_DEAD_LOG2

    def step(state):
        jj, _, carry = state
        carry = body(jj, carry)
        return jj + 1, alive(carry), carry

    _, _, carry = lax.while_loop(lambda st: (st[0] < i) & st[1], step, (jnp.int32(0), alive(carry), carry))
    low_half = lax.broadcasted_iota(jnp.int32, (tile, LANES), 1) < HEAD_DIM
    for p in range(SB_PAIRS):
        o_ref[0, :, p * LANES:(p + 1) * LANES] = jnp.where(
            low_half, carry[2 * p][0], carry[2 * p + 1][0]).astype(o_ref.dtype)


def _stick_breaking(q, k, v):
    b, t, _ = q.shape
    tile = SB_T
    assert t % tile == 0
    u = -(np.arange(tile)[:, None] >= np.arange(tile)[None, :]).astype(np.float32)
    u2 = np.concatenate([u, u], axis=0)
    kv_spec = pl.BlockSpec((1, t, SB_PAIRS * LANES), lambda bi, hp, i: (bi, 0, hp))
    return pl.pallas_call(
        _sb_kernel,
        out_shape=jax.ShapeDtypeStruct((b, t, GROUP_WIDTH), BF16),
        grid=(b, SB_HEADS // (2 * SB_PAIRS), t // tile),
        in_specs=[pl.BlockSpec((1, tile, 2 * SB_PAIRS * LANES), lambda bi, hp, i: (bi, i, hp)),
                  kv_spec, kv_spec,
                  pl.BlockSpec((2 * tile, tile), lambda bi, hp, i: (0, 0))],
        out_specs=pl.BlockSpec((1, tile, SB_PAIRS * LANES), lambda bi, hp, i: (bi, i, hp)),
        compiler_params=_params("parallel", "parallel", "arbitrary"),
        name="stick_breaking",
    )(q, k, v, jnp.asarray(u2, BF16))


def _mixer_conv_nsa(x, norm_g, w_in, dw_w, dw_b, ln_g, ln_b, pe_k, w1_k, w2_k, pe_v, w1_v, w2_v):
    a, q, kc, vc, ks, vs, kw, vw, gates = _proj_ab(x, norm_g, w_in, dw_w, dw_b, ln_g, ln_b)
    k_cmp = _compress(kc, pe_k, w1_k, w2_k, dup=False)
    v_cmp = _compress(vc, pe_v, w1_v, w2_v, dup=True)
    return a, _nsa(q, k_cmp, v_cmp, ks, vs, kw, vw, gates)


def _mixer_shortconv_sb(x, norm_g, w_in, sc_w):
    c, q, k, v = _proj_cd(x, norm_g, w_in, sc_w)
    return c, _stick_breaking(q, k, v)


def kernel(x, ffn1_norm, ffn1_w_in, ffn1_w_out, mix_norm, ffn2_norm, ffn2_w_in, ffn2_w_out, ab_w_in, conv_dw_w, conv_dw_b, conv_ln_g, conv_ln_b, nsa_pe_k, nsa_w1_k, nsa_w2_k, nsa_pe_v, nsa_w1_v, nsa_w2_v, ab_w_out, cd_w_in, sc_conv_w, cd_w_out, final_norm):
    b, t, d = x.shape
    depth = ffn1_norm.shape[0]
    n = b * t
    ffn1_w = (ffn1_w_in.astype(BF16), ffn1_w_out.astype(BF16))
    ffn2_w = (ffn2_w_in.astype(BF16), ffn2_w_out.astype(BF16))
    for layer in range(depth):
        x = _ffn(x.reshape(n, d), ffn1_norm[layer], *ffn1_w, layer).reshape(b, t, d)
        if layer % 2 == 0:
            e = layer // 2
            left, right = _mixer_conv_nsa(x, mix_norm[layer], ab_w_in[e], conv_dw_w[e], conv_dw_b[e], conv_ln_g[e],
                                          conv_ln_b[e], nsa_pe_k[e], nsa_w1_k[e], nsa_w2_k[e],
                                          nsa_pe_v[e], nsa_w1_v[e], nsa_w2_v[e])
            w_mix = ab_w_out[e]
        else:
            o = layer // 2
            left, right = _mixer_shortconv_sb(x, mix_norm[layer], cd_w_in[o], sc_conv_w[o])
            w_mix = cd_w_out[o]
        last = layer == depth - 1
        x = _ffn(x.reshape(n, d), ffn2_norm[layer], *ffn2_w, layer,
                 mixer=(left.reshape(n, -1), right.reshape(n, -1), w_mix.astype(BF16)),
                 final_g=final_norm if last else None).reshape(b, t, d)
    return x
```

```python
import functools

import numpy as np
import jax
import jax.numpy as jnp
from jax import lax
from jax.experimental import pallas as pl
from jax.experimental.pallas import tpu as pltpu

F32 = jnp.float32
BF16 = jnp.bfloat16

D_MODEL = 1024
HEAD_DIM = 64
GROUP_WIDTH = D_MODEL // 2
CONV_WIDTH = 31
NSA_HEADS = GROUP_WIDTH // HEAD_DIM
NSA_KV_GROUPS = 2
NSA_REP = NSA_HEADS // NSA_KV_GROUPS
NSA_CMP_BLOCK = 32
NSA_CMP_STRIDE = 16
NSA_CMP_HIDDEN = 128
NSA_SEL_BLOCK = 64
NSA_TOP_N = 16
NSA_FORCED = 3
NSA_WINDOW = 512
NSA_N_BRANCH = 3
SC_WIDTH = 3
SB_HEADS = GROUP_WIDTH // HEAD_DIM
D_FF = 2816
RMS_EPS = 1e-6
LN_EPS = 1e-5
NEG_INF = -1e30
SEL_FORCE = 1e4
QK_SCALE = HEAD_DIM ** -0.5
LOG2_E = float(np.log2(np.e))

LANES = 128
SUBLANES = 8
VMEM_LIMIT = 48 * 1024 * 1024

FFN_TM = 1024
FFN_TF = 256
PROJ_TM = 512
CONV_HALO = 32
CONV_CHUNK = 64
SC_HALO = 8
NSA_TQ = 256
NSA_TK = 512
SB_T = 256
SB_PAIRS = 2
SB_DEAD_LOG2 = -160.0


def _params(*sem):
    return pltpu.CompilerParams(dimension_semantics=sem, vmem_limit_bytes=VMEM_LIMIT)


def _dot(a, b):
    return jnp.dot(a, b, preferred_element_type=F32)


def _dot_nt(a, b):
    return lax.dot_general(a, b, (((1,), (1,)), ((), ())), preferred_element_type=F32)


def _sigmoid(x):
    return 1.0 / (1.0 + jnp.exp(-x))


def _rmsnorm_rows(x, g):
    return x * lax.rsqrt(jnp.mean(x * x, axis=-1, keepdims=True) + RMS_EPS) * g


def _ffn_kernel(x_ref, g_ref, wi_ref, wo_ref, *rest, n_ff, mixer_out, final_norm):
    rest = list(rest)
    x = x_ref[...]
    if mixer_out:
        a_ref, m_ref, wm_ref = rest[:3]
        rest = rest[3:]
        half = a_ref.shape[1]
        x = x + _dot(a_ref[...], wm_ref[:half, :]) + _dot(m_ref[...], wm_ref[half:, :])
    if final_norm:
        fg_ref, o_ref = rest
    else:
        (o_ref,) = rest
    xn = _rmsnorm_rows(x, g_ref[...]).astype(BF16)
    acc = None
    for j in range(n_ff):
        gate = _dot(xn, wi_ref[:, j * FFN_TF:(j + 1) * FFN_TF])
        up = _dot(xn, wi_ref[:, D_FF + j * FFN_TF:D_FF + (j + 1) * FFN_TF])
        h = gate * _sigmoid(gate) * up
        part = _dot(h.astype(BF16), wo_ref[j * FFN_TF:(j + 1) * FFN_TF, :])
        acc = part if acc is None else acc + part
    y = x + 0.5 * acc
    if final_norm:
        y = _rmsnorm_rows(y, fg_ref[...])
    o_ref[...] = y


def _ffn(x, g, w_in, w_out, layer, mixer=None, final_g=None):
    n, d = x.shape
    n_ff = D_FF // FFN_TF
    tm = min(FFN_TM, n)
    once = pl.Buffered(1)
    in_specs = [
        pl.BlockSpec((tm, d), lambda i: (i, 0)),
        pl.BlockSpec((1, d), lambda i: (0, 0)),
        pl.BlockSpec((None, d, 2 * D_FF), lambda i: (layer, 0, 0), pipeline_mode=once),
        pl.BlockSpec((None, D_FF, d), lambda i: (layer, 0, 0), pipeline_mode=once),
    ]
    args = [x, g.reshape(1, d), w_in, w_out]
    if mixer is not None:
        a, o, w_mix = mixer
        in_specs += [pl.BlockSpec((tm, a.shape[1]), lambda i: (i, 0)), pl.BlockSpec((tm, o.shape[1]), lambda i: (i, 0)),
                     pl.BlockSpec((d, d), lambda i: (0, 0), pipeline_mode=once)]
        args += [a, o, w_mix]
    if final_g is not None:
        in_specs.append(pl.BlockSpec((1, d), lambda i: (0, 0)))
        args.append(final_g.reshape(1, d))
    return pl.pallas_call(
        functools.partial(_ffn_kernel, n_ff=n_ff, mixer_out=mixer is not None, final_norm=final_g is not None),
        out_shape=jax.ShapeDtypeStruct((n, d), F32),
        grid=(n // tm,),
        in_specs=in_specs,
        out_specs=pl.BlockSpec((tm, d), lambda i: (i, 0)),
        compiler_params=_params("parallel"),
        name="ffn",
    )(*args)


AB_A = 0
AB_Q = 1024
AB_KC = 2048
AB_VC = 2176
AB_KS = 2304
AB_VS = 2560
AB_KW = 2816
AB_VW = 3072
AB_G = 3328
AB_COLS = 3584

POS_SPLIT = 3
POS_TOK = HEAD_DIM
POS_CMP = POS_TOK + 2 * POS_SPLIT
MASK_BIG = 2.0 ** 100


def _pos_lanes(hi, lo, base):
    lane = lax.broadcasted_iota(jnp.int32, (1, LANES), 1)
    in_hi = (lane >= base) & (lane < base + POS_SPLIT)
    in_lo = (lane >= base + POS_SPLIT) & (lane < base + 2 * POS_SPLIT)
    return jnp.where(in_hi, hi, jnp.where(in_lo, lo, 0.0))


def _proj_ab_kernel(x_ref, g_ref, w_ref, qb_ref, dw_ref, db_ref, lg_ref, lb_ref,
                    a_ref, q_ref, kc_ref, vc_ref, ks_ref, vs_ref, kw_ref, vw_ref, gt_ref, ext_ref, shift_ref, *, tm):
    xn = _rmsnorm_rows(x_ref[0], g_ref[...]).astype(BF16)

    def seg(lo, hi):
        return _dot(xn, w_ref[:, lo:hi])

    @pl.when(pl.program_id(1) == 0)
    def _():
        ext_ref[0:CONV_HALO, :] = jnp.zeros((CONV_HALO, GROUP_WIDTH), F32)

    av = seg(AB_A, AB_A + 2 * GROUP_WIDTH)
    ext_ref[CONV_HALO:, :] = av[:, :GROUP_WIDTH] * _sigmoid(av[:, GROUP_WIDTH:])
    _conformer_conv_rows(ext_ref, shift_ref, dw_ref, db_ref, lg_ref, lb_ref, a_ref, tm)
    ext_ref[0:CONV_HALO, :] = ext_ref[tm:tm + CONV_HALO, :]
    q_ref[0] = (seg(AB_Q, AB_KC) + qb_ref[...]).astype(BF16)
    kc_ref[0] = seg(AB_KC, AB_VC)
    vc_ref[0] = seg(AB_VC, AB_KS)
    one_lane = jnp.where(lax.broadcasted_iota(jnp.int32, (1, NSA_KV_GROUPS * LANES), 1) % LANES == HEAD_DIM, 1.0, 0.0)
    vs_ref[0] = (seg(AB_VS, AB_KW) + one_lane).astype(BF16)
    vw_ref[0] = (seg(AB_VW, AB_G) + one_lane).astype(BF16)
    gt_ref[0] = _sigmoid(seg(AB_G, AB_COLS))

    tok = pl.program_id(1) * tm + lax.broadcasted_iota(jnp.int32, (tm, 1), 0)
    blk = lax.shift_right_logical(tok, int(np.log2(NSA_SEL_BLOCK)))
    pos = _pos_lanes(blk.astype(F32), (tok & (NSA_SEL_BLOCK - 1)).astype(F32), POS_TOK)
    onehot = jnp.where(lax.broadcasted_iota(jnp.int32, (1, LANES), 1) == blk, 1.0, 0.0).astype(BF16)
    ks = seg(AB_KS, AB_VS)
    kw = seg(AB_KW, AB_VW)
    for g in range(NSA_KV_GROUPS):
        ks_ref[0, :, 2 * g * LANES:(2 * g + 1) * LANES] = (ks[:, g * LANES:(g + 1) * LANES] + pos).astype(BF16)
        ks_ref[0, :, (2 * g + 1) * LANES:(2 * g + 2) * LANES] = onehot
        kw_ref[0, :, g * LANES:(g + 1) * LANES] = (kw[:, g * LANES:(g + 1) * LANES] + pos).astype(BF16)


def _arrange_ab_weight(w):
    d = w.shape[0]
    kvw = NSA_KV_GROUPS * HEAD_DIM
    o = 2 * GROUP_WIDTH
    a = w[:, :o]
    q = w[:, o:o + GROUP_WIDTH].reshape(d, NSA_HEADS, HEAD_DIM) * (QK_SCALE * LOG2_E)
    o += GROUP_WIDTH
    kc, vc, ks, vs, kw, vw = [w[:, o + i * kvw:o + (i + 1) * kvw] for i in range(6)]
    o += 6 * kvw
    g = w[:, o:].reshape(d, NSA_KV_GROUPS, NSA_REP * NSA_N_BRANCH)

    zeros_h = jnp.zeros((d, NSA_HEADS, HEAD_DIM), w.dtype)
    q_pad = jnp.concatenate([q, zeros_h], axis=-1).reshape(d, NSA_HEADS * LANES)

    def k_pad(k):
        k = k.reshape(d, NSA_KV_GROUPS, HEAD_DIM)
        return jnp.concatenate([k, jnp.zeros_like(k)], axis=-1).reshape(d, NSA_KV_GROUPS * LANES)

    g_pad = jnp.pad(g, ((0, 0), (0, 0), (0, LANES - g.shape[-1]))).reshape(d, NSA_KV_GROUPS * LANES)
    out = jnp.concatenate([a, q_pad, kc, vc, k_pad(ks), k_pad(vs), k_pad(kw), k_pad(vw), g_pad], axis=-1)
    assert out.shape[1] == AB_COLS
    return out.astype(BF16)


def _bf16_terms(x, n):
    terms, rest = [], np.asarray(x, np.float64)
    for _ in range(n):
        term = rest.astype(BF16).astype(np.float64)
        terms.append(term)
        rest = rest - term
    return terms


def _alibi_query_lanes():
    row = np.zeros((NSA_HEADS, LANES), np.float64)
    for h in range(NSA_HEADS):
        slope = 2.0 ** (-8.0 * (h + 1) / NSA_HEADS)
        for i, term in enumerate(_bf16_terms(slope * np.log2(np.e), POS_SPLIT)):
            row[h, POS_TOK + i] = NSA_SEL_BLOCK * term
            row[h, POS_TOK + POS_SPLIT + i] = term
            row[h, POS_CMP + i] = 2 * NSA_CMP_STRIDE * term
            row[h, POS_CMP + POS_SPLIT + i] = NSA_CMP_STRIDE * term
    return jnp.asarray(row.reshape(1, NSA_HEADS * LANES), F32)


def _proj_ab(x, g, w, dw_w, dw_b, ln_g, ln_b):
    b, t, d = x.shape
    tm = min(PROJ_TM, t)
    wa = _arrange_ab_weight(w)

    def tok(width):
        return pl.BlockSpec((1, tm, width), lambda bi, i: (bi, i, 0))

    const = lambda width: pl.BlockSpec((1, width), lambda bi, i: (0, 0))
    once = pl.Buffered(1)
    gw = GROUP_WIDTH
    kv = NSA_KV_GROUPS * LANES
    widths = [gw, NSA_HEADS * LANES, LANES, LANES, 2 * kv, kv, kv, kv, kv]
    dtypes = [BF16, BF16, F32, F32, BF16, BF16, BF16, BF16, F32]
    return pl.pallas_call(
        functools.partial(_proj_ab_kernel, tm=tm),
        out_shape=[jax.ShapeDtypeStruct((b, t, wd), dt) for wd, dt in zip(widths, dtypes)],
        grid=(b, t // tm),
        in_specs=[tok(d), const(d), pl.BlockSpec((d, AB_COLS), lambda bi, i: (0, 0), pipeline_mode=once),
                  const(NSA_HEADS * LANES), pl.BlockSpec((CONV_WIDTH, gw), lambda bi, i: (0, 0)),
                  const(gw), const(gw), const(gw)],
        out_specs=[tok(wd) for wd in widths],
        scratch_shapes=[pltpu.VMEM((tm + CONV_HALO, gw), F32), pltpu.VMEM((SUBLANES, tm + CONV_HALO, gw), F32)],
        compiler_params=_params("arbitrary", "arbitrary"),
        name="proj_ab",
    )(x, g.reshape(1, d), wa, _alibi_query_lanes(), dw_w, dw_b.reshape(1, gw), ln_g.reshape(1, gw),
      ln_b.reshape(1, gw))


CD_B = 0
CD_C = 512
CD_U = 1024
CD_Q = 1536
CD_K = 2560
CD_V = 3072
CD_COLS = 3584


def _proj_cd_kernel(x_ref, g_ref, w_ref, scw_ref, c_ref, q_ref, k_ref, v_ref, ext_ref, *, tm):
    xn = _rmsnorm_rows(x_ref[0], g_ref[...]).astype(BF16)

    def seg(lo, hi):
        return _dot(xn, w_ref[:, lo:hi])

    q_ref[0] = seg(CD_Q, CD_K).astype(BF16)
    k_ref[0] = seg(CD_K, CD_V).astype(BF16)
    v_ref[0] = seg(CD_V, CD_COLS).astype(BF16)

    @pl.when(pl.program_id(1) == 0)
    def _():
        ext_ref[0:SC_HALO, :] = jnp.zeros((SC_HALO, GROUP_WIDTH), F32)

    ext_ref[SC_HALO:, :] = seg(CD_C, CD_U) * seg(CD_U, CD_Q)
    off = SC_HALO - (SC_WIDTH - 1)
    acc = jnp.zeros((tm, GROUP_WIDTH), F32)
    for k in range(SC_WIDTH):
        acc = acc + scw_ref[k:k + 1, :] * ext_ref[off + k:off + k + tm, :]
    c_ref[0] = (seg(CD_B, CD_C) * acc).astype(c_ref.dtype)
    ext_ref[0:SC_HALO, :] = ext_ref[tm:tm + SC_HALO, :]


def _arrange_cd_weight(w):
    d = w.shape[0]
    gw = GROUP_WIDTH
    q = w[:, 3 * gw:4 * gw].reshape(d, SB_HEADS // 2, 2, HEAD_DIM) * (QK_SCALE * LOG2_E)
    z = jnp.zeros((d, SB_HEADS // 2, HEAD_DIM), w.dtype)
    q_even = jnp.concatenate([q[:, :, 0], z], axis=-1)
    q_odd = jnp.concatenate([z, q[:, :, 1]], axis=-1)
    q_pad = jnp.stack([q_even, q_odd], axis=2).reshape(d, SB_HEADS * LANES)
    out = jnp.concatenate([w[:, :3 * gw], q_pad, w[:, 4 * gw:]], axis=-1)
    assert out.shape[1] == CD_COLS
    return out.astype(BF16)


def _proj_cd(x, g, w, sc_w):
    b, t, d = x.shape
    tm = min(PROJ_TM, t)
    wa = _arrange_cd_weight(w)

    def tok(width):
        return pl.BlockSpec((1, tm, width), lambda bi, i: (bi, i, 0))

    widths = [GROUP_WIDTH, SB_HEADS * LANES, GROUP_WIDTH, GROUP_WIDTH]
    return pl.pallas_call(
        functools.partial(_proj_cd_kernel, tm=tm),
        out_shape=[jax.ShapeDtypeStruct((b, t, wd), BF16) for wd in widths],
        grid=(b, t // tm),
        in_specs=[tok(d), pl.BlockSpec((1, d), lambda bi, i: (0, 0)),
                  pl.BlockSpec((d, CD_COLS), lambda bi, i: (0, 0)),
                  pl.BlockSpec((SC_WIDTH, GROUP_WIDTH), lambda bi, i: (0, 0))],
        out_specs=[tok(wd) for wd in widths],
        scratch_shapes=[pltpu.VMEM((tm + SC_HALO, GROUP_WIDTH), F32)],
        compiler_params=_params("arbitrary", "arbitrary"),
        name="proj_cd",
    )(x, g.reshape(1, d), wa, sc_w)


def _conformer_conv_rows(ext_ref, shift_ref, w_ref, b_ref, lg_ref, lb_ref, o_ref, tt):
    off = CONV_HALO - (CONV_WIDTH - 1)
    last = off + CONV_WIDTH - 1
    steps = {phase: [m for m in range((last - phase) // SUBLANES + 1) if off <= SUBLANES * m + phase]
             for phase in range(SUBLANES)}
    for phase in range(SUBLANES):
        rows = SUBLANES * steps[phase][-1] + tt
        shift_ref[phase, 0:rows, :] = ext_ref[phase:phase + rows, :]
    for c in range(tt // CONV_CHUNK):
        r0 = c * CONV_CHUNK
        acc = jnp.zeros((CONV_CHUNK, GROUP_WIDTH), F32)
        for phase in range(SUBLANES):
            for m in steps[phase]:
                k = SUBLANES * m + phase - off
                lo = r0 + SUBLANES * m
                acc = acc + w_ref[k:k + 1, :] * shift_ref[phase, lo:lo + CONV_CHUNK, :]
        y = acc + b_ref[...]
        mu = jnp.mean(y, axis=-1, keepdims=True)
        yc = y - mu
        var = jnp.mean(yc * yc, axis=-1, keepdims=True)
        yn = yc * lax.rsqrt(var + LN_EPS) * lg_ref[...] + lb_ref[...]
        o_ref[0, r0:r0 + CONV_CHUNK, :] = (yn * _sigmoid(yn)).astype(o_ref.dtype)


def _gelu_tanh(x):
    return 0.5 * x * (1.0 + jnp.tanh(np.sqrt(2.0 / np.pi).astype(np.float32) * (x + 0.044715 * (x * x * x))))


def _compress_kernel(x_ref, pe_ref, w1_ref, w2_ref, o_ref, y_ref, *, nblk, keys):
    stride = NSA_CMP_STRIDE

    def partial_sum(l0):
        acc = None
        for l in range(l0, l0 + stride):
            xl = x_ref[0, pl.ds(l - l0, nblk, stride=stride), :] + pe_ref[l:l + 1, :]
            part = _dot(xl.astype(BF16), w1_ref[l])
            acc = part if acc is None else acc + part
        return acc

    h = partial_sum(0) + pltpu.roll(partial_sum(stride), nblk - 1, 0)
    row = lax.broadcasted_iota(jnp.int32, (nblk, 1), 0)
    ratio = NSA_SEL_BLOCK // NSA_CMP_STRIDE
    for g in range(NSA_KV_GROUPS):
        y = _dot(_gelu_tanh(h[:, g * NSA_CMP_HIDDEN:(g + 1) * NSA_CMP_HIDDEN]).astype(BF16), w2_ref[...])
        y = jnp.where(row < nblk - 1, y, 0.0)
        if keys:
            y = y + _pos_lanes(lax.shift_right_logical(row, 1).astype(F32), (row & 1).astype(F32), POS_CMP)
        y_ref[...] = y
        for r in range(ratio):
            o_ref[0, g, r] = y_ref[pl.ds(r, nblk // ratio, stride=ratio), :].astype(o_ref.dtype)


def _compress(kx, pe, w1, w2, dup):
    b, t, width = kx.shape
    g = NSA_KV_GROUPS
    nblk = t // NSA_CMP_STRIDE
    ratio = NSA_SEL_BLOCK // NSA_CMP_STRIDE
    hid = NSA_CMP_HIDDEN
    zeros = jnp.zeros_like(w1)
    w1_bd = jnp.concatenate([jnp.concatenate([w1, zeros], axis=-1), jnp.concatenate([zeros, w1], axis=-1)],
                            axis=1).astype(BF16)
    w2p = jnp.concatenate([w2, w2 if dup else jnp.zeros_like(w2)], axis=-1).astype(BF16)
    return pl.pallas_call(
        functools.partial(_compress_kernel, nblk=nblk, keys=not dup),
        out_shape=jax.ShapeDtypeStruct((b, g, ratio, nblk // ratio, LANES), BF16),
        grid=(b,),
        in_specs=[pl.BlockSpec((1, t, width), lambda bi: (bi, 0, 0)),
                  pl.BlockSpec((NSA_CMP_BLOCK, width), lambda bi: (0, 0)),
                  pl.BlockSpec((NSA_CMP_BLOCK, width, g * hid), lambda bi: (0, 0, 0)),
                  pl.BlockSpec((hid, LANES), lambda bi: (0, 0))],
        out_specs=pl.BlockSpec((1, g, ratio, nblk // ratio, LANES), lambda bi: (bi, 0, 0, 0, 0)),
        scratch_shapes=[pltpu.VMEM((nblk, LANES), F32)],
        compiler_params=_params("parallel"),
        name="nsa_compress",
    )(kx, jnp.tile(pe, (1, g)), w1_bd, w2p)


def _nsa_kernel(q_ref, kc_ref, vc_ref, ks_ref, vs_ref, kw_ref, vw_ref, gt_ref, o_ref, *, t_len, ns):
    tq, tk, rep = NSA_TQ, NSA_TK, NSA_REP
    rows = rep * tq
    i = pl.program_id(2)
    q0 = i * tq

    q2 = q_ref[0]
    qg = jnp.concatenate([q2[:, r * LANES:(r + 1) * LANES] for r in range(rep)], axis=0)
    tq_i = q0 + lax.broadcasted_iota(jnp.int32, (tq, 1), 0)

    def per_head(x):
        return jnp.concatenate([x] * rep, axis=0)

    lane_i = lax.broadcasted_iota(jnp.int32, (1, ns), 1)
    ratio = NSA_SEL_BLOCK // NSA_CMP_STRIDE
    n_cmp = t_len // NSA_CMP_STRIDE - 1
    s_list = []
    for r in range(ratio):
        c_i = lane_i * ratio + r
        mask = ((c_i * NSA_CMP_STRIDE + (NSA_CMP_BLOCK - 1)) <= tq_i) & (c_i < n_cmp)
        s_list.append(_dot_nt(qg, kc_ref[0, 0, r]) + per_head(jnp.where(mask, 0.0, NEG_INF)))
    m = jnp.max(functools.reduce(jnp.maximum, s_list), axis=-1, keepdims=True)
    p_list = [jnp.exp2(s - m) for s in s_list]
    l = jnp.sum(functools.reduce(lambda a, b: a + b, p_list), axis=-1, keepdims=True)
    any_visible = per_head(tq_i >= NSA_CMP_BLOCK - 1)
    inv = jnp.where(any_visible, 1.0 / l, 0.0)
    p_list = [p * inv for p in p_list]
    o_c = functools.reduce(lambda a, b: a + b,
                           [_dot(p.astype(BF16), vc_ref[0, 0, r]) for r, p in enumerate(p_list)])

    def head_sum(p):
        return functools.reduce(lambda a, b: a + b, [p[r * tq:(r + 1) * tq] for r in range(rep)])

    ps = [head_sum(p) for p in p_list]
    lane_q = lax.broadcasted_iota(jnp.int32, (tq, ns), 1)
    prev_last = jnp.where(lane_q == 0, 0.0, pltpu.roll(ps[ratio - 1], 1, 1))
    imp = prev_last + ps[0] + ps[1] + ps[2] + ps[3]
    cur =lax.shift_right_logical(tq_i, int(np.log2(NSA_SEL_BLOCK)))
    visible = lane_q * NSA_SEL_BLOCK <= tq_i
    forced = (lane_q == 0) | (lane_q == cur) | (lane_q == cur - 1)
    score = jnp.where(visible, jnp.where(forced, SEL_FORCE, imp), -1.0)

    blk_f = lax.broadcasted_iota(jnp.int32, (ns, tq), 0).astype(F32)
    score_t = score.T
    work = jnp.where(score_t == SEL_FORCE, -2.0, score_t)
    for _ in range(min(NSA_TOP_N, ns) - NSA_FORCED):
        top = jnp.max(work, axis=0, keepdims=True)
        idx = jnp.min(jnp.where(work == top, blk_f, float(ns)), axis=0, keepdims=True)
        work = jnp.where(blk_f == idx, -2.0, work)
    sel = jnp.where((score >= 0.0) & (work.T == -2.0), 1.0, 0.0)

    unsel = ((sel - 1.0) * MASK_BIG).astype(BF16)
    if ns < LANES:
        unsel = jnp.concatenate([unsel, jnp.zeros((tq, LANES - ns), BF16)], axis=1)
    q_aug = jnp.concatenate([qg, jnp.concatenate([unsel] * rep, axis=0)], axis=1)

    def sel_tile(k0, causal):
        s = _dot_nt(q_aug, ks_ref[0, pl.ds(k0, tk), :])
        if causal:
            tok = k0 + lax.broadcasted_iota(jnp.int32, (1, tk), 1)
            s = s + per_head(jnp.where(tok <= tq_i, 0.0, -MASK_BIG))
        return s

    blocks_per_tile = tk // NSA_SEL_BLOCK
    tile_of_blk = lax.shift_right_logical(lax.broadcasted_iota(jnp.int32, (ns, LANES), 0),
                                          int(np.log2(blocks_per_tile)))
    lane_t = lax.broadcasted_iota(jnp.int32, (ns, LANES), 1)
    per_tile = _dot(sel.astype(BF16), jnp.where(tile_of_blk == lane_t, 1.0, 0.0).astype(BF16))
    tile_any = jnp.max(per_tile, axis=0, keepdims=True) > 0.0
    pow2 = lax.shift_left(jnp.ones((1, LANES), jnp.int32), lane_t[:1] & 15).astype(F32)
    tile_bits = jnp.sum(jnp.where(tile_any, pow2, 0.0), axis=-1, keepdims=True).astype(jnp.int32)[0, 0]

    n_past = lax.shift_right_logical(q0, int(np.log2(tk)))
    kd = pl.multiple_of(n_past * tk, tk)
    s = sel_tile(kd, True)
    m_s = jnp.max(s, axis=-1, keepdims=True)
    acc_s = _dot(jnp.exp2(s - m_s).astype(BF16), vs_ref[0, pl.ds(kd, tk), :])

    def sel_update(kt, carry):
        m_run, acc = carry
        k0 = pl.multiple_of(kt * tk, tk)
        s = sel_tile(k0, False)
        m_new = jnp.maximum(m_run, jnp.max(s, axis=-1, keepdims=True))
        acc = jnp.exp2(m_run - m_new) * acc + _dot(jnp.exp2(s - m_new).astype(BF16), vs_ref[0, pl.ds(k0, tk), :])
        return m_new, acc

    def next_active(state):
        return lax.while_loop(lambda st: (st[0] & 1) == 0,
                              lambda st: (lax.shift_right_logical(st[0], 1), st[1] + 1), state)

    def sel_body(state):
        pending, kt = next_active(state[:2])
        return (lax.shift_right_logical(pending, 1), kt + 1) + sel_update(kt, state[2:])

    past_bits = tile_bits & (lax.shift_left(jnp.int32(1), n_past) - 1)
    _, _, _, acc_s = lax.while_loop(lambda st: st[0] != 0, sel_body, (past_bits, jnp.int32(0), m_s, acc_s))
    o_s = acc_s / acc_s[:, HEAD_DIM:HEAD_DIM + 1]

    wlen = NSA_WINDOW + tq
    w0 = pl.multiple_of(jnp.maximum(q0 - NSA_WINDOW, 0), tq)
    dist = tq_i - (w0 + lax.broadcasted_iota(jnp.int32, (1, wlen), 1))
    mask_w = (dist >= 0) & (dist < NSA_WINDOW)
    s_w = _dot_nt(qg, kw_ref[0, pl.ds(w0, wlen), :]) + per_head(jnp.where(mask_w, 0.0, NEG_INF))
    p_w = jnp.exp2(s_w - jnp.max(s_w, axis=-1, keepdims=True))
    o_w = _dot(p_w.astype(BF16), vw_ref[0, pl.ds(w0, wlen), :])
    o_w = o_w / o_w[:, HEAD_DIM:HEAD_DIM + 1]

    gt = gt_ref[0]

    def gate(branch):
        return jnp.concatenate(
            [gt[:, r * NSA_N_BRANCH + branch:r * NSA_N_BRANCH + branch + 1] for r in range(rep)], axis=0)

    o = gate(0) * o_c + gate(1) * o_s + gate(2) * o_w
    low_half = lax.broadcasted_iota(jnp.int32, (tq, LANES), 1) < HEAD_DIM
    for c in range(rep // 2):
        even = o[(2 * c) * tq:(2 * c + 1) * tq]
        odd = pltpu.roll(o[(2 * c + 1) * tq:(2 * c + 2) * tq], HEAD_DIM, 1)
        o_ref[0, :, c * LANES:(c + 1) * LANES] = jnp.where(low_half, even, odd).astype(o_ref.dtype)


def _nsa(q, kc, vc, ks, vs, kw, vw, gates):
    b, t, _ = q.shape
    g = NSA_KV_GROUPS
    ns = t // NSA_SEL_BLOCK
    assert ns <= LANES and t >= NSA_WINDOW + NSA_TQ and t % NSA_TK == 0 and NSA_TK % NSA_TQ == 0
    assert t // NSA_TK <= 16
    ratio = NSA_SEL_BLOCK // NSA_CMP_STRIDE
    qw = NSA_REP * LANES
    cmp_spec = pl.BlockSpec((1, 1, ratio, ns, LANES), lambda bi, gi, i: (bi, gi, 0, 0, 0))
    kv_spec = pl.BlockSpec((1, t, LANES), lambda bi, gi, i: (bi, 0, gi))
    ks_spec = pl.BlockSpec((1, t, 2 * LANES), lambda bi, gi, i: (bi, 0, gi))
    return pl.pallas_call(
        functools.partial(_nsa_kernel, t_len=t, ns=ns),
        out_shape=jax.ShapeDtypeStruct((b, t, GROUP_WIDTH), BF16),
        grid=(b, g, t // NSA_TQ),
        in_specs=[pl.BlockSpec((1, NSA_TQ, qw), lambda bi, gi, i: (bi, i, gi)),
                  cmp_spec, cmp_spec, ks_spec, kv_spec, kv_spec, kv_spec,
                  pl.BlockSpec((1, NSA_TQ, LANES), lambda bi, gi, i: (bi, i, gi))],
        out_specs=pl.BlockSpec((1, NSA_TQ, NSA_REP * HEAD_DIM), lambda bi, gi, i: (bi, i, gi)),
        compiler_params=_params("parallel", "parallel", "arbitrary"),
        name="nsa_attention",
    )(q, kc, vc, ks, vs, kw, vw, gates)


def _sb_tiles(qs, k, v, u2, carries, mask):
    stage1 = []
    for h, qh in enumerate(qs):
        p = h // 2
        z = _dot_nt(qh, k[:, p * LANES:(p + 1) * LANES])
        nk = jnp.maximum(z, 0.0) + jnp.log2(1.0 + jnp.exp2(-jnp.abs(z)))
        if mask is not None:
            nk = jnp.where(mask, nk, 0.0)
        hi = nk.astype(BF16)
        lo = (nk - hi.astype(F32)).astype(BF16)
        stage1.append((z, jnp.concatenate([hi, lo], axis=1)))
    laters = [_dot(hilo, u2) for (_, hilo) in stage1]
    out = []
    for h, ((z, _), later, (acc, c)) in enumerate(zip(stage1, laters, carries)):
        p = h // 2
        a = jnp.exp2(z + later)
        if mask is not None:
            a = jnp.where(mask, a, 0.0)
        acc = acc + jnp.exp2(c) * _dot(a.astype(BF16), v[:, p * LANES:(p + 1) * LANES])
        out.append((acc, c + later[:, 0:1]))
    return tuple(out)


def _sb_kernel(q_ref, k_ref, v_ref, u_ref, o_ref):
    tile = SB_T
    i = pl.program_id(2)
    q2 = q_ref[0]
    u = u_ref[...]
    qs = [q2[:, h * LANES:(h + 1) * LANES] for h in range(2 * SB_PAIRS)]
    q0 = pl.multiple_of(i * tile, tile)
    zero = (jnp.zeros((tile, LANES), F32), jnp.zeros((tile, 1), F32))
    mask = lax.broadcasted_iota(jnp.int32, (tile, tile), 1) < lax.broadcasted_iota(jnp.int32, (tile, tile), 0)
    carry = _sb_tiles(qs, k_ref[0, pl.ds(q0, tile), :], v_ref[0, pl.ds(q0, tile), :], u,
                      tuple(zero for _ in qs), mask)

    def body(jj, carry):
        k0 = pl.multiple_of(jnp.maximum(i - 1 - jj, 0) * tile, tile)
        return _sb_tiles(qs, k_ref[0, pl.ds(k0, tile), :], v_ref[0, pl.ds(k0, tile), :], u, carry, None)

    def alive(carry):
        c_max = functools.reduce(jnp.maximum, [c for _, c in carry])
        return jnp.max(c_max) > SB_DEAD_LOG2

    def step(state):
        jj, _, carry = state
        carry = body(jj, carry)
        return jj + 1, alive(carry), carry

    has_past = i > 0
    off_c = jnp.where(has_past, 0.0, -1e4)
    first = body(0, tuple((acc, c + off_c) for acc, c in carry))
    carry = tuple((acc1, jnp.where(has_past, c1, c0)) for (acc1, c1), (_, c0) in zip(first, carry))
    _, _, carry = lax.while_loop(lambda st: (st[0] < i) & st[1], step, (jnp.int32(1), alive(carry), carry))
    low_half = lax.broadcasted_iota(jnp.int32, (tile, LANES), 1) < HEAD_DIM
    for p in range(SB_PAIRS):
        o_ref[0, :, p * LANES:(p + 1) * LANES] = jnp.where(
            low_half, carry[2 * p][0], carry[2 * p + 1][0]).astype(o_ref.dtype)


def _stick_breaking(q, k, v):
    b, t, _ = q.shape
    tile = SB_T
    assert t % tile == 0
    u = -(np.arange(tile)[:, None] >= np.arange(tile)[None, :]).astype(np.float32)
    u2 = np.concatenate([u, u], axis=0)
    kv_spec = pl.BlockSpec((1, t, SB_PAIRS * LANES), lambda bi, hp, i: (bi, 0, hp))
    return pl.pallas_call(
        _sb_kernel,
        out_shape=jax.ShapeDtypeStruct((b, t, GROUP_WIDTH), BF16),
        grid=(b, SB_HEADS // (2 * SB_PAIRS), t // tile),
        in_specs=[pl.BlockSpec((1, tile, 2 * SB_PAIRS * LANES), lambda bi, hp, i: (bi, i, hp)),
                  kv_spec, kv_spec,
                  pl.BlockSpec((2 * tile, tile), lambda bi, hp, i: (0, 0))],
        out_specs=pl.BlockSpec((1, tile, SB_PAIRS * LANES), lambda bi, hp, i: (bi, i, hp)),
        compiler_params=_params("parallel", "parallel", "arbitrary"),
        name="stick_breaking",
    )(q, k, v, jnp.asarray(u2, BF16))


def _mixer_conv_nsa(x, norm_g, w_in, dw_w, dw_b, ln_g, ln_b, pe_k, w1_k, w2_k, pe_v, w1_v, w2_v):
    a, q, kc, vc, ks, vs, kw, vw, gates = _proj_ab(x, norm_g, w_in, dw_w, dw_b, ln_g, ln_b)
    k_cmp = _compress(kc, pe_k, w1_k, w2_k, dup=False)
    v_cmp = _compress(vc, pe_v, w1_v, w2_v, dup=True)
    return a, _nsa(q, k_cmp, v_cmp, ks, vs, kw, vw, gates)


def _mixer_shortconv_sb(x, norm_g, w_in, sc_w):
    c, q, k, v = _proj_cd(x, norm_g, w_in, sc_w)
    return c, _stick_breaking(q, k, v)


def kernel(x, ffn1_norm, ffn1_w_in, ffn1_w_out, mix_norm, ffn2_norm, ffn2_w_in, ffn2_w_out, ab_w_in, conv_dw_w, conv_dw_b, conv_ln_g, conv_ln_b, nsa_pe_k, nsa_w1_k, nsa_w2_k, nsa_pe_v, nsa_w1_v, nsa_w2_v, ab_w_out, cd_w_in, sc_conv_w, cd_w_out, final_norm):
    b, t, d = x.shape
    depth = ffn1_norm.shape[0]
    n = b * t
    ffn1_w = (ffn1_w_in.astype(BF16), ffn1_w_out.astype(BF16))
    ffn2_w = (ffn2_w_in.astype(BF16), ffn2_w_out.astype(BF16))
    for layer in range(depth):
        x = _ffn(x.reshape(n, d), ffn1_norm[layer], *ffn1_w, layer).reshape(b, t, d)
        if layer % 2 == 0:
            e = layer // 2
            left, right = _mixer_conv_nsa(x, mix_norm[layer], ab_w_in[e], conv_dw_w[e], conv_dw_b[e], conv_ln_g[e],
                                          conv_ln_b[e], nsa_pe_k[e], nsa_w1_k[e], nsa_w2_k[e],
                                          nsa_pe_v[e], nsa_w1_v[e], nsa_w2_v[e])
            w_mix = ab_w_out[e]
        else:
            o = layer // 2
            left, right = _mixer_shortconv_sb(x, mix_norm[layer], cd_w_in[o], sc_conv_w[o])
            w_mix = cd_w_out[o]
        last = layer == depth - 1
        x = _ffn(x.reshape(n, d), ffn2_norm[layer], *ffn2_w, layer,
                 mixer=(left.reshape(n, -1), right.reshape(n, -1), w_mix.astype(BF16)),
                 final_g=final_norm if last else None).reshape(b, t, d)
    return x
```

```python
import functools

import numpy as np
import jax
import jax.numpy as jnp
from jax import lax
from jax.experimental import pallas as pl
from jax.experimental.pallas import tpu as pltpu

F32 = jnp.float32
BF16 = jnp.bfloat16

D_MODEL = 1024
HEAD_DIM = 64
GROUP_WIDTH = D_MODEL // 2
CONV_WIDTH = 31
NSA_HEADS = GROUP_WIDTH // HEAD_DIM
NSA_KV_GROUPS = 2
NSA_REP = NSA_HEADS // NSA_KV_GROUPS
NSA_CMP_BLOCK = 32
NSA_CMP_STRIDE = 16
NSA_CMP_HIDDEN = 128
NSA_SEL_BLOCK = 64
NSA_TOP_N = 16
NSA_FORCED = 3
NSA_WINDOW = 512
NSA_N_BRANCH = 3
SC_WIDTH = 3
SB_HEADS = GROUP_WIDTH // HEAD_DIM
D_FF = 2816
RMS_EPS = 1e-6
LN_EPS = 1e-5
NEG_INF = -1e30
SEL_FORCE = 1e4
QK_SCALE = HEAD_DIM ** -0.5
LOG2_E = float(np.log2(np.e))

LANES = 128
SUBLANES = 8
VMEM_LIMIT = 48 * 1024 * 1024
FFN_VMEM_LIMIT = 56 * 1024 * 1024

FFN_TM = 1024
FFN_TF = 256
PROJ_TM = 512
CONV_HALO = 32
CONV_CHUNK = 64
SC_HALO = 8
NSA_TQ = 256
NSA_TK = 512
SB_T = 256
SB_PAIRS = 2
SB_DEAD_LOG2 = -160.0


def _params(*sem, vmem_limit=VMEM_LIMIT):
    return pltpu.CompilerParams(dimension_semantics=sem, vmem_limit_bytes=vmem_limit)


def _dot(a, b):
    return jnp.dot(a, b, preferred_element_type=F32)


def _dot_nt(a, b):
    return lax.dot_general(a, b, (((1,), (1,)), ((), ())), preferred_element_type=F32)


def _sigmoid(x):
    return 1.0 / (1.0 + jnp.exp(-x))


def _rmsnorm_rows(x, g):
    return x * lax.rsqrt(jnp.mean(x * x, axis=-1, keepdims=True) + RMS_EPS) * g


def _ffn_kernel(x_ref, g_ref, wi_ref, wo_ref, *rest, n_ff, mixer_out, final_norm):
    rest = list(rest)
    x = x_ref[...]
    if mixer_out:
        a_ref, m_ref, wm_ref = rest[:3]
        rest = rest[3:]
        half = a_ref.shape[1]
        x = x + _dot(a_ref[...], wm_ref[:half, :]) + _dot(m_ref[...], wm_ref[half:, :])
    if final_norm:
        fg_ref, o_ref = rest
    else:
        (o_ref,) = rest
    xn = _rmsnorm_rows(x, g_ref[...]).astype(BF16)
    acc = None
    for j in range(n_ff):
        gate = _dot(xn, wi_ref[:, j * FFN_TF:(j + 1) * FFN_TF])
        up = _dot(xn, wi_ref[:, D_FF + j * FFN_TF:D_FF + (j + 1) * FFN_TF])
        h = gate * _sigmoid(gate) * up
        part = _dot(h.astype(BF16), wo_ref[j * FFN_TF:(j + 1) * FFN_TF, :].astype(BF16))
        acc = part if acc is None else acc + part
    y = x + 0.5 * acc
    if final_norm:
        y = _rmsnorm_rows(y, fg_ref[...])
    o_ref[...] = y


def _ffn(x, g, w_in, w_out, layer, mixer=None, final_g=None):
    n, d = x.shape
    n_ff = D_FF // FFN_TF
    tm = min(FFN_TM, n)
    once = pl.Buffered(1)
    in_specs = [
        pl.BlockSpec((tm, d), lambda i: (i, 0)),
        pl.BlockSpec((1, d), lambda i: (0, 0)),
        pl.BlockSpec((None, d, 2 * D_FF), lambda i: (layer, 0, 0), pipeline_mode=once),
        pl.BlockSpec((None, D_FF, d), lambda i: (layer, 0, 0), pipeline_mode=once),
    ]
    args = [x, g.reshape(1, d), w_in, w_out]
    if mixer is not None:
        a, o, w_mix = mixer
        in_specs += [pl.BlockSpec((tm, a.shape[1]), lambda i: (i, 0)), pl.BlockSpec((tm, o.shape[1]), lambda i: (i, 0)),
                     pl.BlockSpec((d, d), lambda i: (0, 0), pipeline_mode=once)]
        args += [a, o, w_mix]
    if final_g is not None:
        in_specs.append(pl.BlockSpec((1, d), lambda i: (0, 0)))
        args.append(final_g.reshape(1, d))
    return pl.pallas_call(
        functools.partial(_ffn_kernel, n_ff=n_ff, mixer_out=mixer is not None, final_norm=final_g is not None),
        out_shape=jax.ShapeDtypeStruct((n, d), F32),
        grid=(n // tm,),
        in_specs=in_specs,
        out_specs=pl.BlockSpec((tm, d), lambda i: (i, 0)),
        compiler_params=_params("parallel", vmem_limit=FFN_VMEM_LIMIT),
        name="ffn",
    )(*args)


AB_A = 0
AB_Q = 1024
AB_KC = 2048
AB_VC = 2176
AB_KS = 2304
AB_VS = 2560
AB_KW = 2816
AB_VW = 3072
AB_G = 3328
AB_COLS = 3584

POS_SPLIT = 3
POS_TOK = HEAD_DIM
POS_CMP = POS_TOK + 2 * POS_SPLIT
MASK_BIG = 2.0 ** 100


def _pos_lanes(hi, lo, base):
    lane = lax.broadcasted_iota(jnp.int32, (1, LANES), 1)
    in_hi = (lane >= base) & (lane < base + POS_SPLIT)
    in_lo = (lane >= base + POS_SPLIT) & (lane < base + 2 * POS_SPLIT)
    return jnp.where(in_hi, hi, jnp.where(in_lo, lo, 0.0))


def _proj_ab_kernel(x_ref, g_ref, w_ref, qb_ref, dw_ref, db_ref, lg_ref, lb_ref,
                    a_ref, q_ref, kc_ref, vc_ref, ks_ref, vs_ref, kw_ref, vw_ref, gt_ref, ext_ref, shift_ref, *, tm):
    xn = _rmsnorm_rows(x_ref[0], g_ref[...]).astype(BF16)

    def seg(lo, hi):
        return _dot(xn, w_ref[:, lo:hi])

    @pl.when(pl.program_id(1) == 0)
    def _():
        ext_ref[0:CONV_HALO, :] = jnp.zeros((CONV_HALO, GROUP_WIDTH), F32)

    av = seg(AB_A, AB_A + 2 * GROUP_WIDTH)
    ext_ref[CONV_HALO:, :] = av[:, :GROUP_WIDTH] * _sigmoid(av[:, GROUP_WIDTH:])
    _conformer_conv_rows(ext_ref, shift_ref, dw_ref, db_ref, lg_ref, lb_ref, a_ref, tm)
    ext_ref[0:CONV_HALO, :] = ext_ref[tm:tm + CONV_HALO, :]
    q_ref[0] = (seg(AB_Q, AB_KC) + qb_ref[...]).astype(BF16)
    kc_ref[0] = seg(AB_KC, AB_VC)
    vc_ref[0] = seg(AB_VC, AB_KS)
    one_lane = jnp.where(lax.broadcasted_iota(jnp.int32, (1, NSA_KV_GROUPS * LANES), 1) % LANES == HEAD_DIM, 1.0, 0.0)
    vs_ref[0] = (seg(AB_VS, AB_KW) + one_lane).astype(BF16)
    vw_ref[0] = (seg(AB_VW, AB_G) + one_lane).astype(BF16)
    gt_ref[0] = _sigmoid(seg(AB_G, AB_COLS))

    tok = pl.program_id(1) * tm + lax.broadcasted_iota(jnp.int32, (tm, 1), 0)
    blk = lax.shift_right_logical(tok, int(np.log2(NSA_SEL_BLOCK)))
    pos = _pos_lanes(blk.astype(F32), (tok & (NSA_SEL_BLOCK - 1)).astype(F32), POS_TOK)
    onehot = jnp.where(lax.broadcasted_iota(jnp.int32, (1, LANES), 1) == blk, 1.0, 0.0).astype(BF16)
    ks = seg(AB_KS, AB_VS)
    kw = seg(AB_KW, AB_VW)
    for g in range(NSA_KV_GROUPS):
        ks_ref[0, :, 2 * g * LANES:(2 * g + 1) * LANES] = (ks[:, g * LANES:(g + 1) * LANES] + pos).astype(BF16)
        ks_ref[0, :, (2 * g + 1) * LANES:(2 * g + 2) * LANES] = onehot
        kw_ref[0, :, g * LANES:(g + 1) * LANES] = (kw[:, g * LANES:(g + 1) * LANES] + pos).astype(BF16)


def _arrange_ab_weight(w):
    d = w.shape[0]
    kvw = NSA_KV_GROUPS * HEAD_DIM
    o = 2 * GROUP_WIDTH
    a = w[:, :o]
    q = w[:, o:o + GROUP_WIDTH].reshape(d, NSA_HEADS, HEAD_DIM) * (QK_SCALE * LOG2_E)
    o += GROUP_WIDTH
    kc, vc, ks, vs, kw, vw = [w[:, o + i * kvw:o + (i + 1) * kvw] for i in range(6)]
    o += 6 * kvw
    g = w[:, o:].reshape(d, NSA_KV_GROUPS, NSA_REP * NSA_N_BRANCH)

    zeros_h = jnp.zeros((d, NSA_HEADS, HEAD_DIM), w.dtype)
    q_pad = jnp.concatenate([q, zeros_h], axis=-1).reshape(d, NSA_HEADS * LANES)

    def k_pad(k):
        k = k.reshape(d, NSA_KV_GROUPS, HEAD_DIM)
        return jnp.concatenate([k, jnp.zeros_like(k)], axis=-1).reshape(d, NSA_KV_GROUPS * LANES)

    g_pad = jnp.pad(g, ((0, 0), (0, 0), (0, LANES - g.shape[-1]))).reshape(d, NSA_KV_GROUPS * LANES)
    out = jnp.concatenate([a, q_pad, kc, vc, k_pad(ks), k_pad(vs), k_pad(kw), k_pad(vw), g_pad], axis=-1)
    assert out.shape[1] == AB_COLS
    return out.astype(BF16)


def _bf16_terms(x, n):
    terms, rest = [], np.asarray(x, np.float64)
    for _ in range(n):
        term = rest.astype(BF16).astype(np.float64)
        terms.append(term)
        rest = rest - term
    return terms


def _alibi_query_lanes():
    row = np.zeros((NSA_HEADS, LANES), np.float64)
    for h in range(NSA_HEADS):
        slope = 2.0 ** (-8.0 * (h + 1) / NSA_HEADS)
        for i, term in enumerate(_bf16_terms(slope * np.log2(np.e), POS_SPLIT)):
            row[h, POS_TOK + i] = NSA_SEL_BLOCK * term
            row[h, POS_TOK + POS_SPLIT + i] = term
            row[h, POS_CMP + i] = 2 * NSA_CMP_STRIDE * term
            row[h, POS_CMP + POS_SPLIT + i] = NSA_CMP_STRIDE * term
    return jnp.asarray(row.reshape(1, NSA_HEADS * LANES), F32)


def _proj_ab(x, g, w, dw_w, dw_b, ln_g, ln_b):
    b, t, d = x.shape
    tm = min(PROJ_TM, t)
    wa = _arrange_ab_weight(w)

    def tok(width):
        return pl.BlockSpec((1, tm, width), lambda bi, i: (bi, i, 0))

    const = lambda width: pl.BlockSpec((1, width), lambda bi, i: (0, 0))
    once = pl.Buffered(1)
    gw = GROUP_WIDTH
    kv = NSA_KV_GROUPS * LANES
    widths = [gw, NSA_HEADS * LANES, LANES, LANES, 2 * kv, kv, kv, kv, kv]
    dtypes = [BF16, BF16, F32, F32, BF16, BF16, BF16, BF16, F32]
    return pl.pallas_call(
        functools.partial(_proj_ab_kernel, tm=tm),
        out_shape=[jax.ShapeDtypeStruct((b, t, wd), dt) for wd, dt in zip(widths, dtypes)],
        grid=(b, t // tm),
        in_specs=[tok(d), const(d), pl.BlockSpec((d, AB_COLS), lambda bi, i: (0, 0), pipeline_mode=once),
                  const(NSA_HEADS * LANES), pl.BlockSpec((CONV_WIDTH, gw), lambda bi, i: (0, 0)),
                  const(gw), const(gw), const(gw)],
        out_specs=[tok(wd) for wd in widths],
        scratch_shapes=[pltpu.VMEM((tm + CONV_HALO, gw), F32), pltpu.VMEM((SUBLANES, tm + CONV_HALO, gw), F32)],
        compiler_params=_params("arbitrary", "arbitrary"),
        name="proj_ab",
    )(x, g.reshape(1, d), wa, _alibi_query_lanes(), dw_w, dw_b.reshape(1, gw), ln_g.reshape(1, gw),
      ln_b.reshape(1, gw))


CD_B = 0
CD_C = 512
CD_U = 1024
CD_Q = 1536
CD_K = 2560
CD_V = 3072
CD_COLS = 3584


def _proj_cd_kernel(x_ref, g_ref, w_ref, scw_ref, c_ref, q_ref, k_ref, v_ref, ext_ref, *, tm):
    xn = _rmsnorm_rows(x_ref[0], g_ref[...]).astype(BF16)

    def seg(lo, hi):
        return _dot(xn, w_ref[:, lo:hi])

    q_ref[0] = seg(CD_Q, CD_K).astype(BF16)
    k_ref[0] = seg(CD_K, CD_V).astype(BF16)
    v_ref[0] = seg(CD_V, CD_COLS).astype(BF16)

    @pl.when(pl.program_id(1) == 0)
    def _():
        ext_ref[0:SC_HALO, :] = jnp.zeros((SC_HALO, GROUP_WIDTH), F32)

    ext_ref[SC_HALO:, :] = seg(CD_C, CD_U) * seg(CD_U, CD_Q)
    off = SC_HALO - (SC_WIDTH - 1)
    acc = jnp.zeros((tm, GROUP_WIDTH), F32)
    for k in range(SC_WIDTH):
        acc = acc + scw_ref[k:k + 1, :] * ext_ref[off + k:off + k + tm, :]
    c_ref[0] = (seg(CD_B, CD_C) * acc).astype(c_ref.dtype)
    ext_ref[0:SC_HALO, :] = ext_ref[tm:tm + SC_HALO, :]


def _arrange_cd_weight(w):
    d = w.shape[0]
    gw = GROUP_WIDTH
    q = w[:, 3 * gw:4 * gw].reshape(d, SB_HEADS // 2, 2, HEAD_DIM) * (QK_SCALE * LOG2_E)
    z = jnp.zeros((d, SB_HEADS // 2, HEAD_DIM), w.dtype)
    q_even = jnp.concatenate([q[:, :, 0], z], axis=-1)
    q_odd = jnp.concatenate([z, q[:, :, 1]], axis=-1)
    q_pad = jnp.stack([q_even, q_odd], axis=2).reshape(d, SB_HEADS * LANES)
    out = jnp.concatenate([w[:, :3 * gw], q_pad, w[:, 4 * gw:]], axis=-1)
    assert out.shape[1] == CD_COLS
    return out.astype(BF16)


def _proj_cd(x, g, w, sc_w):
    b, t, d = x.shape
    tm = min(PROJ_TM, t)
    wa = _arrange_cd_weight(w)

    def tok(width):
        return pl.BlockSpec((1, tm, width), lambda bi, i: (bi, i, 0))

    widths = [GROUP_WIDTH, SB_HEADS * LANES, GROUP_WIDTH, GROUP_WIDTH]
    return pl.pallas_call(
        functools.partial(_proj_cd_kernel, tm=tm),
        out_shape=[jax.ShapeDtypeStruct((b, t, wd), BF16) for wd in widths],
        grid=(b, t // tm),
        in_specs=[tok(d), pl.BlockSpec((1, d), lambda bi, i: (0, 0)),
                  pl.BlockSpec((d, CD_COLS), lambda bi, i: (0, 0)),
                  pl.BlockSpec((SC_WIDTH, GROUP_WIDTH), lambda bi, i: (0, 0))],
        out_specs=[tok(wd) for wd in widths],
        scratch_shapes=[pltpu.VMEM((tm + SC_HALO, GROUP_WIDTH), F32)],
        compiler_params=_params("arbitrary", "arbitrary"),
        name="proj_cd",
    )(x, g.reshape(1, d), wa, sc_w)


def _conformer_conv_rows(ext_ref, shift_ref, w_ref, b_ref, lg_ref, lb_ref, o_ref, tt):
    off = CONV_HALO - (CONV_WIDTH - 1)
    last = off + CONV_WIDTH - 1
    steps = {phase: [m for m in range((last - phase) // SUBLANES + 1) if off <= SUBLANES * m + phase]
             for phase in range(SUBLANES)}
    for phase in range(SUBLANES):
        rows = SUBLANES * steps[phase][-1] + tt
        shift_ref[phase, 0:rows, :] = ext_ref[phase:phase + rows, :]
    for c in range(tt // CONV_CHUNK):
        r0 = c * CONV_CHUNK
        acc = jnp.zeros((CONV_CHUNK, GROUP_WIDTH), F32)
        for phase in range(SUBLANES):
            for m in steps[phase]:
                k = SUBLANES * m + phase - off
                lo = r0 + SUBLANES * m
                acc = acc + w_ref[k:k + 1, :] * shift_ref[phase, lo:lo + CONV_CHUNK, :]
        y = acc + b_ref[...]
        mu = jnp.mean(y, axis=-1, keepdims=True)
        yc = y - mu
        var = jnp.mean(yc * yc, axis=-1, keepdims=True)
        yn = yc * lax.rsqrt(var + LN_EPS) * lg_ref[...] + lb_ref[...]
        o_ref[0, r0:r0 + CONV_CHUNK, :] = (yn * _sigmoid(yn)).astype(o_ref.dtype)


def _gelu_tanh(x):
    return 0.5 * x * (1.0 + jnp.tanh(np.sqrt(2.0 / np.pi).astype(np.float32) * (x + 0.044715 * (x * x * x))))


def _compress_kernel(x_ref, pe_ref, w1_ref, w2_ref, o_ref, y_ref, *, nblk, keys):
    stride = NSA_CMP_STRIDE

    def partial_sum(l0):
        acc = None
        for l in range(l0, l0 + stride):
            xl = x_ref[0, pl.ds(l - l0, nblk, stride=stride), :] + pe_ref[l:l + 1, :]
            part = _dot(xl.astype(BF16), w1_ref[l])
            acc = part if acc is None else acc + part
        return acc

    h = partial_sum(0) + pltpu.roll(partial_sum(stride), nblk - 1, 0)
    row = lax.broadcasted_iota(jnp.int32, (nblk, 1), 0)
    ratio = NSA_SEL_BLOCK // NSA_CMP_STRIDE
    for g in range(NSA_KV_GROUPS):
        y = _dot(_gelu_tanh(h[:, g * NSA_CMP_HIDDEN:(g + 1) * NSA_CMP_HIDDEN]).astype(BF16), w2_ref[...])
        y = jnp.where(row < nblk - 1, y, 0.0)
        if keys:
            y = y + _pos_lanes(lax.shift_right_logical(row, 1).astype(F32), (row & 1).astype(F32), POS_CMP)
        y_ref[...] = y
        for r in range(ratio):
            o_ref[0, g, r] = y_ref[pl.ds(r, nblk // ratio, stride=ratio), :].astype(o_ref.dtype)


def _compress(kx, pe, w1, w2, dup):
    b, t, width = kx.shape
    g = NSA_KV_GROUPS
    nblk = t // NSA_CMP_STRIDE
    ratio = NSA_SEL_BLOCK // NSA_CMP_STRIDE
    hid = NSA_CMP_HIDDEN
    zeros = jnp.zeros_like(w1)
    w1_bd = jnp.concatenate([jnp.concatenate([w1, zeros], axis=-1), jnp.concatenate([zeros, w1], axis=-1)],
                            axis=1).astype(BF16)
    w2p = jnp.concatenate([w2, w2 if dup else jnp.zeros_like(w2)], axis=-1).astype(BF16)
    return pl.pallas_call(
        functools.partial(_compress_kernel, nblk=nblk, keys=not dup),
        out_shape=jax.ShapeDtypeStruct((b, g, ratio, nblk // ratio, LANES), BF16),
        grid=(b,),
        in_specs=[pl.BlockSpec((1, t, width), lambda bi: (bi, 0, 0)),
                  pl.BlockSpec((NSA_CMP_BLOCK, width), lambda bi: (0, 0)),
                  pl.BlockSpec((NSA_CMP_BLOCK, width, g * hid), lambda bi: (0, 0, 0)),
                  pl.BlockSpec((hid, LANES), lambda bi: (0, 0))],
        out_specs=pl.BlockSpec((1, g, ratio, nblk // ratio, LANES), lambda bi: (bi, 0, 0, 0, 0)),
        scratch_shapes=[pltpu.VMEM((nblk, LANES), F32)],
        compiler_params=_params("parallel"),
        name="nsa_compress",
    )(kx, jnp.tile(pe, (1, g)), w1_bd, w2p)


def _nsa_kernel(q_ref, kc_ref, vc_ref, ks_ref, vs_ref, kw_ref, vw_ref, gt_ref, o_ref, *, t_len, ns):
    tq, tk, rep = NSA_TQ, NSA_TK, NSA_REP
    rows = rep * tq
    i = pl.program_id(2)
    q0 = i * tq

    q2 = q_ref[0]
    qg = jnp.concatenate([q2[:, r * LANES:(r + 1) * LANES] for r in range(rep)], axis=0)
    tq_i = q0 + lax.broadcasted_iota(jnp.int32, (tq, 1), 0)

    def per_head(x):
        return jnp.concatenate([x] * rep, axis=0)

    lane_i = lax.broadcasted_iota(jnp.int32, (1, ns), 1)
    ratio = NSA_SEL_BLOCK // NSA_CMP_STRIDE
    n_cmp = t_len // NSA_CMP_STRIDE - 1
    s_list = []
    for r in range(ratio):
        c_i = lane_i * ratio + r
        mask = ((c_i * NSA_CMP_STRIDE + (NSA_CMP_BLOCK - 1)) <= tq_i) & (c_i < n_cmp)
        s_list.append(_dot_nt(qg, kc_ref[0, 0, r]) + per_head(jnp.where(mask, 0.0, NEG_INF)))
    m = jnp.max(functools.reduce(jnp.maximum, s_list), axis=-1, keepdims=True)
    p_list = [jnp.exp2(s - m) for s in s_list]
    l = jnp.sum(functools.reduce(lambda a, b: a + b, p_list), axis=-1, keepdims=True)
    any_visible = per_head(tq_i >= NSA_CMP_BLOCK - 1)
    inv = jnp.where(any_visible, 1.0 / l, 0.0)
    p_list = [p * inv for p in p_list]
    o_c = functools.reduce(lambda a, b: a + b,
                           [_dot(p.astype(BF16), vc_ref[0, 0, r]) for r, p in enumerate(p_list)])

    def head_sum(p):
        return functools.reduce(lambda a, b: a + b, [p[r * tq:(r + 1) * tq] for r in range(rep)])

    ps = [head_sum(p) for p in p_list]
    lane_q = lax.broadcasted_iota(jnp.int32, (tq, ns), 1)
    prev_last = jnp.where(lane_q == 0, 0.0, pltpu.roll(ps[ratio - 1], 1, 1))
    imp = prev_last + ps[0] + ps[1] + ps[2] + ps[3]
    cur =lax.shift_right_logical(tq_i, int(np.log2(NSA_SEL_BLOCK)))
    visible = lane_q * NSA_SEL_BLOCK <= tq_i
    forced = (lane_q == 0) | (lane_q == cur) | (lane_q == cur - 1)
    score = jnp.where(visible, jnp.where(forced, SEL_FORCE, imp), -1.0)

    blk_f = lax.broadcasted_iota(jnp.int32, (ns, tq), 0).astype(F32)
    score_t = score.T
    work = jnp.where(score_t == SEL_FORCE, -2.0, score_t)
    for _ in range(min(NSA_TOP_N, ns) - NSA_FORCED):
        top = jnp.max(work, axis=0, keepdims=True)
        idx = jnp.min(jnp.where(work == top, blk_f, float(ns)), axis=0, keepdims=True)
        work = jnp.where(blk_f == idx, -2.0, work)
    sel = jnp.where((score >= 0.0) & (work.T == -2.0), 1.0, 0.0)

    unsel = ((sel - 1.0) * MASK_BIG).astype(BF16)
    if ns < LANES:
        unsel = jnp.concatenate([unsel, jnp.zeros((tq, LANES - ns), BF16)], axis=1)
    q_aug = jnp.concatenate([qg, jnp.concatenate([unsel] * rep, axis=0)], axis=1)

    def sel_tile(k0, causal):
        s = _dot_nt(q_aug, ks_ref[0, pl.ds(k0, tk), :])
        if causal:
            tok = k0 + lax.broadcasted_iota(jnp.int32, (1, tk), 1)
            s = s + per_head(jnp.where(tok <= tq_i, 0.0, -MASK_BIG))
        return s

    blocks_per_tile = tk // NSA_SEL_BLOCK
    tile_of_blk = lax.shift_right_logical(lax.broadcasted_iota(jnp.int32, (ns, LANES), 0),
                                          int(np.log2(blocks_per_tile)))
    lane_t = lax.broadcasted_iota(jnp.int32, (ns, LANES), 1)
    per_tile = _dot(sel.astype(BF16), jnp.where(tile_of_blk == lane_t, 1.0, 0.0).astype(BF16))
    tile_any = jnp.max(per_tile, axis=0, keepdims=True) > 0.0
    pow2 = lax.shift_left(jnp.ones((1, LANES), jnp.int32), lane_t[:1] & 15).astype(F32)
    tile_bits = jnp.sum(jnp.where(tile_any, pow2, 0.0), axis=-1, keepdims=True).astype(jnp.int32)[0, 0]

    n_past = lax.shift_right_logical(q0, int(np.log2(tk)))
    kd = pl.multiple_of(n_past * tk, tk)
    s = sel_tile(kd, True)
    m_s = jnp.max(s, axis=-1, keepdims=True)
    acc_s = _dot(jnp.exp2(s - m_s).astype(BF16), vs_ref[0, pl.ds(kd, tk), :])

    def sel_update(kt, carry):
        m_run, acc = carry
        k0 = pl.multiple_of(kt * tk, tk)
        s = sel_tile(k0, False)
        m_new = jnp.maximum(m_run, jnp.max(s, axis=-1, keepdims=True))
        acc = jnp.exp2(m_run - m_new) * acc + _dot(jnp.exp2(s - m_new).astype(BF16), vs_ref[0, pl.ds(k0, tk), :])
        return m_new, acc

    def next_active(state):
        return lax.while_loop(lambda st: (st[0] & 1) == 0,
                              lambda st: (lax.shift_right_logical(st[0], 1), st[1] + 1), state)

    def sel_body(state):
        pending, kt = next_active(state[:2])
        return (lax.shift_right_logical(pending, 1), kt + 1) + sel_update(kt, state[2:])

    past_bits = tile_bits & (lax.shift_left(jnp.int32(1), n_past) - 1)
    _, _, _, acc_s = lax.while_loop(lambda st: st[0] != 0, sel_body, (past_bits, jnp.int32(0), m_s, acc_s))
    o_s = acc_s / acc_s[:, HEAD_DIM:HEAD_DIM + 1]

    wlen = NSA_WINDOW + tq
    w0 = pl.multiple_of(jnp.maximum(q0 - NSA_WINDOW, 0), tq)
    dist = tq_i - (w0 + lax.broadcasted_iota(jnp.int32, (1, wlen), 1))
    mask_w = (dist >= 0) & (dist < NSA_WINDOW)
    s_w = _dot_nt(qg, kw_ref[0, pl.ds(w0, wlen), :]) + per_head(jnp.where(mask_w, 0.0, NEG_INF))
    p_w = jnp.exp2(s_w - jnp.max(s_w, axis=-1, keepdims=True))
    o_w = _dot(p_w.astype(BF16), vw_ref[0, pl.ds(w0, wlen), :])
    o_w = o_w / o_w[:, HEAD_DIM:HEAD_DIM + 1]

    gt = gt_ref[0]

    def gate(branch):
        return jnp.concatenate(
            [gt[:, r * NSA_N_BRANCH + branch:r * NSA_N_BRANCH + branch + 1] for r in range(rep)], axis=0)

    o = gate(0) * o_c + gate(1) * o_s + gate(2) * o_w
    low_half = lax.broadcasted_iota(jnp.int32, (tq, LANES), 1) < HEAD_DIM
    for c in range(rep // 2):
        even = o[(2 * c) * tq:(2 * c + 1) * tq]
        odd = pltpu.roll(o[(2 * c + 1) * tq:(2 * c + 2) * tq], HEAD_DIM, 1)
        o_ref[0, :, c * LANES:(c + 1) * LANES] = jnp.where(low_half, even, odd).astype(o_ref.dtype)


def _nsa(q, kc, vc, ks, vs, kw, vw, gates):
    b, t, _ = q.shape
    g = NSA_KV_GROUPS
    ns = t // NSA_SEL_BLOCK
    assert ns <= LANES and t >= NSA_WINDOW + NSA_TQ and t % NSA_TK == 0 and NSA_TK % NSA_TQ == 0
    assert t // NSA_TK <= 16
    ratio = NSA_SEL_BLOCK // NSA_CMP_STRIDE
    qw = NSA_REP * LANES
    cmp_spec = pl.BlockSpec((1, 1, ratio, ns, LANES), lambda bi, gi, i: (bi, gi, 0, 0, 0))
    kv_spec = pl.BlockSpec((1, t, LANES), lambda bi, gi, i: (bi, 0, gi))
    ks_spec = pl.BlockSpec((1, t, 2 * LANES), lambda bi, gi, i: (bi, 0, gi))
    return pl.pallas_call(
        functools.partial(_nsa_kernel, t_len=t, ns=ns),
        out_shape=jax.ShapeDtypeStruct((b, t, GROUP_WIDTH), BF16),
        grid=(b, g, t // NSA_TQ),
        in_specs=[pl.BlockSpec((1, NSA_TQ, qw), lambda bi, gi, i: (bi, i, gi)),
                  cmp_spec, cmp_spec, ks_spec, kv_spec, kv_spec, kv_spec,
                  pl.BlockSpec((1, NSA_TQ, LANES), lambda bi, gi, i: (bi, i, gi))],
        out_specs=pl.BlockSpec((1, NSA_TQ, NSA_REP * HEAD_DIM), lambda bi, gi, i: (bi, i, gi)),
        compiler_params=_params("parallel", "parallel", "arbitrary"),
        name="nsa_attention",
    )(q, kc, vc, ks, vs, kw, vw, gates)


def _sb_tiles(qs, k, v, u2, carries, mask):
    stage1 = []
    for h, qh in enumerate(qs):
        p = h // 2
        z = _dot_nt(qh, k[:, p * LANES:(p + 1) * LANES])
        nk = jnp.maximum(z, 0.0) + jnp.log2(1.0 + jnp.exp2(-jnp.abs(z)))
        if mask is not None:
            nk = jnp.where(mask, nk, 0.0)
        hi = nk.astype(BF16)
        lo = (nk - hi.astype(F32)).astype(BF16)
        stage1.append((z, jnp.concatenate([hi, lo], axis=1)))
    laters = [_dot(hilo, u2) for (_, hilo) in stage1]
    out = []
    for h, ((z, _), later, (acc, c)) in enumerate(zip(stage1, laters, carries)):
        p = h // 2
        a = jnp.exp2(z + later)
        if mask is not None:
            a = jnp.where(mask, a, 0.0)
        acc = acc + jnp.exp2(c) * _dot(a.astype(BF16), v[:, p * LANES:(p + 1) * LANES])
        out.append((acc, c + later[:, 0:1]))
    return tuple(out)


def _sb_kernel(q_ref, k_ref, v_ref, u_ref, o_ref):
    tile = SB_T
    i = pl.program_id(2)
    q2 = q_ref[0]
    u = u_ref[...]
    qs = [q2[:, h * LANES:(h + 1) * LANES] for h in range(2 * SB_PAIRS)]
    q0 = pl.multiple_of(i * tile, tile)
    zero = (jnp.zeros((tile, LANES), F32), jnp.zeros((tile, 1), F32))
    mask = lax.broadcasted_iota(jnp.int32, (tile, tile), 1) < lax.broadcasted_iota(jnp.int32, (tile, tile), 0)
    carry = _sb_tiles(qs, k_ref[0, pl.ds(q0, tile), :], v_ref[0, pl.ds(q0, tile), :], u,
                      tuple(zero for _ in qs), mask)

    def body(jj, carry):
        k0 = pl.multiple_of(jnp.maximum(i - 1 - jj, 0) * tile, tile)
        return _sb_tiles(qs, k_ref[0, pl.ds(k0, tile), :], v_ref[0, pl.ds(k0, tile), :], u, carry, None)

    def alive(carry):
        c_max = functools.reduce(jnp.maximum, [c for _, c in carry])
        return jnp.max(c_max) > SB_DEAD_LOG2

    def step(state):
        jj, _, carry = state
        carry = body(jj, carry)
        return jj + 1, alive(carry), carry

    has_past = i > 0
    off_c = jnp.where(has_past, 0.0, -1e4)
    first = body(0, tuple((acc, c + off_c) for acc, c in carry))
    carry = tuple((acc1, jnp.where(has_past, c1, c0)) for (acc1, c1), (_, c0) in zip(first, carry))
    _, _, carry = lax.while_loop(lambda st: (st[0] < i) & st[1], step, (jnp.int32(1), alive(carry), carry))
    low_half = lax.broadcasted_iota(jnp.int32, (tile, LANES), 1) < HEAD_DIM
    for p in range(SB_PAIRS):
        o_ref[0, :, p * LANES:(p + 1) * LANES] = jnp.where(
            low_half, carry[2 * p][0], carry[2 * p + 1][0]).astype(o_ref.dtype)


def _stick_breaking(q, k, v):
    b, t, _ = q.shape
    tile = SB_T
    assert t % tile == 0
    u = -(np.arange(tile)[:, None] >= np.arange(tile)[None, :]).astype(np.float32)
    u2 = np.concatenate([u, u], axis=0)
    kv_spec = pl.BlockSpec((1, t, SB_PAIRS * LANES), lambda bi, hp, i: (bi, 0, hp))
    return pl.pallas_call(
        _sb_kernel,
        out_shape=jax.ShapeDtypeStruct((b, t, GROUP_WIDTH), BF16),
        grid=(b, SB_HEADS // (2 * SB_PAIRS), t // tile),
        in_specs=[pl.BlockSpec((1, tile, 2 * SB_PAIRS * LANES), lambda bi, hp, i: (bi, i, hp)),
                  kv_spec, kv_spec,
                  pl.BlockSpec((2 * tile, tile), lambda bi, hp, i: (0, 0))],
        out_specs=pl.BlockSpec((1, tile, SB_PAIRS * LANES), lambda bi, hp, i: (bi, i, hp)),
        compiler_params=_params("parallel", "parallel", "arbitrary"),
        name="stick_breaking",
    )(q, k, v, jnp.asarray(u2, BF16))


def _mixer_conv_nsa(x, norm_g, w_in, dw_w, dw_b, ln_g, ln_b, pe_k, w1_k, w2_k, pe_v, w1_v, w2_v):
    a, q, kc, vc, ks, vs, kw, vw, gates = _proj_ab(x, norm_g, w_in, dw_w, dw_b, ln_g, ln_b)
    k_cmp = _compress(kc, pe_k, w1_k, w2_k, dup=False)
    v_cmp = _compress(vc, pe_v, w1_v, w2_v, dup=True)
    return a, _nsa(q, k_cmp, v_cmp, ks, vs, kw, vw, gates)


def _mixer_shortconv_sb(x, norm_g, w_in, sc_w):
    c, q, k, v = _proj_cd(x, norm_g, w_in, sc_w)
    return c, _stick_breaking(q, k, v)


def kernel(x, ffn1_norm, ffn1_w_in, ffn1_w_out, mix_norm, ffn2_norm, ffn2_w_in, ffn2_w_out, ab_w_in, conv_dw_w, conv_dw_b, conv_ln_g, conv_ln_b, nsa_pe_k, nsa_w1_k, nsa_w2_k, nsa_pe_v, nsa_w1_v, nsa_w2_v, ab_w_out, cd_w_in, sc_conv_w, cd_w_out, final_norm):
    b, t, d = x.shape
    depth = ffn1_norm.shape[0]
    n = b * t
    ffn1_w = (ffn1_w_in.astype(BF16), ffn1_w_out)
    ffn2_w = (ffn2_w_in.astype(BF16), ffn2_w_out)
    for layer in range(depth):
        x = _ffn(x.reshape(n, d), ffn1_norm[layer], *ffn1_w, layer).reshape(b, t, d)
        if layer % 2 == 0:
            e = layer // 2
            left, right = _mixer_conv_nsa(x, mix_norm[layer], ab_w_in[e], conv_dw_w[e], conv_dw_b[e], conv_ln_g[e],
                                          conv_ln_b[e], nsa_pe_k[e], nsa_w1_k[e], nsa_w2_k[e],
                                          nsa_pe_v[e], nsa_w1_v[e], nsa_w2_v[e])
            w_mix = ab_w_out[e]
        else:
            o = layer // 2
            left, right = _mixer_shortconv_sb(x, mix_norm[layer], cd_w_in[o], sc_conv_w[o])
            w_mix = cd_w_out[o]
        last = layer == depth - 1
        x = _ffn(x.reshape(n, d), ffn2_norm[layer], *ffn2_w, layer,
                 mixer=(left.reshape(n, -1), right.reshape(n, -1), w_mix.astype(BF16)),
                 final_g=final_norm if last else None).reshape(b, t, d)
    return x
```

```python
import functools

import numpy as np
import jax
import jax.numpy as jnp
from jax import lax
from jax.experimental import pallas as pl
from jax.experimental.pallas import tpu as pltpu

F32 = jnp.float32
BF16 = jnp.bfloat16

D_MODEL = 1024
HEAD_DIM = 64
GROUP_WIDTH = D_MODEL // 2
CONV_WIDTH = 31
NSA_HEADS = GROUP_WIDTH // HEAD_DIM
NSA_KV_GROUPS = 2
NSA_REP = NSA_HEADS // NSA_KV_GROUPS
NSA_CMP_BLOCK = 32
NSA_CMP_STRIDE = 16
NSA_CMP_HIDDEN = 128
NSA_SEL_BLOCK = 64
NSA_TOP_N = 16
NSA_FORCED = 3
NSA_WINDOW = 512
NSA_N_BRANCH = 3
SC_WIDTH = 3
SB_HEADS = GROUP_WIDTH // HEAD_DIM
D_FF = 2816
RMS_EPS = 1e-6
LN_EPS = 1e-5
NEG_INF = -1e30
SEL_FORCE = 1e4
QK_SCALE = HEAD_DIM ** -0.5
LOG2_E = float(np.log2(np.e))

LANES = 128
SUBLANES = 8
VMEM_LIMIT = 48 * 1024 * 1024
FFN_VMEM_LIMIT = 56 * 1024 * 1024

FFN_TM = 1024
FFN_TF = 256
PROJ_TM = 512
CONV_HALO = 32
CONV_CHUNK = 64
SC_HALO = 8
NSA_TQ = 256
NSA_TK = 512
SB_T = 256
SB_PAIRS = 2
SB_DEAD_LOG2 = -160.0


def _params(*sem, vmem_limit=VMEM_LIMIT):
    return pltpu.CompilerParams(dimension_semantics=sem, vmem_limit_bytes=vmem_limit)


def _dot(a, b):
    return jnp.dot(a, b, preferred_element_type=F32)


def _dot_nt(a, b):
    return lax.dot_general(a, b, (((1,), (1,)), ((), ())), preferred_element_type=F32)


def _sigmoid(x):
    return 1.0 / (1.0 + jnp.exp(-x))


def _rmsnorm_rows(x, g):
    return x * lax.rsqrt(jnp.mean(x * x, axis=-1, keepdims=True) + RMS_EPS) * g


def _ffn_kernel(x_ref, g_ref, wi_ref, wo_ref, *rest, n_ff, mixer_out, final_norm):
    rest = list(rest)
    x = x_ref[...]
    if mixer_out:
        a_ref, m_ref, wm_ref = rest[:3]
        rest = rest[3:]
        half = a_ref.shape[1]
        x = x + _dot(a_ref[...], wm_ref[:half, :]) + _dot(m_ref[...], wm_ref[half:, :])
    if final_norm:
        fg_ref, o_ref = rest
    else:
        (o_ref,) = rest
    xn = _rmsnorm_rows(x, g_ref[...]).astype(BF16)
    acc = None
    for j in range(n_ff):
        gate = _dot(xn, wi_ref[:, j * FFN_TF:(j + 1) * FFN_TF])
        up = _dot(xn, wi_ref[:, D_FF + j * FFN_TF:D_FF + (j + 1) * FFN_TF])
        h = gate * _sigmoid(gate) * up
        part = _dot(h.astype(BF16), wo_ref[j * FFN_TF:(j + 1) * FFN_TF, :].astype(BF16))
        acc = part if acc is None else acc + part
    y = x + 0.5 * acc
    if final_norm:
        y = _rmsnorm_rows(y, fg_ref[...])
    o_ref[...] = y


def _ffn(x, g, w_in, w_out, layer, mixer=None, final_g=None):
    n, d = x.shape
    n_ff = D_FF // FFN_TF
    tm = min(FFN_TM, n)
    once = pl.Buffered(1)
    in_specs = [
        pl.BlockSpec((tm, d), lambda i: (i, 0)),
        pl.BlockSpec((1, d), lambda i: (0, 0)),
        pl.BlockSpec((None, d, 2 * D_FF), lambda i: (layer, 0, 0), pipeline_mode=once),
        pl.BlockSpec((None, D_FF, d), lambda i: (layer, 0, 0), pipeline_mode=once),
    ]
    args = [x, g.reshape(1, d), w_in, w_out]
    if mixer is not None:
        a, o, w_mix = mixer
        in_specs += [pl.BlockSpec((tm, a.shape[1]), lambda i: (i, 0)), pl.BlockSpec((tm, o.shape[1]), lambda i: (i, 0)),
                     pl.BlockSpec((d, d), lambda i: (0, 0), pipeline_mode=once)]
        args += [a, o, w_mix]
    if final_g is not None:
        in_specs.append(pl.BlockSpec((1, d), lambda i: (0, 0)))
        args.append(final_g.reshape(1, d))
    return pl.pallas_call(
        functools.partial(_ffn_kernel, n_ff=n_ff, mixer_out=mixer is not None, final_norm=final_g is not None),
        out_shape=jax.ShapeDtypeStruct((n, d), F32),
        grid=(n // tm,),
        in_specs=in_specs,
        out_specs=pl.BlockSpec((tm, d), lambda i: (i, 0)),
        compiler_params=_params("parallel", vmem_limit=FFN_VMEM_LIMIT),
        name="ffn",
    )(*args)


AB_A = 0
AB_Q = 1024
AB_KC = 2048
AB_VC = 2176
AB_KS = 2304
AB_VS = 2560
AB_KW = 2816
AB_VW = 3072
AB_G = 3328
AB_COLS = 3584

POS_SPLIT = 3
POS_TOK = HEAD_DIM
POS_CMP = POS_TOK + 2 * POS_SPLIT
MASK_BIG = 2.0 ** 100


def _pos_lanes(hi, lo, base):
    lane = lax.broadcasted_iota(jnp.int32, (1, LANES), 1)
    in_hi = (lane >= base) & (lane < base + POS_SPLIT)
    in_lo = (lane >= base + POS_SPLIT) & (lane < base + 2 * POS_SPLIT)
    return jnp.where(in_hi, hi, jnp.where(in_lo, lo, 0.0))


def _proj_ab_kernel(x_ref, g_ref, w_ref, qb_ref, dw_ref, db_ref, lg_ref, lb_ref,
                    a_ref, q_ref, kc_ref, vc_ref, ks_ref, vs_ref, kw_ref, vw_ref, gt_ref, ext_ref, shift_ref, *, tm):
    xn = _rmsnorm_rows(x_ref[0], g_ref[...]).astype(BF16)

    def seg(lo, hi):
        return _dot(xn, w_ref[:, lo:hi])

    @pl.when(pl.program_id(1) == 0)
    def _():
        ext_ref[0:CONV_HALO, :] = jnp.zeros((CONV_HALO, GROUP_WIDTH), F32)

    av = seg(AB_A, AB_A + 2 * GROUP_WIDTH)
    ext_ref[CONV_HALO:, :] = av[:, :GROUP_WIDTH] * _sigmoid(av[:, GROUP_WIDTH:])
    _conformer_conv_rows(ext_ref, shift_ref, dw_ref, db_ref, lg_ref, lb_ref, a_ref, tm)
    ext_ref[0:CONV_HALO, :] = ext_ref[tm:tm + CONV_HALO, :]
    q_ref[0] = (seg(AB_Q, AB_KC) + qb_ref[...]).astype(BF16)
    kc_ref[0] = seg(AB_KC, AB_VC)
    vc_ref[0] = seg(AB_VC, AB_KS)
    one_lane = jnp.where(lax.broadcasted_iota(jnp.int32, (1, NSA_KV_GROUPS * LANES), 1) % LANES == HEAD_DIM, 1.0, 0.0)
    vs_ref[0] = (seg(AB_VS, AB_KW) + one_lane).astype(BF16)
    vw_ref[0] = (seg(AB_VW, AB_G) + one_lane).astype(BF16)
    gt_ref[0] = _sigmoid(seg(AB_G, AB_COLS))

    tok = pl.program_id(1) * tm + lax.broadcasted_iota(jnp.int32, (tm, 1), 0)
    blk = lax.shift_right_logical(tok, int(np.log2(NSA_SEL_BLOCK)))
    pos = _pos_lanes(blk.astype(F32), (tok & (NSA_SEL_BLOCK - 1)).astype(F32), POS_TOK)
    onehot = jnp.where(lax.broadcasted_iota(jnp.int32, (1, LANES), 1) == blk, 1.0, 0.0).astype(BF16)
    ks = seg(AB_KS, AB_VS)
    kw = seg(AB_KW, AB_VW)
    for g in range(NSA_KV_GROUPS):
        ks_ref[0, :, 2 * g * LANES:(2 * g + 1) * LANES] = (ks[:, g * LANES:(g + 1) * LANES] + pos).astype(BF16)
        ks_ref[0, :, (2 * g + 1) * LANES:(2 * g + 2) * LANES] = onehot
        kw_ref[0, :, g * LANES:(g + 1) * LANES] = (kw[:, g * LANES:(g + 1) * LANES] + pos).astype(BF16)


def _arrange_ab_weight(w):
    d = w.shape[0]
    kvw = NSA_KV_GROUPS * HEAD_DIM
    o = 2 * GROUP_WIDTH
    a = w[:, :o]
    q = w[:, o:o + GROUP_WIDTH].reshape(d, NSA_HEADS, HEAD_DIM) * (QK_SCALE * LOG2_E)
    o += GROUP_WIDTH
    kc, vc, ks, vs, kw, vw = [w[:, o + i * kvw:o + (i + 1) * kvw] for i in range(6)]
    o += 6 * kvw
    g = w[:, o:].reshape(d, NSA_KV_GROUPS, NSA_REP * NSA_N_BRANCH)

    zeros_h = jnp.zeros((d, NSA_HEADS, HEAD_DIM), w.dtype)
    q_pad = jnp.concatenate([q, zeros_h], axis=-1).reshape(d, NSA_HEADS * LANES)

    def k_pad(k):
        k = k.reshape(d, NSA_KV_GROUPS, HEAD_DIM)
        return jnp.concatenate([k, jnp.zeros_like(k)], axis=-1).reshape(d, NSA_KV_GROUPS * LANES)

    g_pad = jnp.pad(g, ((0, 0), (0, 0), (0, LANES - g.shape[-1]))).reshape(d, NSA_KV_GROUPS * LANES)
    out = jnp.concatenate([a, q_pad, kc, vc, k_pad(ks), k_pad(vs), k_pad(kw), k_pad(vw), g_pad], axis=-1)
    assert out.shape[1] == AB_COLS
    return out.astype(BF16)


def _bf16_terms(x, n):
    terms, rest = [], np.asarray(x, np.float64)
    for _ in range(n):
        term = rest.astype(BF16).astype(np.float64)
        terms.append(term)
        rest = rest - term
    return terms


def _alibi_query_lanes():
    row = np.zeros((NSA_HEADS, LANES), np.float64)
    for h in range(NSA_HEADS):
        slope = 2.0 ** (-8.0 * (h + 1) / NSA_HEADS)
        for i, term in enumerate(_bf16_terms(slope * np.log2(np.e), POS_SPLIT)):
            row[h, POS_TOK + i] = NSA_SEL_BLOCK * term
            row[h, POS_TOK + POS_SPLIT + i] = term
            row[h, POS_CMP + i] = 2 * NSA_CMP_STRIDE * term
            row[h, POS_CMP + POS_SPLIT + i] = NSA_CMP_STRIDE * term
    return jnp.asarray(row.reshape(1, NSA_HEADS * LANES), F32)


def _proj_ab(x, g, w, dw_w, dw_b, ln_g, ln_b):
    b, t, d = x.shape
    tm = min(PROJ_TM, t)
    wa = _arrange_ab_weight(w)

    def tok(width):
        return pl.BlockSpec((1, tm, width), lambda bi, i: (bi, i, 0))

    const = lambda width: pl.BlockSpec((1, width), lambda bi, i: (0, 0))
    once = pl.Buffered(1)
    gw = GROUP_WIDTH
    kv = NSA_KV_GROUPS * LANES
    widths = [gw, NSA_HEADS * LANES, LANES, LANES, 2 * kv, kv, kv, kv, kv]
    dtypes = [BF16, BF16, F32, F32, BF16, BF16, BF16, BF16, F32]
    return pl.pallas_call(
        functools.partial(_proj_ab_kernel, tm=tm),
        out_shape=[jax.ShapeDtypeStruct((b, t, wd), dt) for wd, dt in zip(widths, dtypes)],
        grid=(b, t // tm),
        in_specs=[tok(d), const(d), pl.BlockSpec((d, AB_COLS), lambda bi, i: (0, 0), pipeline_mode=once),
                  const(NSA_HEADS * LANES), pl.BlockSpec((CONV_WIDTH, gw), lambda bi, i: (0, 0)),
                  const(gw), const(gw), const(gw)],
        out_specs=[tok(wd) for wd in widths],
        scratch_shapes=[pltpu.VMEM((tm + CONV_HALO, gw), F32), pltpu.VMEM((SUBLANES, tm + CONV_HALO, gw), F32)],
        compiler_params=_params("arbitrary", "arbitrary"),
        name="proj_ab",
    )(x, g.reshape(1, d), wa, _alibi_query_lanes(), dw_w, dw_b.reshape(1, gw), ln_g.reshape(1, gw),
      ln_b.reshape(1, gw))


CD_B = 0
CD_C = 512
CD_U = 1024
CD_Q = 1536
CD_K = 2048
CD_V = 2560
CD_COLS = 3072


def _proj_cd_kernel(x_ref, g_ref, w_ref, scw_ref, c_ref, q_ref, k_ref, v_ref, ext_ref, *, tm):
    xn = _rmsnorm_rows(x_ref[0], g_ref[...]).astype(BF16)

    def seg(lo, hi):
        return _dot(xn, w_ref[:, lo:hi])

    q_ref[0] = seg(CD_Q, CD_K).astype(BF16)
    k_ref[0] = seg(CD_K, CD_V).astype(BF16)
    v_ref[0] = seg(CD_V, CD_COLS).astype(BF16)

    @pl.when(pl.program_id(1) == 0)
    def _():
        ext_ref[0:SC_HALO, :] = jnp.zeros((SC_HALO, GROUP_WIDTH), F32)

    ext_ref[SC_HALO:, :] = seg(CD_C, CD_U) * seg(CD_U, CD_Q)
    off = SC_HALO - (SC_WIDTH - 1)
    acc = jnp.zeros((tm, GROUP_WIDTH), F32)
    for k in range(SC_WIDTH):
        acc = acc + scw_ref[k:k + 1, :] * ext_ref[off + k:off + k + tm, :]
    c_ref[0] = (seg(CD_B, CD_C) * acc).astype(c_ref.dtype)
    ext_ref[0:SC_HALO, :] = ext_ref[tm:tm + SC_HALO, :]


def _arrange_cd_weight(w):
    gw = GROUP_WIDTH
    out = jnp.concatenate([w[:, :3 * gw], w[:, 3 * gw:4 * gw] * (QK_SCALE * LOG2_E), w[:, 4 * gw:]], axis=-1)
    assert out.shape[1] == CD_COLS
    return out.astype(BF16)


def _proj_cd(x, g, w, sc_w):
    b, t, d = x.shape
    tm = min(PROJ_TM, t)
    wa = _arrange_cd_weight(w)

    def tok(width):
        return pl.BlockSpec((1, tm, width), lambda bi, i: (bi, i, 0))

    widths = [GROUP_WIDTH] * 4
    return pl.pallas_call(
        functools.partial(_proj_cd_kernel, tm=tm),
        out_shape=[jax.ShapeDtypeStruct((b, t, wd), BF16) for wd in widths],
        grid=(b, t // tm),
        in_specs=[tok(d), pl.BlockSpec((1, d), lambda bi, i: (0, 0)),
                  pl.BlockSpec((d, CD_COLS), lambda bi, i: (0, 0)),
                  pl.BlockSpec((SC_WIDTH, GROUP_WIDTH), lambda bi, i: (0, 0))],
        out_specs=[tok(wd) for wd in widths],
        scratch_shapes=[pltpu.VMEM((tm + SC_HALO, GROUP_WIDTH), F32)],
        compiler_params=_params("arbitrary", "arbitrary"),
        name="proj_cd",
    )(x, g.reshape(1, d), wa, sc_w)


def _conformer_conv_rows(ext_ref, shift_ref, w_ref, b_ref, lg_ref, lb_ref, o_ref, tt):
    off = CONV_HALO - (CONV_WIDTH - 1)
    last = off + CONV_WIDTH - 1
    steps = {phase: [m for m in range((last - phase) // SUBLANES + 1) if off <= SUBLANES * m + phase]
             for phase in range(SUBLANES)}
    for phase in range(SUBLANES):
        rows = SUBLANES * steps[phase][-1] + tt
        shift_ref[phase, 0:rows, :] = ext_ref[phase:phase + rows, :]
    for c in range(tt // CONV_CHUNK):
        r0 = c * CONV_CHUNK
        acc = jnp.zeros((CONV_CHUNK, GROUP_WIDTH), F32)
        for phase in range(SUBLANES):
            for m in steps[phase]:
                k = SUBLANES * m + phase - off
                lo = r0 + SUBLANES * m
                acc = acc + w_ref[k:k + 1, :] * shift_ref[phase, lo:lo + CONV_CHUNK, :]
        y = acc + b_ref[...]
        mu = jnp.mean(y, axis=-1, keepdims=True)
        yc = y - mu
        var = jnp.mean(yc * yc, axis=-1, keepdims=True)
        yn = yc * lax.rsqrt(var + LN_EPS) * lg_ref[...] + lb_ref[...]
        o_ref[0, r0:r0 + CONV_CHUNK, :] = (yn * _sigmoid(yn)).astype(o_ref.dtype)


def _gelu_tanh(x):
    return 0.5 * x * (1.0 + jnp.tanh(np.sqrt(2.0 / np.pi).astype(np.float32) * (x + 0.044715 * (x * x * x))))


def _compress_kernel(x_ref, pe_ref, w1_ref, w2_ref, o_ref, y_ref, *, nblk, keys):
    stride = NSA_CMP_STRIDE

    def partial_sum(l0):
        acc = None
        for l in range(l0, l0 + stride):
            xl = x_ref[0, pl.ds(l - l0, nblk, stride=stride), :] + pe_ref[l:l + 1, :]
            part = _dot(xl.astype(BF16), w1_ref[l])
            acc = part if acc is None else acc + part
        return acc

    h = partial_sum(0) + pltpu.roll(partial_sum(stride), nblk - 1, 0)
    row = lax.broadcasted_iota(jnp.int32, (nblk, 1), 0)
    ratio = NSA_SEL_BLOCK // NSA_CMP_STRIDE
    for g in range(NSA_KV_GROUPS):
        y = _dot(_gelu_tanh(h[:, g * NSA_CMP_HIDDEN:(g + 1) * NSA_CMP_HIDDEN]).astype(BF16), w2_ref[...])
        y = jnp.where(row < nblk - 1, y, 0.0)
        if keys:
            y = y + _pos_lanes(lax.shift_right_logical(row, 1).astype(F32), (row & 1).astype(F32), POS_CMP)
        y_ref[...] = y
        for r in range(ratio):
            o_ref[0, g, r] = y_ref[pl.ds(r, nblk // ratio, stride=ratio), :].astype(o_ref.dtype)


def _compress(kx, pe, w1, w2, dup):
    b, t, width = kx.shape
    g = NSA_KV_GROUPS
    nblk = t // NSA_CMP_STRIDE
    ratio = NSA_SEL_BLOCK // NSA_CMP_STRIDE
    hid = NSA_CMP_HIDDEN
    zeros = jnp.zeros_like(w1)
    w1_bd = jnp.concatenate([jnp.concatenate([w1, zeros], axis=-1), jnp.concatenate([zeros, w1], axis=-1)],
                            axis=1).astype(BF16)
    w2p = jnp.concatenate([w2, w2 if dup else jnp.zeros_like(w2)], axis=-1).astype(BF16)
    return pl.pallas_call(
        functools.partial(_compress_kernel, nblk=nblk, keys=not dup),
        out_shape=jax.ShapeDtypeStruct((b, g, ratio, nblk // ratio, LANES), BF16),
        grid=(b,),
        in_specs=[pl.BlockSpec((1, t, width), lambda bi: (bi, 0, 0)),
                  pl.BlockSpec((NSA_CMP_BLOCK, width), lambda bi: (0, 0)),
                  pl.BlockSpec((NSA_CMP_BLOCK, width, g * hid), lambda bi: (0, 0, 0)),
                  pl.BlockSpec((hid, LANES), lambda bi: (0, 0))],
        out_specs=pl.BlockSpec((1, g, ratio, nblk // ratio, LANES), lambda bi: (bi, 0, 0, 0, 0)),
        scratch_shapes=[pltpu.VMEM((nblk, LANES), F32)],
        compiler_params=_params("parallel"),
        name="nsa_compress",
    )(kx, jnp.tile(pe, (1, g)), w1_bd, w2p)


def _nsa_kernel(q_ref, kc_ref, vc_ref, ks_ref, vs_ref, kw_ref, vw_ref, gt_ref, o_ref, *, t_len, ns):
    tq, tk, rep = NSA_TQ, NSA_TK, NSA_REP
    rows = rep * tq
    i = pl.program_id(2)
    q0 = i * tq

    q2 = q_ref[0]
    qg = jnp.concatenate([q2[:, r * LANES:(r + 1) * LANES] for r in range(rep)], axis=0)
    tq_i = q0 + lax.broadcasted_iota(jnp.int32, (tq, 1), 0)

    def per_head(x):
        return jnp.concatenate([x] * rep, axis=0)

    lane_i = lax.broadcasted_iota(jnp.int32, (1, ns), 1)
    ratio = NSA_SEL_BLOCK // NSA_CMP_STRIDE
    n_cmp = t_len // NSA_CMP_STRIDE - 1
    s_list = []
    for r in range(ratio):
        c_i = lane_i * ratio + r
        mask = ((c_i * NSA_CMP_STRIDE + (NSA_CMP_BLOCK - 1)) <= tq_i) & (c_i < n_cmp)
        s_list.append(_dot_nt(qg, kc_ref[0, 0, r]) + per_head(jnp.where(mask, 0.0, NEG_INF)))
    m = jnp.max(functools.reduce(jnp.maximum, s_list), axis=-1, keepdims=True)
    p_list = [jnp.exp2(s - m) for s in s_list]
    l = jnp.sum(functools.reduce(lambda a, b: a + b, p_list), axis=-1, keepdims=True)
    any_visible = per_head(tq_i >= NSA_CMP_BLOCK - 1)
    inv = jnp.where(any_visible, 1.0 / l, 0.0)
    p_list = [p * inv for p in p_list]
    o_c = functools.reduce(lambda a, b: a + b,
                           [_dot(p.astype(BF16), vc_ref[0, 0, r]) for r, p in enumerate(p_list)])

    def head_sum(p):
        return functools.reduce(lambda a, b: a + b, [p[r * tq:(r + 1) * tq] for r in range(rep)])

    ps = [head_sum(p) for p in p_list]
    lane_q = lax.broadcasted_iota(jnp.int32, (tq, ns), 1)
    prev_last = jnp.where(lane_q == 0, 0.0, pltpu.roll(ps[ratio - 1], 1, 1))
    imp = prev_last + ps[0] + ps[1] + ps[2] + ps[3]
    cur =lax.shift_right_logical(tq_i, int(np.log2(NSA_SEL_BLOCK)))
    visible = lane_q * NSA_SEL_BLOCK <= tq_i
    forced = (lane_q == 0) | (lane_q == cur) | (lane_q == cur - 1)
    score = jnp.where(visible, jnp.where(forced, SEL_FORCE, imp), -1.0)

    blk_f = lax.broadcasted_iota(jnp.int32, (ns, tq), 0).astype(F32)
    score_t = score.T
    work = jnp.where(score_t == SEL_FORCE, -2.0, score_t)
    for _ in range(min(NSA_TOP_N, ns) - NSA_FORCED):
        top = jnp.max(work, axis=0, keepdims=True)
        idx = jnp.min(jnp.where(work == top, blk_f, float(ns)), axis=0, keepdims=True)
        work = jnp.where(blk_f == idx, -2.0, work)
    sel = jnp.where((score >= 0.0) & (work.T == -2.0), 1.0, 0.0)

    unsel = ((sel - 1.0) * MASK_BIG).astype(BF16)
    if ns < LANES:
        unsel = jnp.concatenate([unsel, jnp.zeros((tq, LANES - ns), BF16)], axis=1)
    q_aug = jnp.concatenate([qg, jnp.concatenate([unsel] * rep, axis=0)], axis=1)

    def sel_tile(k0, causal):
        s = _dot_nt(q_aug, ks_ref[0, pl.ds(k0, tk), :])
        if causal:
            tok = k0 + lax.broadcasted_iota(jnp.int32, (1, tk), 1)
            s = s + per_head(jnp.where(tok <= tq_i, 0.0, -MASK_BIG))
        return s

    blocks_per_tile = tk // NSA_SEL_BLOCK
    tile_of_blk = lax.shift_right_logical(lax.broadcasted_iota(jnp.int32, (ns, LANES), 0),
                                          int(np.log2(blocks_per_tile)))
    lane_t = lax.broadcasted_iota(jnp.int32, (ns, LANES), 1)
    per_tile = _dot(sel.astype(BF16), jnp.where(tile_of_blk == lane_t, 1.0, 0.0).astype(BF16))
    tile_any = jnp.max(per_tile, axis=0, keepdims=True) > 0.0
    pow2 = lax.shift_left(jnp.ones((1, LANES), jnp.int32), lane_t[:1] & 15).astype(F32)
    tile_bits = jnp.sum(jnp.where(tile_any, pow2, 0.0), axis=-1, keepdims=True).astype(jnp.int32)[0, 0]

    n_past = lax.shift_right_logical(q0, int(np.log2(tk)))
    kd = pl.multiple_of(n_past * tk, tk)
    s = sel_tile(kd, True)
    m_s = jnp.max(s, axis=-1, keepdims=True)
    acc_s = _dot(jnp.exp2(s - m_s).astype(BF16), vs_ref[0, pl.ds(kd, tk), :])

    def sel_update(kt, carry):
        m_run, acc = carry
        k0 = pl.multiple_of(kt * tk, tk)
        s = sel_tile(k0, False)
        m_new = jnp.maximum(m_run, jnp.max(s, axis=-1, keepdims=True))
        acc = jnp.exp2(m_run - m_new) * acc + _dot(jnp.exp2(s - m_new).astype(BF16), vs_ref[0, pl.ds(k0, tk), :])
        return m_new, acc

    def next_active(state):
        return lax.while_loop(lambda st: (st[0] & 1) == 0,
                              lambda st: (lax.shift_right_logical(st[0], 1), st[1] + 1), state)

    def sel_body(state):
        pending, kt = next_active(state[:2])
        return (lax.shift_right_logical(pending, 1), kt + 1) + sel_update(kt, state[2:])

    past_bits = tile_bits & (lax.shift_left(jnp.int32(1), n_past) - 1)
    _, _, _, acc_s = lax.while_loop(lambda st: st[0] != 0, sel_body, (past_bits, jnp.int32(0), m_s, acc_s))
    o_s = acc_s / acc_s[:, HEAD_DIM:HEAD_DIM + 1]

    wlen = NSA_WINDOW + tq
    w0 = pl.multiple_of(jnp.maximum(q0 - NSA_WINDOW, 0), tq)
    dist = tq_i - (w0 + lax.broadcasted_iota(jnp.int32, (1, wlen), 1))
    mask_w = (dist >= 0) & (dist < NSA_WINDOW)
    s_w = _dot_nt(qg, kw_ref[0, pl.ds(w0, wlen), :]) + per_head(jnp.where(mask_w, 0.0, NEG_INF))
    p_w = jnp.exp2(s_w - jnp.max(s_w, axis=-1, keepdims=True))
    o_w = _dot(p_w.astype(BF16), vw_ref[0, pl.ds(w0, wlen), :])
    o_w = o_w / o_w[:, HEAD_DIM:HEAD_DIM + 1]

    gt = gt_ref[0]

    def gate(branch):
        return jnp.concatenate(
            [gt[:, r * NSA_N_BRANCH + branch:r * NSA_N_BRANCH + branch + 1] for r in range(rep)], axis=0)

    o = gate(0) * o_c + gate(1) * o_s + gate(2) * o_w
    low_half = lax.broadcasted_iota(jnp.int32, (tq, LANES), 1) < HEAD_DIM
    for c in range(rep // 2):
        even = o[(2 * c) * tq:(2 * c + 1) * tq]
        odd = pltpu.roll(o[(2 * c + 1) * tq:(2 * c + 2) * tq], HEAD_DIM, 1)
        o_ref[0, :, c * LANES:(c + 1) * LANES] = jnp.where(low_half, even, odd).astype(o_ref.dtype)


def _nsa(q, kc, vc, ks, vs, kw, vw, gates):
    b, t, _ = q.shape
    g = NSA_KV_GROUPS
    ns = t // NSA_SEL_BLOCK
    assert ns <= LANES and t >= NSA_WINDOW + NSA_TQ and t % NSA_TK == 0 and NSA_TK % NSA_TQ == 0
    assert t // NSA_TK <= 16
    ratio = NSA_SEL_BLOCK // NSA_CMP_STRIDE
    qw = NSA_REP * LANES
    cmp_spec = pl.BlockSpec((1, 1, ratio, ns, LANES), lambda bi, gi, i: (bi, gi, 0, 0, 0))
    kv_spec = pl.BlockSpec((1, t, LANES), lambda bi, gi, i: (bi, 0, gi))
    ks_spec = pl.BlockSpec((1, t, 2 * LANES), lambda bi, gi, i: (bi, 0, gi))
    return pl.pallas_call(
        functools.partial(_nsa_kernel, t_len=t, ns=ns),
        out_shape=jax.ShapeDtypeStruct((b, t, GROUP_WIDTH), BF16),
        grid=(b, g, t // NSA_TQ),
        in_specs=[pl.BlockSpec((1, NSA_TQ, qw), lambda bi, gi, i: (bi, i, gi)),
                  cmp_spec, cmp_spec, ks_spec, kv_spec, kv_spec, kv_spec,
                  pl.BlockSpec((1, NSA_TQ, LANES), lambda bi, gi, i: (bi, i, gi))],
        out_specs=pl.BlockSpec((1, NSA_TQ, NSA_REP * HEAD_DIM), lambda bi, gi, i: (bi, i, gi)),
        compiler_params=_params("parallel", "parallel", "arbitrary"),
        name="nsa_attention",
    )(q, kc, vc, ks, vs, kw, vw, gates)


def _sb_tiles(qs, k, v, u2, carries, mask):
    stage1 = []
    for h, qh in enumerate(qs):
        p = h // 2
        z = _dot_nt(qh, k[:, p * LANES:(p + 1) * LANES])
        nk = jnp.maximum(z, 0.0) + jnp.log2(1.0 + jnp.exp2(-jnp.abs(z)))
        if mask is not None:
            nk = jnp.where(mask, nk, 0.0)
        hi = nk.astype(BF16)
        lo = (nk - hi.astype(F32)).astype(BF16)
        stage1.append((z, jnp.concatenate([hi, lo], axis=1)))
    laters = [_dot(hilo, u2) for (_, hilo) in stage1]
    out = []
    for h, ((z, _), later, (acc, c)) in enumerate(zip(stage1, laters, carries)):
        p = h // 2
        a = jnp.exp2(z + later)
        if mask is not None:
            a = jnp.where(mask, a, 0.0)
        acc = acc + jnp.exp2(c) * _dot(a.astype(BF16), v[:, p * LANES:(p + 1) * LANES])
        out.append((acc, c + later[:, 0:1]))
    return tuple(out)


def _sb_kernel(q_ref, k_ref, v_ref, u_ref, o_ref):
    tile = SB_T
    i = pl.program_id(2)
    q2 = q_ref[0]
    u = u_ref[...]
    low_lanes = lax.broadcasted_iota(jnp.int32, (1, LANES), 1) < HEAD_DIM
    qs = [jnp.where(low_lanes if h % 2 == 0 else ~low_lanes, q2[:, (h // 2) * LANES:(h // 2 + 1) * LANES],
                    jnp.zeros((), BF16))
          for h in range(2 * SB_PAIRS)]
    q0 = pl.multiple_of(i * tile, tile)
    zero = (jnp.zeros((tile, LANES), F32), jnp.zeros((tile, 1), F32))
    mask = lax.broadcasted_iota(jnp.int32, (tile, tile), 1) < lax.broadcasted_iota(jnp.int32, (tile, tile), 0)
    carry = _sb_tiles(qs, k_ref[0, pl.ds(q0, tile), :], v_ref[0, pl.ds(q0, tile), :], u,
                      tuple(zero for _ in qs), mask)

    def body(jj, carry):
        k0 = pl.multiple_of(jnp.maximum(i - 1 - jj, 0) * tile, tile)
        return _sb_tiles(qs, k_ref[0, pl.ds(k0, tile), :], v_ref[0, pl.ds(k0, tile), :], u, carry, None)

    def alive(carry):
        c_max = functools.reduce(jnp.maximum, [c for _, c in carry])
        return jnp.max(c_max) > SB_DEAD_LOG2

    def step(state):
        jj, _, carry = state
        carry = body(jj, carry)
        return jj + 1, alive(carry), carry

    has_past = i > 0
    off_c = jnp.where(has_past, 0.0, -1e4)
    first = body(0, tuple((acc, c + off_c) for acc, c in carry))
    carry = tuple((acc1, jnp.where(has_past, c1, c0)) for (acc1, c1), (_, c0) in zip(first, carry))
    _, _, carry = lax.while_loop(lambda st: (st[0] < i) & st[1], step, (jnp.int32(1), alive(carry), carry))
    low_half = lax.broadcasted_iota(jnp.int32, (tile, LANES), 1) < HEAD_DIM
    for p in range(SB_PAIRS):
        o_ref[0, :, p * LANES:(p + 1) * LANES] = jnp.where(
            low_half, carry[2 * p][0], carry[2 * p + 1][0]).astype(o_ref.dtype)


def _stick_breaking(q, k, v):
    b, t, _ = q.shape
    tile = SB_T
    assert t % tile == 0
    u = -(np.arange(tile)[:, None] >= np.arange(tile)[None, :]).astype(np.float32)
    u2 = np.concatenate([u, u], axis=0)
    kv_spec = pl.BlockSpec((1, t, SB_PAIRS * LANES), lambda bi, hp, i: (bi, 0, hp))
    return pl.pallas_call(
        _sb_kernel,
        out_shape=jax.ShapeDtypeStruct((b, t, GROUP_WIDTH), BF16),
        grid=(b, SB_HEADS // (2 * SB_PAIRS), t // tile),
        in_specs=[pl.BlockSpec((1, tile, SB_PAIRS * LANES), lambda bi, hp, i: (bi, i, hp)),
                  kv_spec, kv_spec,
                  pl.BlockSpec((2 * tile, tile), lambda bi, hp, i: (0, 0))],
        out_specs=pl.BlockSpec((1, tile, SB_PAIRS * LANES), lambda bi, hp, i: (bi, i, hp)),
        compiler_params=_params("parallel", "parallel", "arbitrary"),
        name="stick_breaking",
    )(q, k, v, jnp.asarray(u2, BF16))


def _mixer_conv_nsa(x, norm_g, w_in, dw_w, dw_b, ln_g, ln_b, pe_k, w1_k, w2_k, pe_v, w1_v, w2_v):
    a, q, kc, vc, ks, vs, kw, vw, gates = _proj_ab(x, norm_g, w_in, dw_w, dw_b, ln_g, ln_b)
    k_cmp = _compress(kc, pe_k, w1_k, w2_k, dup=False)
    v_cmp = _compress(vc, pe_v, w1_v, w2_v, dup=True)
    return a, _nsa(q, k_cmp, v_cmp, ks, vs, kw, vw, gates)


def _mixer_shortconv_sb(x, norm_g, w_in, sc_w):
    c, q, k, v = _proj_cd(x, norm_g, w_in, sc_w)
    return c, _stick_breaking(q, k, v)


def kernel(x, ffn1_norm, ffn1_w_in, ffn1_w_out, mix_norm, ffn2_norm, ffn2_w_in, ffn2_w_out, ab_w_in, conv_dw_w, conv_dw_b, conv_ln_g, conv_ln_b, nsa_pe_k, nsa_w1_k, nsa_w2_k, nsa_pe_v, nsa_w1_v, nsa_w2_v, ab_w_out, cd_w_in, sc_conv_w, cd_w_out, final_norm):
    b, t, d = x.shape
    depth = ffn1_norm.shape[0]
    n = b * t
    ffn1_w = (ffn1_w_in.astype(BF16), ffn1_w_out)
    ffn2_w = (ffn2_w_in.astype(BF16), ffn2_w_out)
    for layer in range(depth):
        x = _ffn(x.reshape(n, d), ffn1_norm[layer], *ffn1_w, layer).reshape(b, t, d)
        if layer % 2 == 0:
            e = layer // 2
            left, right = _mixer_conv_nsa(x, mix_norm[layer], ab_w_in[e], conv_dw_w[e], conv_dw_b[e], conv_ln_g[e],
                                          conv_ln_b[e], nsa_pe_k[e], nsa_w1_k[e], nsa_w2_k[e],
                                          nsa_pe_v[e], nsa_w1_v[e], nsa_w2_v[e])
            w_mix = ab_w_out[e]
        else:
            o = layer // 2
            left, right = _mixer_shortconv_sb(x, mix_norm[layer], cd_w_in[o], sc_conv_w[o])
            w_mix = cd_w_out[o]
        last = layer == depth - 1
        x = _ffn(x.reshape(n, d), ffn2_norm[layer], *ffn2_w, layer,
                 mixer=(left.reshape(n, -1), right.reshape(n, -1), w_mix.astype(BF16)),
                 final_g=final_norm if last else None).reshape(b, t, d)
    return x
```

```python
import functools

import numpy as np
import jax
import jax.numpy as jnp
from jax import lax
from jax.experimental import pallas as pl
from jax.experimental.pallas import tpu as pltpu

F32 = jnp.float32
BF16 = jnp.bfloat16

D_MODEL = 1024
HEAD_DIM = 64
GROUP_WIDTH = D_MODEL // 2
CONV_WIDTH = 31
NSA_HEADS = GROUP_WIDTH // HEAD_DIM
NSA_KV_GROUPS = 2
NSA_REP = NSA_HEADS // NSA_KV_GROUPS
NSA_CMP_BLOCK = 32
NSA_CMP_STRIDE = 16
NSA_CMP_HIDDEN = 128
NSA_SEL_BLOCK = 64
NSA_TOP_N = 16
NSA_FORCED = 3
NSA_WINDOW = 512
NSA_N_BRANCH = 3
SC_WIDTH = 3
SB_HEADS = GROUP_WIDTH // HEAD_DIM
D_FF = 2816
RMS_EPS = 1e-6
LN_EPS = 1e-5
NEG_INF = -1e30
SEL_FORCE = 1e4
QK_SCALE = HEAD_DIM ** -0.5
LOG2_E = float(np.log2(np.e))

LANES = 128
SUBLANES = 8
VMEM_LIMIT = 48 * 1024 * 1024
FFN_VMEM_LIMIT = 56 * 1024 * 1024

FFN_TM = 1024
FFN_TF = 256
PROJ_TM = 512
CONV_HALO = 32
CONV_CHUNK = 128
SC_HALO = 8
NSA_TQ = 256
NSA_TK = 512
SB_T = 256
SB_PAIRS = 2
SB_DEAD_LOG2 = -160.0


def _params(*sem, vmem_limit=VMEM_LIMIT):
    return pltpu.CompilerParams(dimension_semantics=sem, vmem_limit_bytes=vmem_limit)


def _dot(a, b):
    return jnp.dot(a, b, preferred_element_type=F32)


def _dot_nt(a, b):
    return lax.dot_general(a, b, (((1,), (1,)), ((), ())), preferred_element_type=F32)


def _sigmoid(x):
    return 1.0 / (1.0 + jnp.exp(-x))


def _rmsnorm_rows(x, g):
    return x * lax.rsqrt(jnp.mean(x * x, axis=-1, keepdims=True) + RMS_EPS) * g


def _ffn_kernel(x_ref, g_ref, wi_ref, wo_ref, *rest, n_ff, mixer_out, final_norm):
    rest = list(rest)
    x = x_ref[...]
    if mixer_out:
        a_ref, m_ref, wm_ref = rest[:3]
        rest = rest[3:]
        half = a_ref.shape[1]
        x = x + _dot(a_ref[...], wm_ref[:half, :]) + _dot(m_ref[...], wm_ref[half:, :])
    if final_norm:
        fg_ref, o_ref = rest
    else:
        (o_ref,) = rest
    xn = _rmsnorm_rows(x, g_ref[...]).astype(BF16)
    acc = None
    for j in range(n_ff):
        gate = _dot(xn, wi_ref[:, j * FFN_TF:(j + 1) * FFN_TF])
        up = _dot(xn, wi_ref[:, D_FF + j * FFN_TF:D_FF + (j + 1) * FFN_TF])
        h = gate * _sigmoid(gate) * up
        part = _dot(h.astype(BF16), wo_ref[j * FFN_TF:(j + 1) * FFN_TF, :].astype(BF16))
        acc = part if acc is None else acc + part
    y = x + 0.5 * acc
    if final_norm:
        y = _rmsnorm_rows(y, fg_ref[...])
    o_ref[...] = y


def _ffn(x, g, w_in, w_out, layer, mixer=None, final_g=None):
    n, d = x.shape
    n_ff = D_FF // FFN_TF
    tm = min(FFN_TM, n)
    once = pl.Buffered(1)
    in_specs = [
        pl.BlockSpec((tm, d), lambda i: (i, 0)),
        pl.BlockSpec((1, d), lambda i: (0, 0)),
        pl.BlockSpec((None, d, 2 * D_FF), lambda i: (layer, 0, 0), pipeline_mode=once),
        pl.BlockSpec((None, D_FF, d), lambda i: (layer, 0, 0), pipeline_mode=once),
    ]
    args = [x, g.reshape(1, d), w_in, w_out]
    if mixer is not None:
        a, o, w_mix = mixer
        in_specs += [pl.BlockSpec((tm, a.shape[1]), lambda i: (i, 0)), pl.BlockSpec((tm, o.shape[1]), lambda i: (i, 0)),
                     pl.BlockSpec((d, d), lambda i: (0, 0), pipeline_mode=once)]
        args += [a, o, w_mix]
    if final_g is not None:
        in_specs.append(pl.BlockSpec((1, d), lambda i: (0, 0)))
        args.append(final_g.reshape(1, d))
    return pl.pallas_call(
        functools.partial(_ffn_kernel, n_ff=n_ff, mixer_out=mixer is not None, final_norm=final_g is not None),
        out_shape=jax.ShapeDtypeStruct((n, d), F32),
        grid=(n // tm,),
        in_specs=in_specs,
        out_specs=pl.BlockSpec((tm, d), lambda i: (i, 0)),
        compiler_params=_params("parallel", vmem_limit=FFN_VMEM_LIMIT),
        name="ffn",
    )(*args)


AB_A = 0
AB_Q = 1024
AB_KC = 2048
AB_VC = 2176
AB_KS = 2304
AB_VS = 2560
AB_KW = 2816
AB_VW = 3072
AB_G = 3328
AB_COLS = 3584

POS_SPLIT = 3
POS_TOK = HEAD_DIM
POS_CMP = POS_TOK + 2 * POS_SPLIT
MASK_BIG = 2.0 ** 100


def _pos_lanes(hi, lo, base):
    lane = lax.broadcasted_iota(jnp.int32, (1, LANES), 1)
    in_hi = (lane >= base) & (lane < base + POS_SPLIT)
    in_lo = (lane >= base + POS_SPLIT) & (lane < base + 2 * POS_SPLIT)
    return jnp.where(in_hi, hi, jnp.where(in_lo, lo, 0.0))


def _proj_ab_kernel(x_ref, g_ref, w_ref, qb_ref, dw_ref, db_ref, lg_ref, lb_ref,
                    a_ref, q_ref, kc_ref, vc_ref, ks_ref, vs_ref, kw_ref, vw_ref, gt_ref, ext_ref, shift_ref, *, tm):
    xn = _rmsnorm_rows(x_ref[0], g_ref[...]).astype(BF16)

    def seg(lo, hi):
        return _dot(xn, w_ref[:, lo:hi])

    @pl.when(pl.program_id(1) == 0)
    def _():
        ext_ref[0:CONV_HALO, :] = jnp.zeros((CONV_HALO, GROUP_WIDTH), F32)

    av = seg(AB_A, AB_A + 2 * GROUP_WIDTH)
    ext_ref[CONV_HALO:, :] = av[:, :GROUP_WIDTH] * _sigmoid(av[:, GROUP_WIDTH:])
    _conformer_conv_rows(ext_ref, shift_ref, dw_ref, db_ref, lg_ref, lb_ref, a_ref, tm)
    ext_ref[0:CONV_HALO, :] = ext_ref[tm:tm + CONV_HALO, :]
    q_ref[0] = (seg(AB_Q, AB_KC) + qb_ref[...]).astype(BF16)
    kc_ref[0] = seg(AB_KC, AB_VC)
    vc_ref[0] = seg(AB_VC, AB_KS)
    one_lane = jnp.where(lax.broadcasted_iota(jnp.int32, (1, NSA_KV_GROUPS * LANES), 1) % LANES == HEAD_DIM, 1.0, 0.0)
    vs_ref[0] = (seg(AB_VS, AB_KW) + one_lane).astype(BF16)
    vw_ref[0] = (seg(AB_VW, AB_G) + one_lane).astype(BF16)
    gt_ref[0] = _sigmoid(seg(AB_G, AB_COLS))

    tok = pl.program_id(1) * tm + lax.broadcasted_iota(jnp.int32, (tm, 1), 0)
    blk = lax.shift_right_logical(tok, int(np.log2(NSA_SEL_BLOCK)))
    pos = _pos_lanes(blk.astype(F32), (tok & (NSA_SEL_BLOCK - 1)).astype(F32), POS_TOK)
    onehot = jnp.where(lax.broadcasted_iota(jnp.int32, (1, LANES), 1) == blk, 1.0, 0.0).astype(BF16)
    ks = seg(AB_KS, AB_VS)
    kw = seg(AB_KW, AB_VW)
    for g in range(NSA_KV_GROUPS):
        ks_ref[0, :, 2 * g * LANES:(2 * g + 1) * LANES] = (ks[:, g * LANES:(g + 1) * LANES] + pos).astype(BF16)
        ks_ref[0, :, (2 * g + 1) * LANES:(2 * g + 2) * LANES] = onehot
        kw_ref[0, :, g * LANES:(g + 1) * LANES] = (kw[:, g * LANES:(g + 1) * LANES] + pos).astype(BF16)


def _arrange_ab_weight(w):
    d = w.shape[0]
    kvw = NSA_KV_GROUPS * HEAD_DIM
    o = 2 * GROUP_WIDTH
    a = w[:, :o]
    q = w[:, o:o + GROUP_WIDTH].reshape(d, NSA_HEADS, HEAD_DIM) * (QK_SCALE * LOG2_E)
    o += GROUP_WIDTH
    kc, vc, ks, vs, kw, vw = [w[:, o + i * kvw:o + (i + 1) * kvw] for i in range(6)]
    o += 6 * kvw
    g = w[:, o:].reshape(d, NSA_KV_GROUPS, NSA_REP * NSA_N_BRANCH)

    zeros_h = jnp.zeros((d, NSA_HEADS, HEAD_DIM), w.dtype)
    q_pad = jnp.concatenate([q, zeros_h], axis=-1).reshape(d, NSA_HEADS * LANES)

    def k_pad(k):
        k = k.reshape(d, NSA_KV_GROUPS, HEAD_DIM)
        return jnp.concatenate([k, jnp.zeros_like(k)], axis=-1).reshape(d, NSA_KV_GROUPS * LANES)

    g_pad = jnp.pad(g, ((0, 0), (0, 0), (0, LANES - g.shape[-1]))).reshape(d, NSA_KV_GROUPS * LANES)
    out = jnp.concatenate([a, q_pad, kc, vc, k_pad(ks), k_pad(vs), k_pad(kw), k_pad(vw), g_pad], axis=-1)
    assert out.shape[1] == AB_COLS
    return out.astype(BF16)


def _bf16_terms(x, n):
    terms, rest = [], np.asarray(x, np.float64)
    for _ in range(n):
        term = rest.astype(BF16).astype(np.float64)
        terms.append(term)
        rest = rest - term
    return terms


def _alibi_query_lanes():
    row = np.zeros((NSA_HEADS, LANES), np.float64)
    for h in range(NSA_HEADS):
        slope = 2.0 ** (-8.0 * (h + 1) / NSA_HEADS)
        for i, term in enumerate(_bf16_terms(slope * np.log2(np.e), POS_SPLIT)):
            row[h, POS_TOK + i] = NSA_SEL_BLOCK * term
            row[h, POS_TOK + POS_SPLIT + i] = term
            row[h, POS_CMP + i] = 2 * NSA_CMP_STRIDE * term
            row[h, POS_CMP + POS_SPLIT + i] = NSA_CMP_STRIDE * term
    return jnp.asarray(row.reshape(1, NSA_HEADS * LANES), F32)


def _proj_ab(x, g, w, dw_w, dw_b, ln_g, ln_b):
    b, t, d = x.shape
    tm = min(PROJ_TM, t)
    wa = _arrange_ab_weight(w)

    def tok(width):
        return pl.BlockSpec((1, tm, width), lambda bi, i: (bi, i, 0))

    const = lambda width: pl.BlockSpec((1, width), lambda bi, i: (0, 0))
    once = pl.Buffered(1)
    gw = GROUP_WIDTH
    kv = NSA_KV_GROUPS * LANES
    widths = [gw, NSA_HEADS * LANES, LANES, LANES, 2 * kv, kv, kv, kv, kv]
    dtypes = [BF16, BF16, F32, F32, BF16, BF16, BF16, BF16, F32]
    return pl.pallas_call(
        functools.partial(_proj_ab_kernel, tm=tm),
        out_shape=[jax.ShapeDtypeStruct((b, t, wd), dt) for wd, dt in zip(widths, dtypes)],
        grid=(b, t // tm),
        in_specs=[tok(d), const(d), pl.BlockSpec((d, AB_COLS), lambda bi, i: (0, 0), pipeline_mode=once),
                  const(NSA_HEADS * LANES), pl.BlockSpec((CONV_WIDTH, gw), lambda bi, i: (0, 0)),
                  const(gw), const(gw), const(gw)],
        out_specs=[tok(wd) for wd in widths],
        scratch_shapes=[pltpu.VMEM((tm + CONV_HALO, gw), F32), pltpu.VMEM((SUBLANES, tm + CONV_HALO, gw), F32)],
        compiler_params=_params("arbitrary", "arbitrary"),
        name="proj_ab",
    )(x, g.reshape(1, d), wa, _alibi_query_lanes(), dw_w, dw_b.reshape(1, gw), ln_g.reshape(1, gw),
      ln_b.reshape(1, gw))


CD_B = 0
CD_C = 512
CD_U = 1024
CD_Q = 1536
CD_K = 2048
CD_V = 2560
CD_COLS = 3072


def _proj_cd_kernel(x_ref, g_ref, w_ref, scw_ref, c_ref, q_ref, k_ref, v_ref, ext_ref, *, tm):
    xn = _rmsnorm_rows(x_ref[0], g_ref[...]).astype(BF16)

    def seg(lo, hi):
        return _dot(xn, w_ref[:, lo:hi])

    q_ref[0] = seg(CD_Q, CD_K).astype(BF16)
    k_ref[0] = seg(CD_K, CD_V).astype(BF16)
    v_ref[0] = seg(CD_V, CD_COLS).astype(BF16)

    @pl.when(pl.program_id(1) == 0)
    def _():
        ext_ref[0:SC_HALO, :] = jnp.zeros((SC_HALO, GROUP_WIDTH), F32)

    ext_ref[SC_HALO:, :] = seg(CD_C, CD_U) * seg(CD_U, CD_Q)
    off = SC_HALO - (SC_WIDTH - 1)
    acc = jnp.zeros((tm, GROUP_WIDTH), F32)
    for k in range(SC_WIDTH):
        acc = acc + scw_ref[k:k + 1, :] * ext_ref[off + k:off + k + tm, :]
    c_ref[0] = (seg(CD_B, CD_C) * acc).astype(c_ref.dtype)
    ext_ref[0:SC_HALO, :] = ext_ref[tm:tm + SC_HALO, :]


def _arrange_cd_weight(w):
    gw = GROUP_WIDTH
    out = jnp.concatenate([w[:, :3 * gw], w[:, 3 * gw:4 * gw] * (QK_SCALE * LOG2_E), w[:, 4 * gw:]], axis=-1)
    assert out.shape[1] == CD_COLS
    return out.astype(BF16)


def _proj_cd(x, g, w, sc_w):
    b, t, d = x.shape
    tm = min(PROJ_TM, t)
    wa = _arrange_cd_weight(w)

    def tok(width):
        return pl.BlockSpec((1, tm, width), lambda bi, i: (bi, i, 0))

    widths = [GROUP_WIDTH] * 4
    return pl.pallas_call(
        functools.partial(_proj_cd_kernel, tm=tm),
        out_shape=[jax.ShapeDtypeStruct((b, t, wd), BF16) for wd in widths],
        grid=(b, t // tm),
        in_specs=[tok(d), pl.BlockSpec((1, d), lambda bi, i: (0, 0)),
                  pl.BlockSpec((d, CD_COLS), lambda bi, i: (0, 0)),
                  pl.BlockSpec((SC_WIDTH, GROUP_WIDTH), lambda bi, i: (0, 0))],
        out_specs=[tok(wd) for wd in widths],
        scratch_shapes=[pltpu.VMEM((tm + SC_HALO, GROUP_WIDTH), F32)],
        compiler_params=_params("arbitrary", "arbitrary"),
        name="proj_cd",
    )(x, g.reshape(1, d), wa, sc_w)


def _conformer_conv_rows(ext_ref, shift_ref, w_ref, b_ref, lg_ref, lb_ref, o_ref, tt):
    off = CONV_HALO - (CONV_WIDTH - 1)
    last = off + CONV_WIDTH - 1
    steps = {phase: [m for m in range((last - phase) // SUBLANES + 1) if off <= SUBLANES * m + phase]
             for phase in range(SUBLANES)}
    for phase in range(SUBLANES):
        rows = SUBLANES * steps[phase][-1] + tt
        shift_ref[phase, 0:rows, :] = ext_ref[phase:phase + rows, :]
    for c in range(tt // CONV_CHUNK):
        r0 = c * CONV_CHUNK
        acc = jnp.zeros((CONV_CHUNK, GROUP_WIDTH), F32)
        for phase in range(SUBLANES):
            for m in steps[phase]:
                k = SUBLANES * m + phase - off
                lo = r0 + SUBLANES * m
                acc = acc + w_ref[k:k + 1, :] * shift_ref[phase, lo:lo + CONV_CHUNK, :]
        y = acc + b_ref[...]
        mu = jnp.mean(y, axis=-1, keepdims=True)
        yc = y - mu
        var = jnp.mean(yc * yc, axis=-1, keepdims=True)
        yn = yc * lax.rsqrt(var + LN_EPS) * lg_ref[...] + lb_ref[...]
        o_ref[0, r0:r0 + CONV_CHUNK, :] = (yn * _sigmoid(yn)).astype(o_ref.dtype)


def _gelu_tanh(x):
    return 0.5 * x * (1.0 + jnp.tanh(np.sqrt(2.0 / np.pi).astype(np.float32) * (x + 0.044715 * (x * x * x))))


def _compress_kernel(x_ref, pe_ref, w1_ref, w2_ref, o_ref, y_ref, *, nblk, keys):
    stride = NSA_CMP_STRIDE

    def partial_sum(l0):
        acc = None
        for l in range(l0, l0 + stride):
            xl = x_ref[0, pl.ds(l - l0, nblk, stride=stride), :] + pe_ref[l:l + 1, :]
            part = _dot(xl.astype(BF16), w1_ref[l])
            acc = part if acc is None else acc + part
        return acc

    h = partial_sum(0) + pltpu.roll(partial_sum(stride), nblk - 1, 0)
    row = lax.broadcasted_iota(jnp.int32, (nblk, 1), 0)
    ratio = NSA_SEL_BLOCK // NSA_CMP_STRIDE
    for g in range(NSA_KV_GROUPS):
        y = _dot(_gelu_tanh(h[:, g * NSA_CMP_HIDDEN:(g + 1) * NSA_CMP_HIDDEN]).astype(BF16), w2_ref[...])
        y = jnp.where(row < nblk - 1, y, 0.0)
        if keys:
            y = y + _pos_lanes(lax.shift_right_logical(row, 1).astype(F32), (row & 1).astype(F32), POS_CMP)
        y_ref[...] = y
        for r in range(ratio):
            o_ref[0, g, r] = y_ref[pl.ds(r, nblk // ratio, stride=ratio), :].astype(o_ref.dtype)


def _compress(kx, pe, w1, w2, dup):
    b, t, width = kx.shape
    g = NSA_KV_GROUPS
    nblk = t // NSA_CMP_STRIDE
    ratio = NSA_SEL_BLOCK // NSA_CMP_STRIDE
    hid = NSA_CMP_HIDDEN
    zeros = jnp.zeros_like(w1)
    w1_bd = jnp.concatenate([jnp.concatenate([w1, zeros], axis=-1), jnp.concatenate([zeros, w1], axis=-1)],
                            axis=1).astype(BF16)
    w2p = jnp.concatenate([w2, w2 if dup else jnp.zeros_like(w2)], axis=-1).astype(BF16)
    return pl.pallas_call(
        functools.partial(_compress_kernel, nblk=nblk, keys=not dup),
        out_shape=jax.ShapeDtypeStruct((b, g, ratio, nblk // ratio, LANES), BF16),
        grid=(b,),
        in_specs=[pl.BlockSpec((1, t, width), lambda bi: (bi, 0, 0)),
                  pl.BlockSpec((NSA_CMP_BLOCK, width), lambda bi: (0, 0)),
                  pl.BlockSpec((NSA_CMP_BLOCK, width, g * hid), lambda bi: (0, 0, 0)),
                  pl.BlockSpec((hid, LANES), lambda bi: (0, 0))],
        out_specs=pl.BlockSpec((1, g, ratio, nblk // ratio, LANES), lambda bi: (bi, 0, 0, 0, 0)),
        scratch_shapes=[pltpu.VMEM((nblk, LANES), F32)],
        compiler_params=_params("parallel"),
        name="nsa_compress",
    )(kx, jnp.tile(pe, (1, g)), w1_bd, w2p)


def _nsa_kernel(q_ref, kc_ref, vc_ref, ks_ref, vs_ref, kw_ref, vw_ref, gt_ref, o_ref, *, t_len, ns):
    tq, tk, rep = NSA_TQ, NSA_TK, NSA_REP
    rows = rep * tq
    i = pl.program_id(2)
    q0 = i * tq

    q2 = q_ref[0]
    qg = jnp.concatenate([q2[:, r * LANES:(r + 1) * LANES] for r in range(rep)], axis=0)
    tq_i = q0 + lax.broadcasted_iota(jnp.int32, (tq, 1), 0)

    def per_head(x):
        return jnp.concatenate([x] * rep, axis=0)

    lane_i = lax.broadcasted_iota(jnp.int32, (1, ns), 1)
    ratio = NSA_SEL_BLOCK // NSA_CMP_STRIDE
    n_cmp = t_len // NSA_CMP_STRIDE - 1
    s_list = []
    for r in range(ratio):
        c_i = lane_i * ratio + r
        mask = ((c_i * NSA_CMP_STRIDE + (NSA_CMP_BLOCK - 1)) <= tq_i) & (c_i < n_cmp)
        s_list.append(_dot_nt(qg, kc_ref[0, 0, r]) + per_head(jnp.where(mask, 0.0, NEG_INF)))
    m = jnp.max(functools.reduce(jnp.maximum, s_list), axis=-1, keepdims=True)
    p_list = [jnp.exp2(s - m) for s in s_list]
    l = jnp.sum(functools.reduce(lambda a, b: a + b, p_list), axis=-1, keepdims=True)
    any_visible = per_head(tq_i >= NSA_CMP_BLOCK - 1)
    inv = jnp.where(any_visible, 1.0 / l, 0.0)
    p_list = [p * inv for p in p_list]
    o_c = functools.reduce(lambda a, b: a + b,
                           [_dot(p.astype(BF16), vc_ref[0, 0, r]) for r, p in enumerate(p_list)])

    def head_sum(p):
        return functools.reduce(lambda a, b: a + b, [p[r * tq:(r + 1) * tq] for r in range(rep)])

    ps = [head_sum(p) for p in p_list]
    lane_q = lax.broadcasted_iota(jnp.int32, (tq, ns), 1)
    prev_last = jnp.where(lane_q == 0, 0.0, pltpu.roll(ps[ratio - 1], 1, 1))
    imp = prev_last + ps[0] + ps[1] + ps[2] + ps[3]
    cur =lax.shift_right_logical(tq_i, int(np.log2(NSA_SEL_BLOCK)))
    visible = lane_q * NSA_SEL_BLOCK <= tq_i
    forced = (lane_q == 0) | (lane_q == cur) | (lane_q == cur - 1)
    score = jnp.where(visible, jnp.where(forced, SEL_FORCE, imp), -1.0)

    blk_f = lax.broadcasted_iota(jnp.int32, (ns, tq), 0).astype(F32)
    score_t = score.T
    work = jnp.where(score_t == SEL_FORCE, -2.0, score_t)
    for _ in range(min(NSA_TOP_N, ns) - NSA_FORCED):
        top = jnp.max(work, axis=0, keepdims=True)
        idx = jnp.min(jnp.where(work == top, blk_f, float(ns)), axis=0, keepdims=True)
        work = jnp.where(blk_f == idx, -2.0, work)
    sel = jnp.where((score >= 0.0) & (work.T == -2.0), 1.0, 0.0)

    unsel = ((sel - 1.0) * MASK_BIG).astype(BF16)
    if ns < LANES:
        unsel = jnp.concatenate([unsel, jnp.zeros((tq, LANES - ns), BF16)], axis=1)
    q_aug = jnp.concatenate([qg, jnp.concatenate([unsel] * rep, axis=0)], axis=1)

    def sel_tile(k0, causal):
        s = _dot_nt(q_aug, ks_ref[0, pl.ds(k0, tk), :])
        if causal:
            tok = k0 + lax.broadcasted_iota(jnp.int32, (1, tk), 1)
            s = s + per_head(jnp.where(tok <= tq_i, 0.0, -MASK_BIG))
        return s

    blocks_per_tile = tk // NSA_SEL_BLOCK
    tile_of_blk = lax.shift_right_logical(lax.broadcasted_iota(jnp.int32, (ns, LANES), 0),
                                          int(np.log2(blocks_per_tile)))
    lane_t = lax.broadcasted_iota(jnp.int32, (ns, LANES), 1)
    per_tile = _dot(sel.astype(BF16), jnp.where(tile_of_blk == lane_t, 1.0, 0.0).astype(BF16))
    tile_any = jnp.max(per_tile, axis=0, keepdims=True) > 0.0
    pow2 = lax.shift_left(jnp.ones((1, LANES), jnp.int32), lane_t[:1] & 15).astype(F32)
    tile_bits = jnp.sum(jnp.where(tile_any, pow2, 0.0), axis=-1, keepdims=True).astype(jnp.int32)[0, 0]

    n_past = lax.shift_right_logical(q0, int(np.log2(tk)))
    kd = pl.multiple_of(n_past * tk, tk)
    s = sel_tile(kd, True)
    m_s = jnp.max(s, axis=-1, keepdims=True)
    acc_s = _dot(jnp.exp2(s - m_s).astype(BF16), vs_ref[0, pl.ds(kd, tk), :])

    def sel_update(kt, carry, offset=None):
        m_run, acc = carry
        k0 = pl.multiple_of(kt * tk, tk)
        s = sel_tile(k0, False)
        if offset is not None:
            s = s + offset
        m_new = jnp.maximum(m_run, jnp.max(s, axis=-1, keepdims=True))
        acc = jnp.exp2(m_run - m_new) * acc + _dot(jnp.exp2(s - m_new).astype(BF16), vs_ref[0, pl.ds(k0, tk), :])
        return m_new, acc

    def next_active(state):
        return lax.while_loop(lambda st: (st[0] & 1) == 0,
                              lambda st: (lax.shift_right_logical(st[0], 1), st[1] + 1), state)

    def sel_body(state):
        pending, kt = next_active(state[:2])
        return (lax.shift_right_logical(pending, 1), kt + 1) + sel_update(kt, state[2:])

    newest = jnp.maximum(n_past - 1, 0)
    m_s, acc_s = sel_update(newest, (m_s, acc_s), offset=jnp.where(n_past > 0, 0.0, -MASK_BIG))
    past_bits = tile_bits & (lax.shift_left(jnp.int32(1), newest) - 1)
    _, _, _, acc_s = lax.while_loop(lambda st: st[0] != 0, sel_body, (past_bits, jnp.int32(0), m_s, acc_s))
    o_s = acc_s / acc_s[:, HEAD_DIM:HEAD_DIM + 1]

    wlen = NSA_WINDOW + tq
    w0 = pl.multiple_of(jnp.maximum(q0 - NSA_WINDOW, 0), tq)
    dist = tq_i - (w0 + lax.broadcasted_iota(jnp.int32, (1, wlen), 1))
    mask_w = (dist >= 0) & (dist < NSA_WINDOW)
    s_w = _dot_nt(qg, kw_ref[0, pl.ds(w0, wlen), :]) + per_head(jnp.where(mask_w, 0.0, NEG_INF))
    p_w = jnp.exp2(s_w - jnp.max(s_w, axis=-1, keepdims=True))
    o_w = _dot(p_w.astype(BF16), vw_ref[0, pl.ds(w0, wlen), :])
    o_w = o_w / o_w[:, HEAD_DIM:HEAD_DIM + 1]

    gt = gt_ref[0]

    def gate(branch):
        return jnp.concatenate(
            [gt[:, r * NSA_N_BRANCH + branch:r * NSA_N_BRANCH + branch + 1] for r in range(rep)], axis=0)

    o = gate(0) * o_c + gate(1) * o_s + gate(2) * o_w
    low_half = lax.broadcasted_iota(jnp.int32, (tq, LANES), 1) < HEAD_DIM
    for c in range(rep // 2):
        even = o[(2 * c) * tq:(2 * c + 1) * tq]
        odd = pltpu.roll(o[(2 * c + 1) * tq:(2 * c + 2) * tq], HEAD_DIM, 1)
        o_ref[0, :, c * LANES:(c + 1) * LANES] = jnp.where(low_half, even, odd).astype(o_ref.dtype)


def _nsa(q, kc, vc, ks, vs, kw, vw, gates):
    b, t, _ = q.shape
    g = NSA_KV_GROUPS
    ns = t // NSA_SEL_BLOCK
    assert ns <= LANES and t >= NSA_WINDOW + NSA_TQ and t % NSA_TK == 0 and NSA_TK % NSA_TQ == 0
    assert t // NSA_TK <= 16
    ratio = NSA_SEL_BLOCK // NSA_CMP_STRIDE
    qw = NSA_REP * LANES
    cmp_spec = pl.BlockSpec((1, 1, ratio, ns, LANES), lambda bi, gi, i: (bi, gi, 0, 0, 0))
    kv_spec = pl.BlockSpec((1, t, LANES), lambda bi, gi, i: (bi, 0, gi))
    ks_spec = pl.BlockSpec((1, t, 2 * LANES), lambda bi, gi, i: (bi, 0, gi))
    return pl.pallas_call(
        functools.partial(_nsa_kernel, t_len=t, ns=ns),
        out_shape=jax.ShapeDtypeStruct((b, t, GROUP_WIDTH), BF16),
        grid=(b, g, t // NSA_TQ),
        in_specs=[pl.BlockSpec((1, NSA_TQ, qw), lambda bi, gi, i: (bi, i, gi)),
                  cmp_spec, cmp_spec, ks_spec, kv_spec, kv_spec, kv_spec,
                  pl.BlockSpec((1, NSA_TQ, LANES), lambda bi, gi, i: (bi, i, gi))],
        out_specs=pl.BlockSpec((1, NSA_TQ, NSA_REP * HEAD_DIM), lambda bi, gi, i: (bi, i, gi)),
        compiler_params=_params("parallel", "parallel", "arbitrary"),
        name="nsa_attention",
    )(q, kc, vc, ks, vs, kw, vw, gates)


def _sb_tiles(qs, k, v, u2, carries, mask):
    stage1 = []
    for h, qh in enumerate(qs):
        p = h // 2
        z = _dot_nt(qh, k[:, p * LANES:(p + 1) * LANES])
        nk = jnp.maximum(z, 0.0) + jnp.log2(1.0 + jnp.exp2(-jnp.abs(z)))
        if mask is not None:
            nk = jnp.where(mask, nk, 0.0)
        hi = nk.astype(BF16)
        lo = (nk - hi.astype(F32)).astype(BF16)
        stage1.append((z, jnp.concatenate([hi, lo], axis=1)))
    laters = [_dot(hilo, u2) for (_, hilo) in stage1]
    out = []
    for h, ((z, _), later, (acc, c)) in enumerate(zip(stage1, laters, carries)):
        p = h // 2
        a = jnp.exp2(z + later)
        if mask is not None:
            a = jnp.where(mask, a, 0.0)
        acc = acc + jnp.exp2(c) * _dot(a.astype(BF16), v[:, p * LANES:(p + 1) * LANES])
        out.append((acc, c + later[:, 0:1]))
    return tuple(out)


def _sb_kernel(q_ref, k_ref, v_ref, u_ref, o_ref):
    tile = SB_T
    i = pl.program_id(2)
    q2 = q_ref[0]
    u = u_ref[...]
    low_lanes = lax.broadcasted_iota(jnp.int32, (1, LANES), 1) < HEAD_DIM
    qs = [jnp.where(low_lanes if h % 2 == 0 else ~low_lanes, q2[:, (h // 2) * LANES:(h // 2 + 1) * LANES],
                    jnp.zeros((), BF16))
          for h in range(2 * SB_PAIRS)]
    q0 = pl.multiple_of(i * tile, tile)
    zero = (jnp.zeros((tile, LANES), F32), jnp.zeros((tile, 1), F32))
    mask = lax.broadcasted_iota(jnp.int32, (tile, tile), 1) < lax.broadcasted_iota(jnp.int32, (tile, tile), 0)
    carry = _sb_tiles(qs, k_ref[0, pl.ds(q0, tile), :], v_ref[0, pl.ds(q0, tile), :], u,
                      tuple(zero for _ in qs), mask)

    def body(jj, carry):
        k0 = pl.multiple_of(jnp.maximum(i - 1 - jj, 0) * tile, tile)
        return _sb_tiles(qs, k_ref[0, pl.ds(k0, tile), :], v_ref[0, pl.ds(k0, tile), :], u, carry, None)

    def alive(carry):
        c_max = functools.reduce(jnp.maximum, [c for _, c in carry])
        return jnp.max(c_max) > SB_DEAD_LOG2

    def step(state):
        jj, _, carry = state
        carry = body(jj, carry)
        return jj + 1, alive(carry), carry

    has_past = i > 0
    off_c = jnp.where(has_past, 0.0, -1e4)
    first = body(0, tuple((acc, c + off_c) for acc, c in carry))
    carry = tuple((acc1, jnp.where(has_past, c1, c0)) for (acc1, c1), (_, c0) in zip(first, carry))
    _, _, carry = lax.while_loop(lambda st: (st[0] < i) & st[1], step, (jnp.int32(1), alive(carry), carry))
    low_half = lax.broadcasted_iota(jnp.int32, (tile, LANES), 1) < HEAD_DIM
    for p in range(SB_PAIRS):
        o_ref[0, :, p * LANES:(p + 1) * LANES] = jnp.where(
            low_half, carry[2 * p][0], carry[2 * p + 1][0]).astype(o_ref.dtype)


def _stick_breaking(q, k, v):
    b, t, _ = q.shape
    tile = SB_T
    assert t % tile == 0
    u = -(np.arange(tile)[:, None] >= np.arange(tile)[None, :]).astype(np.float32)
    u2 = np.concatenate([u, u], axis=0)
    kv_spec = pl.BlockSpec((1, t, SB_PAIRS * LANES), lambda bi, hp, i: (bi, 0, hp))
    return pl.pallas_call(
        _sb_kernel,
        out_shape=jax.ShapeDtypeStruct((b, t, GROUP_WIDTH), BF16),
        grid=(b, SB_HEADS // (2 * SB_PAIRS), t // tile),
        in_specs=[pl.BlockSpec((1, tile, SB_PAIRS * LANES), lambda bi, hp, i: (bi, i, hp)),
                  kv_spec, kv_spec,
                  pl.BlockSpec((2 * tile, tile), lambda bi, hp, i: (0, 0))],
        out_specs=pl.BlockSpec((1, tile, SB_PAIRS * LANES), lambda bi, hp, i: (bi, i, hp)),
        compiler_params=_params("parallel", "parallel", "arbitrary"),
        name="stick_breaking",
    )(q, k, v, jnp.asarray(u2, BF16))


def _mixer_conv_nsa(x, norm_g, w_in, dw_w, dw_b, ln_g, ln_b, pe_k, w1_k, w2_k, pe_v, w1_v, w2_v):
    a, q, kc, vc, ks, vs, kw, vw, gates = _proj_ab(x, norm_g, w_in, dw_w, dw_b, ln_g, ln_b)
    k_cmp = _compress(kc, pe_k, w1_k, w2_k, dup=False)
    v_cmp = _compress(vc, pe_v, w1_v, w2_v, dup=True)
    return a, _nsa(q, k_cmp, v_cmp, ks, vs, kw, vw, gates)


def _mixer_shortconv_sb(x, norm_g, w_in, sc_w):
    c, q, k, v = _proj_cd(x, norm_g, w_in, sc_w)
    return c, _stick_breaking(q, k, v)


def kernel(x, ffn1_norm, ffn1_w_in, ffn1_w_out, mix_norm, ffn2_norm, ffn2_w_in, ffn2_w_out, ab_w_in, conv_dw_w, conv_dw_b, conv_ln_g, conv_ln_b, nsa_pe_k, nsa_w1_k, nsa_w2_k, nsa_pe_v, nsa_w1_v, nsa_w2_v, ab_w_out, cd_w_in, sc_conv_w, cd_w_out, final_norm):
    b, t, d = x.shape
    depth = ffn1_norm.shape[0]
    n = b * t
    ffn1_w = (ffn1_w_in.astype(BF16), ffn1_w_out)
    ffn2_w = (ffn2_w_in.astype(BF16), ffn2_w_out)
    for layer in range(depth):
        x = _ffn(x.reshape(n, d), ffn1_norm[layer], *ffn1_w, layer).reshape(b, t, d)
        if layer % 2 == 0:
            e = layer // 2
            left, right = _mixer_conv_nsa(x, mix_norm[layer], ab_w_in[e], conv_dw_w[e], conv_dw_b[e], conv_ln_g[e],
                                          conv_ln_b[e], nsa_pe_k[e], nsa_w1_k[e], nsa_w2_k[e],
                                          nsa_pe_v[e], nsa_w1_v[e], nsa_w2_v[e])
            w_mix = ab_w_out[e]
        else:
            o = layer // 2
            left, right = _mixer_shortconv_sb(x, mix_norm[layer], cd_w_in[o], sc_conv_w[o])
            w_mix = cd_w_out[o]
        last = layer == depth - 1
        x = _ffn(x.reshape(n, d), ffn2_norm[layer], *ffn2_w, layer,
                 mixer=(left.reshape(n, -1), right.reshape(n, -1), w_mix.astype(BF16)),
                 final_g=final_norm if last else None).reshape(b, t, d)
    return x
```

```python
import functools

import numpy as np
import jax
import jax.numpy as jnp
from jax import lax
from jax.experimental import pallas as pl
from jax.experimental.pallas import tpu as pltpu

F32 = jnp.float32
BF16 = jnp.bfloat16

D_MODEL = 1024
HEAD_DIM = 64
GROUP_WIDTH = D_MODEL // 2
CONV_WIDTH = 31
NSA_HEADS = GROUP_WIDTH // HEAD_DIM
NSA_KV_GROUPS = 2
NSA_REP = NSA_HEADS // NSA_KV_GROUPS
NSA_CMP_BLOCK = 32
NSA_CMP_STRIDE = 16
NSA_CMP_HIDDEN = 128
NSA_SEL_BLOCK = 64
NSA_TOP_N = 16
NSA_FORCED = 3
NSA_WINDOW = 512
NSA_N_BRANCH = 3
SC_WIDTH = 3
SB_HEADS = GROUP_WIDTH // HEAD_DIM
D_FF = 2816
RMS_EPS = 1e-6
LN_EPS = 1e-5
NEG_INF = -1e30
SEL_FORCE = 1e4
QK_SCALE = HEAD_DIM ** -0.5
LOG2_E = float(np.log2(np.e))

LANES = 128
SUBLANES = 8
VMEM_LIMIT = 48 * 1024 * 1024
FFN_VMEM_LIMIT = 56 * 1024 * 1024

FFN_TM = 1024
FFN_TF = 256
PROJ_TM = 512
CONV_HALO = 32
CONV_CHUNK = 128
SC_HALO = 8
NSA_TQ = 256
NSA_TK = 512
SB_T = 256
SB_PAIRS = 2
SB_DEAD_LOG2 = -160.0


def _params(*sem, vmem_limit=VMEM_LIMIT):
    return pltpu.CompilerParams(dimension_semantics=sem, vmem_limit_bytes=vmem_limit)


def _dot(a, b):
    return jnp.dot(a, b, preferred_element_type=F32)


def _dot_nt(a, b):
    return lax.dot_general(a, b, (((1,), (1,)), ((), ())), preferred_element_type=F32)


def _sigmoid(x):
    return 1.0 / (1.0 + jnp.exp(-x))


def _rmsnorm_rows(x, g):
    return x * lax.rsqrt(jnp.mean(x * x, axis=-1, keepdims=True) + RMS_EPS) * g


def _ffn_kernel(x_ref, g_ref, wi_ref, wo_ref, *rest, n_ff, mixer_out, final_norm):
    rest = list(rest)
    x = x_ref[...]
    if mixer_out:
        a_ref, m_ref, wm_ref = rest[:3]
        rest = rest[3:]
        half = a_ref.shape[1]
        x = x + _dot(a_ref[...], wm_ref[:half, :]) + _dot(m_ref[...], wm_ref[half:, :])
    if final_norm:
        fg_ref, o_ref = rest
    else:
        (o_ref,) = rest
    xn = _rmsnorm_rows(x, g_ref[...]).astype(BF16)
    acc = None
    for j in range(n_ff):
        gate = _dot(xn, wi_ref[:, j * FFN_TF:(j + 1) * FFN_TF])
        up = _dot(xn, wi_ref[:, D_FF + j * FFN_TF:D_FF + (j + 1) * FFN_TF])
        h = gate * _sigmoid(gate) * up
        part = _dot(h.astype(BF16), wo_ref[j * FFN_TF:(j + 1) * FFN_TF, :].astype(BF16))
        acc = part if acc is None else acc + part
    y = x + 0.5 * acc
    if final_norm:
        y = _rmsnorm_rows(y, fg_ref[...])
    o_ref[...] = y


def _ffn(x, g, w_in, w_out, layer, mixer=None, final_g=None):
    n, d = x.shape
    n_ff = D_FF // FFN_TF
    tm = min(FFN_TM, n)
    once = pl.Buffered(1)
    in_specs = [
        pl.BlockSpec((tm, d), lambda i: (i, 0)),
        pl.BlockSpec((1, d), lambda i: (0, 0)),
        pl.BlockSpec((None, d, 2 * D_FF), lambda i: (layer, 0, 0), pipeline_mode=once),
        pl.BlockSpec((None, D_FF, d), lambda i: (layer, 0, 0), pipeline_mode=once),
    ]
    args = [x, g.reshape(1, d), w_in, w_out]
    if mixer is not None:
        a, o, w_mix = mixer
        in_specs += [pl.BlockSpec((tm, a.shape[1]), lambda i: (i, 0)), pl.BlockSpec((tm, o.shape[1]), lambda i: (i, 0)),
                     pl.BlockSpec((d, d), lambda i: (0, 0), pipeline_mode=once)]
        args += [a, o, w_mix]
    if final_g is not None:
        in_specs.append(pl.BlockSpec((1, d), lambda i: (0, 0)))
        args.append(final_g.reshape(1, d))
    return pl.pallas_call(
        functools.partial(_ffn_kernel, n_ff=n_ff, mixer_out=mixer is not None, final_norm=final_g is not None),
        out_shape=jax.ShapeDtypeStruct((n, d), F32),
        grid=(n // tm,),
        in_specs=in_specs,
        out_specs=pl.BlockSpec((tm, d), lambda i: (i, 0)),
        compiler_params=_params("parallel", vmem_limit=FFN_VMEM_LIMIT),
        name="ffn",
    )(*args)


AB_A = 0
AB_Q = 1024
AB_KC = 2048
AB_VC = 2176
AB_KS = 2304
AB_VS = 2560
AB_KW = 2816
AB_VW = 3072
AB_G = 3328
AB_COLS = 3584

POS_SPLIT = 3
POS_TOK = HEAD_DIM
POS_CMP = POS_TOK + 2 * POS_SPLIT
MASK_BIG = 2.0 ** 100


def _pos_lanes(hi, lo, base):
    lane = lax.broadcasted_iota(jnp.int32, (1, LANES), 1)
    in_hi = (lane >= base) & (lane < base + POS_SPLIT)
    in_lo = (lane >= base + POS_SPLIT) & (lane < base + 2 * POS_SPLIT)
    return jnp.where(in_hi, hi, jnp.where(in_lo, lo, 0.0))


def _proj_ab_kernel(x_ref, g_ref, w_ref, qb_ref, dw_ref, db_ref, lg_ref, lb_ref,
                    a_ref, q_ref, kc_ref, vc_ref, ks_ref, vs_ref, kw_ref, vw_ref, gt_ref, ext_ref, shift_ref, *, tm):
    xn = _rmsnorm_rows(x_ref[0], g_ref[...]).astype(BF16)

    def seg(lo, hi):
        return _dot(xn, w_ref[:, lo:hi])

    @pl.when(pl.program_id(1) == 0)
    def _():
        ext_ref[0:CONV_HALO, :] = jnp.zeros((CONV_HALO, GROUP_WIDTH), F32)

    av = seg(AB_A, AB_A + 2 * GROUP_WIDTH)
    ext_ref[CONV_HALO:, :] = av[:, :GROUP_WIDTH] * _sigmoid(av[:, GROUP_WIDTH:])
    _conformer_conv_rows(ext_ref, shift_ref, dw_ref, db_ref, lg_ref, lb_ref, a_ref, tm)
    ext_ref[0:CONV_HALO, :] = ext_ref[tm:tm + CONV_HALO, :]
    q_ref[0] = (seg(AB_Q, AB_KC) + qb_ref[...]).astype(BF16)
    kc_ref[0] = seg(AB_KC, AB_VC)
    vc_ref[0] = seg(AB_VC, AB_KS)
    one_lane = jnp.where(lax.broadcasted_iota(jnp.int32, (1, NSA_KV_GROUPS * LANES), 1) % LANES == HEAD_DIM, 1.0, 0.0)
    vs_ref[0] = (seg(AB_VS, AB_KW) + one_lane).astype(BF16)
    vw_ref[0] = (seg(AB_VW, AB_G) + one_lane).astype(BF16)
    gt_ref[0] = _sigmoid(seg(AB_G, AB_COLS))

    tok = pl.program_id(1) * tm + lax.broadcasted_iota(jnp.int32, (tm, 1), 0)
    blk = lax.shift_right_logical(tok, int(np.log2(NSA_SEL_BLOCK)))
    pos = _pos_lanes(blk.astype(F32), (tok & (NSA_SEL_BLOCK - 1)).astype(F32), POS_TOK)
    onehot = jnp.where(lax.broadcasted_iota(jnp.int32, (1, LANES), 1) == blk, 1.0, 0.0).astype(BF16)
    ks = seg(AB_KS, AB_VS)
    kw = seg(AB_KW, AB_VW)
    for g in range(NSA_KV_GROUPS):
        ks_ref[0, :, 2 * g * LANES:(2 * g + 1) * LANES] = (ks[:, g * LANES:(g + 1) * LANES] + pos).astype(BF16)
        ks_ref[0, :, (2 * g + 1) * LANES:(2 * g + 2) * LANES] = onehot
        kw_ref[0, :, g * LANES:(g + 1) * LANES] = (kw[:, g * LANES:(g + 1) * LANES] + pos).astype(BF16)


def _arrange_ab_weight(w):
    d = w.shape[0]
    kvw = NSA_KV_GROUPS * HEAD_DIM
    o = 2 * GROUP_WIDTH
    a = w[:, :o]
    q = w[:, o:o + GROUP_WIDTH].reshape(d, NSA_HEADS, HEAD_DIM) * (QK_SCALE * LOG2_E)
    o += GROUP_WIDTH
    kc, vc, ks, vs, kw, vw = [w[:, o + i * kvw:o + (i + 1) * kvw] for i in range(6)]
    o += 6 * kvw
    g = w[:, o:].reshape(d, NSA_KV_GROUPS, NSA_REP * NSA_N_BRANCH)

    zeros_h = jnp.zeros((d, NSA_HEADS, HEAD_DIM), w.dtype)
    q_pad = jnp.concatenate([q, zeros_h], axis=-1).reshape(d, NSA_HEADS * LANES)

    def k_pad(k):
        k = k.reshape(d, NSA_KV_GROUPS, HEAD_DIM)
        return jnp.concatenate([k, jnp.zeros_like(k)], axis=-1).reshape(d, NSA_KV_GROUPS * LANES)

    g_pad = jnp.pad(g, ((0, 0), (0, 0), (0, LANES - g.shape[-1]))).reshape(d, NSA_KV_GROUPS * LANES)
    out = jnp.concatenate([a, q_pad, kc, vc, k_pad(ks), k_pad(vs), k_pad(kw), k_pad(vw), g_pad], axis=-1)
    assert out.shape[1] == AB_COLS
    return out.astype(BF16)


def _bf16_terms(x, n):
    terms, rest = [], np.asarray(x, np.float64)
    for _ in range(n):
        term = rest.astype(BF16).astype(np.float64)
        terms.append(term)
        rest = rest - term
    return terms


def _alibi_query_lanes():
    row = np.zeros((NSA_HEADS, LANES), np.float64)
    for h in range(NSA_HEADS):
        slope = 2.0 ** (-8.0 * (h + 1) / NSA_HEADS)
        for i, term in enumerate(_bf16_terms(slope * np.log2(np.e), POS_SPLIT)):
            row[h, POS_TOK + i] = NSA_SEL_BLOCK * term
            row[h, POS_TOK + POS_SPLIT + i] = term
            row[h, POS_CMP + i] = 2 * NSA_CMP_STRIDE * term
            row[h, POS_CMP + POS_SPLIT + i] = NSA_CMP_STRIDE * term
    return jnp.asarray(row.reshape(1, NSA_HEADS * LANES), F32)


def _proj_ab(x, g, w, dw_w, dw_b, ln_g, ln_b):
    b, t, d = x.shape
    tm = min(PROJ_TM, t)
    wa = _arrange_ab_weight(w)

    def tok(width):
        return pl.BlockSpec((1, tm, width), lambda bi, i: (bi, i, 0))

    const = lambda width: pl.BlockSpec((1, width), lambda bi, i: (0, 0))
    once = pl.Buffered(1)
    gw = GROUP_WIDTH
    kv = NSA_KV_GROUPS * LANES
    widths = [gw, NSA_HEADS * LANES, LANES, LANES, 2 * kv, kv, kv, kv, kv]
    dtypes = [BF16, BF16, F32, F32, BF16, BF16, BF16, BF16, F32]
    return pl.pallas_call(
        functools.partial(_proj_ab_kernel, tm=tm),
        out_shape=[jax.ShapeDtypeStruct((b, t, wd), dt) for wd, dt in zip(widths, dtypes)],
        grid=(b, t // tm),
        in_specs=[tok(d), const(d), pl.BlockSpec((d, AB_COLS), lambda bi, i: (0, 0), pipeline_mode=once),
                  const(NSA_HEADS * LANES), pl.BlockSpec((CONV_WIDTH, gw), lambda bi, i: (0, 0)),
                  const(gw), const(gw), const(gw)],
        out_specs=[tok(wd) for wd in widths],
        scratch_shapes=[pltpu.VMEM((tm + CONV_HALO, gw), F32), pltpu.VMEM((SUBLANES, tm + CONV_HALO, gw), F32)],
        compiler_params=_params("arbitrary", "arbitrary"),
        name="proj_ab",
    )(x, g.reshape(1, d), wa, _alibi_query_lanes(), dw_w, dw_b.reshape(1, gw), ln_g.reshape(1, gw),
      ln_b.reshape(1, gw))


CD_B = 0
CD_C = 512
CD_U = 1024
CD_Q = 1536
CD_K = 2048
CD_V = 2560
CD_COLS = 3072


def _proj_cd_kernel(x_ref, g_ref, w_ref, scw_ref, c_ref, q_ref, k_ref, v_ref, ext_ref, *, tm):
    xn = _rmsnorm_rows(x_ref[0], g_ref[...]).astype(BF16)

    def seg(lo, hi):
        return _dot(xn, w_ref[:, lo:hi])

    q_ref[0] = seg(CD_Q, CD_K).astype(BF16)
    k_ref[0] = seg(CD_K, CD_V).astype(BF16)
    v_ref[0] = seg(CD_V, CD_COLS).astype(BF16)

    @pl.when(pl.program_id(1) == 0)
    def _():
        ext_ref[0:SC_HALO, :] = jnp.zeros((SC_HALO, GROUP_WIDTH), F32)

    ext_ref[SC_HALO:, :] = seg(CD_C, CD_U) * seg(CD_U, CD_Q)
    off = SC_HALO - (SC_WIDTH - 1)
    acc = jnp.zeros((tm, GROUP_WIDTH), F32)
    for k in range(SC_WIDTH):
        acc = acc + scw_ref[k:k + 1, :] * ext_ref[off + k:off + k + tm, :]
    c_ref[0] = (seg(CD_B, CD_C) * acc).astype(c_ref.dtype)
    ext_ref[0:SC_HALO, :] = ext_ref[tm:tm + SC_HALO, :]


def _arrange_cd_weight(w):
    gw = GROUP_WIDTH
    out = jnp.concatenate([w[:, :3 * gw], w[:, 3 * gw:4 * gw] * (QK_SCALE * LOG2_E), w[:, 4 * gw:]], axis=-1)
    assert out.shape[1] == CD_COLS
    return out.astype(BF16)


def _proj_cd(x, g, w, sc_w):
    b, t, d = x.shape
    tm = min(PROJ_TM, t)
    wa = _arrange_cd_weight(w)

    def tok(width):
        return pl.BlockSpec((1, tm, width), lambda bi, i: (bi, i, 0))

    widths = [GROUP_WIDTH] * 4
    return pl.pallas_call(
        functools.partial(_proj_cd_kernel, tm=tm),
        out_shape=[jax.ShapeDtypeStruct((b, t, wd), BF16) for wd in widths],
        grid=(b, t // tm),
        in_specs=[tok(d), pl.BlockSpec((1, d), lambda bi, i: (0, 0)),
                  pl.BlockSpec((d, CD_COLS), lambda bi, i: (0, 0)),
                  pl.BlockSpec((SC_WIDTH, GROUP_WIDTH), lambda bi, i: (0, 0))],
        out_specs=[tok(wd) for wd in widths],
        scratch_shapes=[pltpu.VMEM((tm + SC_HALO, GROUP_WIDTH), F32)],
        compiler_params=_params("arbitrary", "arbitrary"),
        name="proj_cd",
    )(x, g.reshape(1, d), wa, sc_w)


def _conformer_conv_rows(ext_ref, shift_ref, w_ref, b_ref, lg_ref, lb_ref, o_ref, tt):
    off = CONV_HALO - (CONV_WIDTH - 1)
    last = off + CONV_WIDTH - 1
    steps = {phase: [m for m in range((last - phase) // SUBLANES + 1) if off <= SUBLANES * m + phase]
             for phase in range(SUBLANES)}
    for phase in range(SUBLANES):
        rows = SUBLANES * steps[phase][-1] + tt
        shift_ref[phase, 0:rows, :] = ext_ref[phase:phase + rows, :]
    for c in range(tt // CONV_CHUNK):
        r0 = c * CONV_CHUNK
        acc = jnp.zeros((CONV_CHUNK, GROUP_WIDTH), F32)
        for phase in range(SUBLANES):
            for m in steps[phase]:
                k = SUBLANES * m + phase - off
                lo = r0 + SUBLANES * m
                acc = acc + w_ref[k:k + 1, :] * shift_ref[phase, lo:lo + CONV_CHUNK, :]
        y = acc + b_ref[...]
        mu = jnp.mean(y, axis=-1, keepdims=True)
        yc = y - mu
        var = jnp.mean(yc * yc, axis=-1, keepdims=True)
        yn = yc * lax.rsqrt(var + LN_EPS) * lg_ref[...] + lb_ref[...]
        o_ref[0, r0:r0 + CONV_CHUNK, :] = (yn * _sigmoid(yn)).astype(o_ref.dtype)


def _gelu_tanh(x):
    return 0.5 * x * (1.0 + jnp.tanh(np.sqrt(2.0 / np.pi).astype(np.float32) * (x + 0.044715 * (x * x * x))))


def _compress_kernel(x_ref, pe_ref, w1_ref, w2_ref, o_ref, y_ref, *, nblk, keys):
    stride = NSA_CMP_STRIDE

    def partial_sum(l0):
        acc = None
        for l in range(l0, l0 + stride):
            xl = x_ref[0, pl.ds(l - l0, nblk, stride=stride), :] + pe_ref[l:l + 1, :]
            part = _dot(xl.astype(BF16), w1_ref[l])
            acc = part if acc is None else acc + part
        return acc

    h = partial_sum(0) + pltpu.roll(partial_sum(stride), nblk - 1, 0)
    row = lax.broadcasted_iota(jnp.int32, (nblk, 1), 0)
    ratio = NSA_SEL_BLOCK // NSA_CMP_STRIDE
    for g in range(NSA_KV_GROUPS):
        y = _dot(_gelu_tanh(h[:, g * NSA_CMP_HIDDEN:(g + 1) * NSA_CMP_HIDDEN]).astype(BF16), w2_ref[...])
        y = jnp.where(row < nblk - 1, y, 0.0)
        if keys:
            y = y + _pos_lanes(lax.shift_right_logical(row, 1).astype(F32), (row & 1).astype(F32), POS_CMP)
        y_ref[...] = y
        for r in range(ratio):
            o_ref[0, g, r] = y_ref[pl.ds(r, nblk // ratio, stride=ratio), :].astype(o_ref.dtype)


def _compress(kx, pe, w1, w2, dup):
    b, t, width = kx.shape
    g = NSA_KV_GROUPS
    nblk = t // NSA_CMP_STRIDE
    ratio = NSA_SEL_BLOCK // NSA_CMP_STRIDE
    hid = NSA_CMP_HIDDEN
    zeros = jnp.zeros_like(w1)
    w1_bd = jnp.concatenate([jnp.concatenate([w1, zeros], axis=-1), jnp.concatenate([zeros, w1], axis=-1)],
                            axis=1).astype(BF16)
    w2p = jnp.concatenate([w2, w2 if dup else jnp.zeros_like(w2)], axis=-1).astype(BF16)
    return pl.pallas_call(
        functools.partial(_compress_kernel, nblk=nblk, keys=not dup),
        out_shape=jax.ShapeDtypeStruct((b, g, ratio, nblk // ratio, LANES), BF16),
        grid=(b,),
        in_specs=[pl.BlockSpec((1, t, width), lambda bi: (bi, 0, 0)),
                  pl.BlockSpec((NSA_CMP_BLOCK, width), lambda bi: (0, 0)),
                  pl.BlockSpec((NSA_CMP_BLOCK, width, g * hid), lambda bi: (0, 0, 0)),
                  pl.BlockSpec((hid, LANES), lambda bi: (0, 0))],
        out_specs=pl.BlockSpec((1, g, ratio, nblk // ratio, LANES), lambda bi: (bi, 0, 0, 0, 0)),
        scratch_shapes=[pltpu.VMEM((nblk, LANES), F32)],
        compiler_params=_params("parallel"),
        name="nsa_compress",
    )(kx, jnp.tile(pe, (1, g)), w1_bd, w2p)


def _nsa_kernel(q_ref, kc_ref, vc_ref, ks_ref, vs_ref, kw_ref, vw_ref, gt_ref, o_ref, *, t_len, ns):
    tq, tk, rep = NSA_TQ, NSA_TK, NSA_REP
    rows = rep * tq
    i = pl.program_id(2)
    q0 = i * tq

    q2 = q_ref[0]
    qg = jnp.concatenate([q2[:, r * LANES:(r + 1) * LANES] for r in range(rep)], axis=0)
    tq_i = q0 + lax.broadcasted_iota(jnp.int32, (tq, 1), 0)

    def per_head(x):
        return jnp.concatenate([x] * rep, axis=0)

    lane_i = lax.broadcasted_iota(jnp.int32, (1, ns), 1)
    ratio = NSA_SEL_BLOCK // NSA_CMP_STRIDE
    n_cmp = t_len // NSA_CMP_STRIDE - 1
    s_list = []
    for r in range(ratio):
        c_i = lane_i * ratio + r
        mask = ((c_i * NSA_CMP_STRIDE + (NSA_CMP_BLOCK - 1)) <= tq_i) & (c_i < n_cmp)
        s_list.append(_dot_nt(qg, kc_ref[0, 0, r]) + per_head(jnp.where(mask, 0.0, NEG_INF)))
    m = jnp.max(functools.reduce(jnp.maximum, s_list), axis=-1, keepdims=True)
    p_list = [jnp.exp2(s - m) for s in s_list]
    l = jnp.sum(functools.reduce(lambda a, b: a + b, p_list), axis=-1, keepdims=True)
    any_visible = per_head(tq_i >= NSA_CMP_BLOCK - 1)
    inv = jnp.where(any_visible, 1.0 / l, 0.0)
    p_list = [p * inv for p in p_list]
    o_c = functools.reduce(lambda a, b: a + b,
                           [_dot(p.astype(BF16), vc_ref[0, 0, r]) for r, p in enumerate(p_list)])

    def head_sum(p):
        return functools.reduce(lambda a, b: a + b, [p[r * tq:(r + 1) * tq] for r in range(rep)])

    ps = [head_sum(p) for p in p_list]
    lane_q = lax.broadcasted_iota(jnp.int32, (tq, ns), 1)
    prev_last = jnp.where(lane_q == 0, 0.0, pltpu.roll(ps[ratio - 1], 1, 1))
    imp = prev_last + ps[0] + ps[1] + ps[2] + ps[3]
    cur =lax.shift_right_logical(tq_i, int(np.log2(NSA_SEL_BLOCK)))
    visible = lane_q * NSA_SEL_BLOCK <= tq_i
    forced = (lane_q == 0) | (lane_q == cur) | (lane_q == cur - 1)
    score = jnp.where(visible, jnp.where(forced, SEL_FORCE, imp), -1.0)

    wlen = NSA_WINDOW + tq
    w0 = pl.multiple_of(jnp.maximum(q0 - NSA_WINDOW, 0), tq)
    dist = tq_i - (w0 + lax.broadcasted_iota(jnp.int32, (1, wlen), 1))
    mask_w = (dist >= 0) & (dist < NSA_WINDOW)
    s_w = _dot_nt(qg, kw_ref[0, pl.ds(w0, wlen), :]) + per_head(jnp.where(mask_w, 0.0, NEG_INF))
    p_w = jnp.exp2(s_w - jnp.max(s_w, axis=-1, keepdims=True))
    o_w = _dot(p_w.astype(BF16), vw_ref[0, pl.ds(w0, wlen), :])
    o_w = o_w / o_w[:, HEAD_DIM:HEAD_DIM + 1]

    blk_f =lax.broadcasted_iota(jnp.int32, (ns, tq), 0).astype(F32)
    score_t = score.T
    work = jnp.where(score_t == SEL_FORCE, -2.0, score_t)
    for _ in range(min(NSA_TOP_N, ns) - NSA_FORCED):
        top = jnp.max(work, axis=0, keepdims=True)
        idx = jnp.min(jnp.where(work == top, blk_f, float(ns)), axis=0, keepdims=True)
        work = jnp.where(blk_f == idx, -2.0, work)
    sel = jnp.where((score >= 0.0) & (work.T == -2.0), 1.0, 0.0)

    unsel = ((sel - 1.0) * MASK_BIG).astype(BF16)
    if ns < LANES:
        unsel = jnp.concatenate([unsel, jnp.zeros((tq, LANES - ns), BF16)], axis=1)
    q_aug = jnp.concatenate([qg, jnp.concatenate([unsel] * rep, axis=0)], axis=1)

    def sel_tile(k0, causal):
        s = _dot_nt(q_aug, ks_ref[0, pl.ds(k0, tk), :])
        if causal:
            tok = k0 + lax.broadcasted_iota(jnp.int32, (1, tk), 1)
            s = s + per_head(jnp.where(tok <= tq_i, 0.0, -MASK_BIG))
        return s

    blocks_per_tile = tk // NSA_SEL_BLOCK
    tile_of_blk = lax.shift_right_logical(lax.broadcasted_iota(jnp.int32, (ns, LANES), 0),
                                          int(np.log2(blocks_per_tile)))
    lane_t = lax.broadcasted_iota(jnp.int32, (ns, LANES), 1)
    per_tile = _dot(sel.astype(BF16), jnp.where(tile_of_blk == lane_t, 1.0, 0.0).astype(BF16))
    tile_any = jnp.max(per_tile, axis=0, keepdims=True) > 0.0
    pow2 = lax.shift_left(jnp.ones((1, LANES), jnp.int32), lane_t[:1] & 15).astype(F32)
    tile_bits = jnp.sum(jnp.where(tile_any, pow2, 0.0), axis=-1, keepdims=True).astype(jnp.int32)[0, 0]

    n_past = lax.shift_right_logical(q0, int(np.log2(tk)))
    kd = pl.multiple_of(n_past * tk, tk)
    s = sel_tile(kd, True)
    m_s = jnp.max(s, axis=-1, keepdims=True)
    acc_s = _dot(jnp.exp2(s - m_s).astype(BF16), vs_ref[0, pl.ds(kd, tk), :])

    def sel_update(kt, carry, offset=None):
        m_run, acc = carry
        k0 = pl.multiple_of(kt * tk, tk)
        s = sel_tile(k0, False)
        if offset is not None:
            s = s + offset
        m_new = jnp.maximum(m_run, jnp.max(s, axis=-1, keepdims=True))
        acc = jnp.exp2(m_run - m_new) * acc + _dot(jnp.exp2(s - m_new).astype(BF16), vs_ref[0, pl.ds(k0, tk), :])
        return m_new, acc

    def next_active(state):
        return lax.while_loop(lambda st: (st[0] & 1) == 0,
                              lambda st: (lax.shift_right_logical(st[0], 1), st[1] + 1), state)

    def sel_body(state):
        pending, kt = next_active(state[:2])
        return (lax.shift_right_logical(pending, 1), kt + 1) + sel_update(kt, state[2:])

    newest = jnp.maximum(n_past - 1, 0)
    m_s, acc_s = sel_update(newest, (m_s, acc_s), offset=jnp.where(n_past > 0, 0.0, -MASK_BIG))
    past_bits = tile_bits & (lax.shift_left(jnp.int32(1), newest) - 1)
    _, _, _, acc_s = lax.while_loop(lambda st: st[0] != 0, sel_body, (past_bits, jnp.int32(0), m_s, acc_s))
    o_s = acc_s / acc_s[:, HEAD_DIM:HEAD_DIM + 1]

    gt = gt_ref[0]

    def gate(branch):
        return jnp.concatenate(
            [gt[:, r * NSA_N_BRANCH + branch:r * NSA_N_BRANCH + branch + 1] for r in range(rep)], axis=0)

    o = gate(0) * o_c + gate(1) * o_s + gate(2) * o_w
    low_half = lax.broadcasted_iota(jnp.int32, (tq, LANES), 1) < HEAD_DIM
    for c in range(rep // 2):
        even = o[(2 * c) * tq:(2 * c + 1) * tq]
        odd = pltpu.roll(o[(2 * c + 1) * tq:(2 * c + 2) * tq], HEAD_DIM, 1)
        o_ref[0, :, c * LANES:(c + 1) * LANES] = jnp.where(low_half, even, odd).astype(o_ref.dtype)


def _nsa(q, kc, vc, ks, vs, kw, vw, gates):
    b, t, _ = q.shape
    g = NSA_KV_GROUPS
    ns = t // NSA_SEL_BLOCK
    assert ns <= LANES and t >= NSA_WINDOW + NSA_TQ and t % NSA_TK == 0 and NSA_TK % NSA_TQ == 0
    assert t // NSA_TK <= 16
    ratio = NSA_SEL_BLOCK // NSA_CMP_STRIDE
    qw = NSA_REP * LANES
    cmp_spec = pl.BlockSpec((1, 1, ratio, ns, LANES), lambda bi, gi, i: (bi, gi, 0, 0, 0))
    kv_spec = pl.BlockSpec((1, t, LANES), lambda bi, gi, i: (bi, 0, gi))
    ks_spec = pl.BlockSpec((1, t, 2 * LANES), lambda bi, gi, i: (bi, 0, gi))
    return pl.pallas_call(
        functools.partial(_nsa_kernel, t_len=t, ns=ns),
        out_shape=jax.ShapeDtypeStruct((b, t, GROUP_WIDTH), BF16),
        grid=(b, g, t // NSA_TQ),
        in_specs=[pl.BlockSpec((1, NSA_TQ, qw), lambda bi, gi, i: (bi, i, gi)),
                  cmp_spec, cmp_spec, ks_spec, kv_spec, kv_spec, kv_spec,
                  pl.BlockSpec((1, NSA_TQ, LANES), lambda bi, gi, i: (bi, i, gi))],
        out_specs=pl.BlockSpec((1, NSA_TQ, NSA_REP * HEAD_DIM), lambda bi, gi, i: (bi, i, gi)),
        compiler_params=_params("parallel", "parallel", "arbitrary"),
        name="nsa_attention",
    )(q, kc, vc, ks, vs, kw, vw, gates)


def _sb_tiles(qs, k, v, u2, carries, mask):
    stage1 = []
    for h, qh in enumerate(qs):
        p = h // 2
        z = _dot_nt(qh, k[:, p * LANES:(p + 1) * LANES])
        nk = jnp.maximum(z, 0.0) + jnp.log2(1.0 + jnp.exp2(-jnp.abs(z)))
        if mask is not None:
            nk = jnp.where(mask, nk, 0.0)
        hi = nk.astype(BF16)
        lo = (nk - hi.astype(F32)).astype(BF16)
        stage1.append((z, jnp.concatenate([hi, lo], axis=1)))
    laters = [_dot(hilo, u2) for (_, hilo) in stage1]
    out = []
    for h, ((z, _), later, (acc, c)) in enumerate(zip(stage1, laters, carries)):
        p = h // 2
        a = jnp.exp2(z + later)
        if mask is not None:
            a = jnp.where(mask, a, 0.0)
        acc = acc + jnp.exp2(c) * _dot(a.astype(BF16), v[:, p * LANES:(p + 1) * LANES])
        out.append((acc, c + later[:, 0:1]))
    return tuple(out)


def _sb_kernel(q_ref, k_ref, v_ref, u_ref, o_ref):
    tile = SB_T
    i = pl.program_id(2)
    q2 = q_ref[0]
    u = u_ref[...]
    low_lanes = lax.broadcasted_iota(jnp.int32, (1, LANES), 1) < HEAD_DIM
    qs = [jnp.where(low_lanes if h % 2 == 0 else ~low_lanes, q2[:, (h // 2) * LANES:(h // 2 + 1) * LANES],
                    jnp.zeros((), BF16))
          for h in range(2 * SB_PAIRS)]
    q0 = pl.multiple_of(i * tile, tile)
    zero = (jnp.zeros((tile, LANES), F32), jnp.zeros((tile, 1), F32))
    mask = lax.broadcasted_iota(jnp.int32, (tile, tile), 1) < lax.broadcasted_iota(jnp.int32, (tile, tile), 0)
    carry = _sb_tiles(qs, k_ref[0, pl.ds(q0, tile), :], v_ref[0, pl.ds(q0, tile), :], u,
                      tuple(zero for _ in qs), mask)

    def body(jj, carry):
        k0 = pl.multiple_of(jnp.maximum(i - 1 - jj, 0) * tile, tile)
        return _sb_tiles(qs, k_ref[0, pl.ds(k0, tile), :], v_ref[0, pl.ds(k0, tile), :], u, carry, None)

    def alive(carry):
        c_max = functools.reduce(jnp.maximum, [c for _, c in carry])
        return jnp.max(c_max) > SB_DEAD_LOG2

    def step(state):
        jj, _, carry = state
        carry = body(jj, carry)
        return jj + 1, alive(carry), carry

    has_past = i > 0
    off_c = jnp.where(has_past, 0.0, -1e4)
    first = body(0, tuple((acc, c + off_c) for acc, c in carry))
    carry = tuple((acc1, jnp.where(has_past, c1, c0)) for (acc1, c1), (_, c0) in zip(first, carry))
    _, _, carry = lax.while_loop(lambda st: (st[0] < i) & st[1], step, (jnp.int32(1), alive(carry), carry))
    low_half = lax.broadcasted_iota(jnp.int32, (tile, LANES), 1) < HEAD_DIM
    for p in range(SB_PAIRS):
        o_ref[0, :, p * LANES:(p + 1) * LANES] = jnp.where(
            low_half, carry[2 * p][0], carry[2 * p + 1][0]).astype(o_ref.dtype)


def _stick_breaking(q, k, v):
    b, t, _ = q.shape
    tile = SB_T
    assert t % tile == 0
    u = -(np.arange(tile)[:, None] >= np.arange(tile)[None, :]).astype(np.float32)
    u2 = np.concatenate([u, u], axis=0)
    kv_spec = pl.BlockSpec((1, t, SB_PAIRS * LANES), lambda bi, hp, i: (bi, 0, hp))
    return pl.pallas_call(
        _sb_kernel,
        out_shape=jax.ShapeDtypeStruct((b, t, GROUP_WIDTH), BF16),
        grid=(b, SB_HEADS // (2 * SB_PAIRS), t // tile),
        in_specs=[pl.BlockSpec((1, tile, SB_PAIRS * LANES), lambda bi, hp, i: (bi, i, hp)),
                  kv_spec, kv_spec,
                  pl.BlockSpec((2 * tile, tile), lambda bi, hp, i: (0, 0))],
        out_specs=pl.BlockSpec((1, tile, SB_PAIRS * LANES), lambda bi, hp, i: (bi, i, hp)),
        compiler_params=_params("parallel", "parallel", "arbitrary"),
        name="stick_breaking",
    )(q, k, v, jnp.asarray(u2, BF16))


def _mixer_conv_nsa(x, norm_g, w_in, dw_w, dw_b, ln_g, ln_b, pe_k, w1_k, w2_k, pe_v, w1_v, w2_v):
    a, q, kc, vc, ks, vs, kw, vw, gates = _proj_ab(x, norm_g, w_in, dw_w, dw_b, ln_g, ln_b)
    k_cmp = _compress(kc, pe_k, w1_k, w2_k, dup=False)
    v_cmp = _compress(vc, pe_v, w1_v, w2_v, dup=True)
    return a, _nsa(q, k_cmp, v_cmp, ks, vs, kw, vw, gates)


def _mixer_shortconv_sb(x, norm_g, w_in, sc_w):
    c, q, k, v = _proj_cd(x, norm_g, w_in, sc_w)
    return c, _stick_breaking(q, k, v)


def kernel(x, ffn1_norm, ffn1_w_in, ffn1_w_out, mix_norm, ffn2_norm, ffn2_w_in, ffn2_w_out, ab_w_in, conv_dw_w, conv_dw_b, conv_ln_g, conv_ln_b, nsa_pe_k, nsa_w1_k, nsa_w2_k, nsa_pe_v, nsa_w1_v, nsa_w2_v, ab_w_out, cd_w_in, sc_conv_w, cd_w_out, final_norm):
    b, t, d = x.shape
    depth = ffn1_norm.shape[0]
    n = b * t
    ffn1_w = (ffn1_w_in.astype(BF16), ffn1_w_out)
    ffn2_w = (ffn2_w_in.astype(BF16), ffn2_w_out)
    for layer in range(depth):
        x = _ffn(x.reshape(n, d), ffn1_norm[layer], *ffn1_w, layer).reshape(b, t, d)
        if layer % 2 == 0:
            e = layer // 2
            left, right = _mixer_conv_nsa(x, mix_norm[layer], ab_w_in[e], conv_dw_w[e], conv_dw_b[e], conv_ln_g[e],
                                          conv_ln_b[e], nsa_pe_k[e], nsa_w1_k[e], nsa_w2_k[e],
                                          nsa_pe_v[e], nsa_w1_v[e], nsa_w2_v[e])
            w_mix = ab_w_out[e]
        else:
            o = layer // 2
            left, right = _mixer_shortconv_sb(x, mix_norm[layer], cd_w_in[o], sc_conv_w[o])
            w_mix = cd_w_out[o]
        last = layer == depth - 1
        x = _ffn(x.reshape(n, d), ffn2_norm[layer], *ffn2_w, layer,
                 mixer=(left.reshape(n, -1), right.reshape(n, -1), w_mix.astype(BF16)),
                 final_g=final_norm if last else None).reshape(b, t, d)
    return x
```
